```python
import jax, jax.numpy as jnp
from jax import lax
import numpy as np

D_MODEL = 1024
BATCH = 8
SEQ = 4096
DEPTH = 2
DEC_BATCH = 8
DEC_SEQ = 32
PAST_LEN = 2048

CHUNK = 64
HEAD_DIM = 128
H_SB = 4
H_FOX = 4
H_DSA = 8
KV_DSA = 2
H_IDX = 4
D_IDX = 64
TOPK_MAX = 256
N_GROUPS = 4
EXPERTS_PER_GROUP = 4
N_EXPERTS = N_GROUPS * EXPERTS_PER_GROUP
TOPK_IN_GROUP = 2
D_EXPERT = 256
ROPE_THETA = 10000.0
EPS = 1e-6
Q_BLOCK = 128
DSA_Q_BLOCK = 64
FORGET_BIAS_INIT = 3.0

AB_SPLITS = (H_SB * HEAD_DIM,) * 3 + (H_FOX * HEAD_DIM,) * 3 + (H_FOX,)
DSA_SPLITS = (H_DSA * HEAD_DIM, KV_DSA * HEAD_DIM, KV_DSA * HEAD_DIM, H_IDX * D_IDX, D_IDX, H_IDX)

kernel_name = 'hybrid_stickbreak_fox_dsa_hmoe_stream'


def split_cols(a, sizes):
    cuts = [int(c) for c in np.cumsum(sizes)[:-1]]
    return jnp.split(a, cuts, axis=-1)


def rms_norm(x, g):
    xf = x.astype(jnp.float32)
    y = xf * lax.rsqrt(jnp.mean(xf * xf, axis=-1, keepdims=True) + EPS)
    return (y * g.astype(jnp.float32)).astype(x.dtype)


def rotary(x, pos):
    half = x.shape[-1] // 2
    inv_freq = ROPE_THETA ** (-jnp.arange(half, dtype=jnp.float32) / half)
    ang = pos.astype(jnp.float32)[:, None] * inv_freq[None, :]
    cos = jnp.cos(ang)[None, :, None, :]
    sin = jnp.sin(ang)[None, :, None, :]
    xf = x.astype(jnp.float32)
    x1, x2 = xf[..., :half], xf[..., half:]
    return jnp.concatenate([x1 * cos - x2 * sin, x2 * cos + x1 * sin], axis=-1).astype(x.dtype)


def sweep_queries(fn, q_arrays, q_pos, block):
    T = q_pos.shape[0]
    blk = block if T % block == 0 else T
    nb = T // blk

    def split(a):
        return jnp.moveaxis(a.reshape(a.shape[0], nb, blk, *a.shape[2:]), 1, 0)

    def body(args):
        arrs, qp = args
        return fn(*arrs, qp)

    out = lax.map(body, (tuple(split(a) for a in q_arrays), q_pos.reshape(nb, blk)))
    out = jnp.moveaxis(out, 0, 1)
    return out.reshape(out.shape[0], T, *out.shape[3:])


def stick_breaking_block(q, k, v, q_pos, k_pos):
    z = jnp.einsum('bqhd,bkhd->bhqk', q, k, preferred_element_type=jnp.float32) * HEAD_DIM ** -0.5
    mask = k_pos[None, :] < q_pos[:, None]
    log_1m = jnp.where(mask, jax.nn.log_sigmoid(-z), 0.0)
    between = lax.cumsum(log_1m, axis=3, reverse=True) - log_1m
    log_a = jnp.where(mask, jax.nn.log_sigmoid(z) + between, -jnp.inf)
    a = jnp.exp(log_a)
    return jnp.einsum('bhqk,bkhd->bqhd', a.astype(v.dtype), v)


def forgetting_block(q, fq, k, v, fk, q_pos, k_pos):
    s = jnp.einsum('bqhd,bkhd->bhqk', q, k, preferred_element_type=jnp.float32) * HEAD_DIM ** -0.5
    s = s + jnp.swapaxes(fq, 1, 2)[:, :, :, None] - jnp.swapaxes(fk, 1, 2)[:, :, None, :]
    mask = k_pos[None, :] <= q_pos[:, None]
    p = jax.nn.softmax(jnp.where(mask, s, -jnp.inf), axis=-1)
    return jnp.einsum('bhqk,bkhd->bqhd', p.astype(v.dtype), v)


def mixer_ab(h, past, w_in, b_forget, w_out):
    B, T, _ = h.shape
    proj = jnp.einsum('btd,de->bte', h, w_in)
    q_sb, k_sb, v_sb, q_fx, k_fx, v_fx, f_logit = split_cols(proj, AB_SPLITS)
    q_sb = q_sb.reshape(B, T, H_SB, HEAD_DIM)
    k_sb = k_sb.reshape(B, T, H_SB, HEAD_DIM)
    v_sb = v_sb.reshape(B, T, H_SB, HEAD_DIM)
    q_fx = q_fx.reshape(B, T, H_FOX, HEAD_DIM)
    k_fx = k_fx.reshape(B, T, H_FOX, HEAD_DIM)
    v_fx = v_fx.reshape(B, T, H_FOX, HEAD_DIM)
    logf = jax.nn.log_sigmoid((f_logit + b_forget).astype(jnp.float32))
    rows = (k_sb, v_sb, k_fx, v_fx, logf)
    if past is None:
        keys = rows
    else:
        keys = tuple(jnp.concatenate([c, r], axis=1) for c, r in zip(past, rows))
    K_sb, V_sb, K_fx, V_fx, logf_all = keys
    L = K_sb.shape[1]
    k_pos = jnp.arange(L)
    q_pos = (L - T) + jnp.arange(T)
    F_all = lax.cumsum(logf_all.astype(jnp.float32), axis=1)
    F_q = F_all[:, L - T:]
    o_sb = sweep_queries(lambda q, qp: stick_breaking_block(q, K_sb, V_sb, qp, k_pos), (q_sb,), q_pos, Q_BLOCK)
    o_fx = sweep_queries(lambda q, fq, qp: forgetting_block(q, fq, K_fx, V_fx, F_all, qp, k_pos), (q_fx, F_q), q_pos, Q_BLOCK)
    o = jnp.concatenate([o_sb.reshape(B, T, -1), o_fx.reshape(B, T, -1).astype(o_sb.dtype)], axis=-1)
    return jnp.einsum('bte,ed->btd', o, w_out), rows


def mixer_dsa(h, past, w_in, w_out):
    B, T, _ = h.shape
    proj = jnp.einsum('btd,de->bte', h, w_in)
    q, k, v, q_idx, k_idx, w_idx = split_cols(proj, DSA_SPLITS)
    L = T if past is None else past[0].shape[1] + T
    q_pos = (L - T) + jnp.arange(T)
    q = rotary(q.reshape(B, T, H_DSA, HEAD_DIM), q_pos)
    k = rotary(k.reshape(B, T, KV_DSA, HEAD_DIM), q_pos)
    v = v.reshape(B, T, KV_DSA, HEAD_DIM)
    q_idx = rotary(q_idx.reshape(B, T, H_IDX, D_IDX), q_pos)
    k_idx = rotary(k_idx[:, :, None, :], q_pos)[:, :, 0, :]
    rows = (k, v, k_idx)
    if past is None:
        K, V, KI = rows
    else:
        K, V, KI = tuple(jnp.concatenate([c, r], axis=1) for c, r in zip(past, rows))
    k_chunk = jnp.arange(L) // CHUNK
    k_sel = min(TOPK_MAX, L // 4)

    def block(qb, qib, wib, qp):
        nq = qb.shape[1]
        q_chunk = qp // CHUNK
        logits = jnp.einsum('bqhe,bke->bqhk', qib, KI, preferred_element_type=jnp.float32) * D_IDX ** -0.5
        score = jnp.einsum('bqhk,bqh->bqk', jax.nn.relu(logits), wib.astype(jnp.float32) * H_IDX ** -0.5)
        score = jnp.where(k_chunk[None, None, :] <= q_chunk[None, :, None], score, -jnp.inf)
        _, idx = lax.top_k(score, k_sel)
        valid = (idx // CHUNK) <= q_chunk[None, :, None]
        k_g = jax.vmap(lambda kk, ii: kk[ii])(K, idx)
        v_g = jax.vmap(lambda vv, ii: vv[ii])(V, idx)
        qg = qb.reshape(B, nq, KV_DSA, H_DSA // KV_DSA, HEAD_DIM)
        s = jnp.einsum('bqgrd,bqkgd->bqgrk', qg, k_g, preferred_element_type=jnp.float32) * HEAD_DIM ** -0.5
        p = jax.nn.softmax(jnp.where(valid[:, :, None, None, :], s, -jnp.inf), axis=-1)
        o = jnp.einsum('bqgrk,bqkgd->bqgrd', p.astype(v_g.dtype), v_g)
        return o.reshape(B, nq, H_DSA, HEAD_DIM)

    o = sweep_queries(block, (q, q_idx, w_idx), q_pos, DSA_Q_BLOCK)
    return jnp.einsum('bte,ed->btd', o.reshape(B, T, -1), w_out), rows


def moe_tokens(h, w_group, w_router, w_gate, w_up, w_down):
    g_logits = jnp.einsum('nd,dg->ng', h, w_group, preferred_element_type=jnp.float32)
    g_sel = jnp.argmax(g_logits, axis=-1)
    p_group = jnp.max(jax.nn.softmax(g_logits, axis=-1), axis=-1, keepdims=True)
    e_all = jnp.einsum('nd,dge->nge', h, w_router, preferred_element_type=jnp.float32)
    e_logits = jnp.einsum('nge,ng->ne', e_all, jax.nn.one_hot(g_sel, N_GROUPS, dtype=jnp.float32))
    top_val, top_idx = lax.top_k(e_logits, TOPK_IN_GROUP)
    gates = jax.nn.softmax(top_val, axis=-1) * p_group
    expert = g_sel[:, None] * EXPERTS_PER_GROUP + top_idx
    combine = jnp.einsum('nk,nke->ne', gates, jax.nn.one_hot(expert, N_EXPERTS, dtype=jnp.float32))
    act = jax.nn.silu(jnp.einsum('nd,edf->nef', h, w_gate)) * jnp.einsum('nd,edf->nef', h, w_up)
    return jnp.einsum('nef,efd->nd', act * combine[:, :, None].astype(act.dtype), w_down)


def hier_moe(h, w_group, w_router, w_gate, w_up, w_down):
    return lax.map(lambda hb: moe_tokens(hb, w_group, w_router, w_gate, w_up, w_down), h)


def trunk(x, past_ab, past_dsa, norm_mix, norm_ffn, norm_final, w_in_ab, b_forget, w_out_ab,
          w_in_dsa, w_out_dsa, moe_w_group, moe_w_router, moe_w_gate, moe_w_up, moe_w_down):
    rows_ab, rows_dsa = None, None
    for layer in range(DEPTH):
        h = rms_norm(x, norm_mix[layer])
        if layer % 2 == 0:
            y, rows_ab = mixer_ab(h, past_ab, w_in_ab, b_forget, w_out_ab)
        else:
            y, rows_dsa = mixer_dsa(h, past_dsa, w_in_dsa, w_out_dsa)
        x = x + y.astype(x.dtype)
        h = rms_norm(x, norm_ffn[layer])
        x = x + hier_moe(h, moe_w_group[layer], moe_w_router[layer], moe_w_gate[layer],
                         moe_w_up[layer], moe_w_down[layer]).astype(x.dtype)
    return rms_norm(x, norm_final), rows_ab, rows_dsa


def setup_inputs(seed: int = 0) -> dict:
    key = jax.random.key(seed)
    ks = jax.random.split(key, 24)
    f32 = jnp.float32

    def nrm(k, shape, scale):
        return scale * jax.random.normal(k, shape, f32)

    ab_cols = sum(AB_SPLITS)
    dsa_cols = sum(DSA_SPLITS)
    return {
        'x_prompt': nrm(ks[0], (BATCH, SEQ, D_MODEL), 1.0),
        'x_sample': nrm(ks[1], (DEC_BATCH, DEC_SEQ, D_MODEL), 1.0),
        'cache_sb_k': nrm(ks[2], (DEC_BATCH, PAST_LEN, H_SB, HEAD_DIM), 1.0),
        'cache_sb_v': nrm(ks[3], (DEC_BATCH, PAST_LEN, H_SB, HEAD_DIM), 1.0),
        'cache_fox_k': nrm(ks[4], (DEC_BATCH, PAST_LEN, H_FOX, HEAD_DIM), 1.0),
        'cache_fox_v': nrm(ks[5], (DEC_BATCH, PAST_LEN, H_FOX, HEAD_DIM), 1.0),
        'cache_fox_logf': jax.nn.log_sigmoid(FORGET_BIAS_INIT + nrm(ks[6], (DEC_BATCH, PAST_LEN, H_FOX), 1.0)),
        'cache_dsa_k': nrm(ks[7], (DEC_BATCH, PAST_LEN, KV_DSA, HEAD_DIM), 1.0),
        'cache_dsa_v': nrm(ks[8], (DEC_BATCH, PAST_LEN, KV_DSA, HEAD_DIM), 1.0),
        'cache_dsa_idx_k': nrm(ks[9], (DEC_BATCH, PAST_LEN, D_IDX), 1.0),
        'norm_mix': 1.0 + nrm(ks[10], (DEPTH, D_MODEL), 0.05),
        'norm_ffn': 1.0 + nrm(ks[11], (DEPTH, D_MODEL), 0.05),
        'norm_final': 1.0 + nrm(ks[12], (D_MODEL,), 0.05),
        'w_in_ab': nrm(ks[13], (D_MODEL, ab_cols), D_MODEL ** -0.5),
        'b_forget': FORGET_BIAS_INIT + nrm(ks[14], (H_FOX,), 0.5),
        'w_out_ab': nrm(ks[15], ((H_SB + H_FOX) * HEAD_DIM, D_MODEL), ((H_SB + H_FOX) * HEAD_DIM) ** -0.5),
        'w_in_dsa': nrm(ks[16], (D_MODEL, dsa_cols), D_MODEL ** -0.5),
        'w_out_dsa': nrm(ks[17], (H_DSA * HEAD_DIM, D_MODEL), (H_DSA * HEAD_DIM) ** -0.5),
        'moe_w_group': nrm(ks[18], (DEPTH, D_MODEL, N_GROUPS), D_MODEL ** -0.5),
        'moe_w_router': nrm(ks[19], (DEPTH, D_MODEL, N_GROUPS, EXPERTS_PER_GROUP), D_MODEL ** -0.5),
        'moe_w_gate': nrm(ks[20], (DEPTH, N_EXPERTS, D_MODEL, D_EXPERT), D_MODEL ** -0.5),
        'moe_w_up': nrm(ks[21], (DEPTH, N_EXPERTS, D_MODEL, D_EXPERT), D_MODEL ** -0.5),
        'moe_w_down': nrm(ks[22], (DEPTH, N_EXPERTS, D_EXPERT, D_MODEL), D_EXPERT ** -0.5),
    }


def reference(x_prompt, x_sample, cache_sb_k, cache_sb_v, cache_fox_k, cache_fox_v, cache_fox_logf,
              cache_dsa_k, cache_dsa_v, cache_dsa_idx_k, norm_mix, norm_ffn, norm_final,
              w_in_ab, b_forget, w_out_ab, w_in_dsa, w_out_dsa,
              moe_w_group, moe_w_router, moe_w_gate, moe_w_up, moe_w_down):
    y_prompt, rows_ab_p, rows_dsa_p = trunk(
        x_prompt, None, None, norm_mix, norm_ffn, norm_final, w_in_ab, b_forget, w_out_ab,
        w_in_dsa, w_out_dsa, moe_w_group, moe_w_router, moe_w_gate, moe_w_up, moe_w_down)
    y_sample, rows_ab_s, rows_dsa_s = trunk(
        x_sample, (cache_sb_k, cache_sb_v, cache_fox_k, cache_fox_v, cache_fox_logf),
        (cache_dsa_k, cache_dsa_v, cache_dsa_idx_k), norm_mix, norm_ffn, norm_final,
        w_in_ab, b_forget, w_out_ab, w_in_dsa, w_out_dsa,
        moe_w_group, moe_w_router, moe_w_gate, moe_w_up, moe_w_down)
    p_sb_k, p_sb_v, p_fox_k, p_fox_v, p_fox_logf = rows_ab_p
    p_dsa_k, p_dsa_v, p_dsa_idx_k = rows_dsa_p
    s_sb_k, s_sb_v, s_fox_k, s_fox_v, s_fox_logf = rows_ab_s
    s_dsa_k, s_dsa_v, s_dsa_idx_k = rows_dsa_s
    return (y_prompt, y_sample,
            p_sb_k, p_sb_v, p_fox_k, p_fox_v, p_fox_logf, p_dsa_k, p_dsa_v, p_dsa_idx_k,
            s_sb_k, s_sb_v, s_fox_k, s_fox_v, s_fox_logf, s_dsa_k, s_dsa_v, s_dsa_idx_k)
```

```python
import functools

import jax
import jax.numpy as jnp
import numpy as np
from jax import lax
from jax.experimental import pallas as pl
from jax.experimental.pallas import tpu as pltpu

CHUNK = 64
HEAD_DIM = 128
H_SB = 4
H_FOX = 4
H_DSA = 8
KV_DSA = 2
H_IDX = 4
D_IDX = 64
TOPK_MAX = 256
ROPE_THETA = 10000.0
EPS = 1e-6

LANES = 128
SUBLANES = 8

KEY_PAD = 512
NEG_BIG = -1e30
INT32_MIN = -(2 ** 31)
NEG_INF_KEY = int(np.int32(np.uint32(0xFF800000) ^ np.uint32(0x7FFFFFFF)))

F32 = jnp.float32
BF16 = jnp.bfloat16
NT_DIMS = (((1,), (1,)), ((), ()))


def _round_up(a, b):
    return (a + b - 1) // b * b


def _cparams(semantics, vmem_mib=48):
    return pltpu.CompilerParams(dimension_semantics=semantics,
                                vmem_limit_bytes=vmem_mib * 1024 * 1024)


def _rms(x, g):
    return x * lax.rsqrt(jnp.mean(x * x, axis=-1, keepdims=True) + EPS) * g


def _log_sigmoid(z):
    return jnp.minimum(z, 0.0) - jnp.log1p(jnp.exp(-jnp.abs(z)))


def _norm_matmul_kernel(x_ref, g_ref, w_ref, of_ref, ob_ref, h_scr):
    @pl.when(pl.program_id(1) == 0)
    def _():
        h_scr[...] = _rms(x_ref[...], g_ref[...]).astype(BF16)

    acc = jnp.dot(h_scr[...], w_ref[...], preferred_element_type=F32)
    of_ref[...] = acc
    ob_ref[...] = acc.astype(BF16)


def norm_matmul(x, g, w, tn=256):
    n, d = x.shape
    e = w.shape[1]
    tm = min(512, n)
    return pl.pallas_call(
        _norm_matmul_kernel,
        grid=(n // tm, e // tn),
        in_specs=[pl.BlockSpec((tm, d), lambda i, j: (i, 0)),
                  pl.BlockSpec((1, d), lambda i, j: (0, 0)),
                  pl.BlockSpec((d, tn), lambda i, j: (0, j))],
        out_specs=[pl.BlockSpec((tm, tn), lambda i, j: (i, j)),
                   pl.BlockSpec((tm, tn), lambda i, j: (i, j))],
        out_shape=[jax.ShapeDtypeStruct((n, e), F32),
                   jax.ShapeDtypeStruct((n, e), BF16)],
        scratch_shapes=[pltpu.VMEM((tm, d), BF16)],
        compiler_params=_cparams(("parallel", "arbitrary")),
    )(x, g.reshape(1, d), w)


def _matmul_res_kernel(a_ref, w_ref, r_ref, o_ref):
    o_ref[...] = r_ref[...] + jnp.dot(a_ref[...], w_ref[...], preferred_element_type=F32)


def matmul_res(a, w, res):
    n, k = a.shape
    d = w.shape[1]
    tm = min(512, n)
    return pl.pallas_call(
        _matmul_res_kernel,
        grid=(n // tm,),
        in_specs=[pl.BlockSpec((tm, k), lambda i: (i, 0)),
                  pl.BlockSpec((k, d), lambda i: (0, 0)),
                  pl.BlockSpec((tm, d), lambda i: (i, 0))],
        out_specs=pl.BlockSpec((tm, d), lambda i: (i, 0)),
        out_shape=jax.ShapeDtypeStruct((n, d), F32),
        compiler_params=_cparams(("parallel",)),
    )(a, w, res)


def _split3(x):
    hi = x.astype(BF16)
    r1 = x - hi.astype(F32)
    mid = r1.astype(BF16)
    lo = (r1 - mid.astype(F32)).astype(BF16)
    return hi, mid, lo


def _logf_cumsum_kernel(pre_ref, b_ref, logf_ref, cum_ref, carry, *, past, length, cw):
    j = pl.program_id(0)

    @pl.when(j == 0)
    def _():
        carry[...] = jnp.zeros_like(carry)

    pre = pre_ref[...]
    pos = j * cw + lax.broadcasted_iota(jnp.int32, pre.shape, 1)
    x = jnp.where(pos >= past, _log_sigmoid(pre + b_ref[...]), pre)
    x = jnp.where(pos < length, x, 0.0)
    logf_ref[...] = x
    row = lax.broadcasted_iota(jnp.int32, (cw, cw), 0)
    col = lax.broadcasted_iota(jnp.int32, (cw, cw), 1)
    upper = jnp.where(row <= col, 1.0, 0.0).astype(BF16)
    hi, mid, lo = _split3(x)
    cs = (jnp.dot(hi, upper, preferred_element_type=F32)
          + jnp.dot(mid, upper, preferred_element_type=F32)
          + jnp.dot(lo, upper, preferred_element_type=F32))
    cum = cs + carry[:, 0:1]
    cum_ref[...] = cum
    carry[...] = jnp.broadcast_to(cum[:, cw - 1:cw], carry.shape)


def logf_cumsum(pre, b_col, past, length):
    r, lp = pre.shape
    cw = KEY_PAD
    return pl.pallas_call(
        functools.partial(_logf_cumsum_kernel, past=past, length=length, cw=cw),
        grid=(lp // cw,),
        in_specs=[pl.BlockSpec((r, cw), lambda j: (0, j)),
                  pl.BlockSpec((r, 1), lambda j: (0, 0))],
        out_specs=[pl.BlockSpec((r, cw), lambda j: (0, j)),
                   pl.BlockSpec((r, cw), lambda j: (0, j))],
        out_shape=[jax.ShapeDtypeStruct((r, lp), F32),
                   jax.ShapeDtypeStruct((r, lp), F32)],
        scratch_shapes=[pltpu.VMEM((r, LANES), F32)],
        compiler_params=_cparams(("arbitrary",)),
    )(pre, b_col)


def _sb_kernel(q_ref, k_ref, v_ref, o_ref, acc, carry, *, off, tq, tk, nsteps, heads):
    qi = pl.program_id(1)
    j = pl.program_id(2)
    kb_last = (off + qi * tq + tq - 2) // tk
    scale = HEAD_DIM ** -0.5

    @pl.when(j == 0)
    def _():
        acc[...] = jnp.zeros_like(acc)
        carry[...] = jnp.zeros_like(carry)

    @pl.when(j <= kb_last)
    def _():
        kb = kb_last - j
        qpos = off + qi * tq + lax.broadcasted_iota(jnp.int32, (tq, 1), 0)
        kpos = kb * tk + lax.broadcasted_iota(jnp.int32, (1, tk), 1)
        mask = kpos < qpos
        row = lax.broadcasted_iota(jnp.int32, (LANES, LANES), 0)
        col = lax.broadcasted_iota(jnp.int32, (LANES, LANES), 1)
        lower = jnp.where(row >= col, 1.0, 0.0).astype(BF16)
        for h in range(heads):
            sl = slice(h * HEAD_DIM, (h + 1) * HEAD_DIM)
            z = lax.dot_general(q_ref[0, :, sl], k_ref[0, :, sl], NT_DIMS,
                                preferred_element_type=F32) * scale
            ls = _log_sigmoid(z)
            l1m = jnp.where(mask, ls - z, 0.0)
            run = carry[h][:, 0:1]
            parts = [None] * (tk // LANES)
            for s in reversed(range(tk // LANES)):
                xs = l1m[:, s * LANES:(s + 1) * LANES]
                hi = xs.astype(BF16)
                lo = (xs - hi.astype(F32)).astype(BF16)
                rc = (jnp.dot(hi, lower, preferred_element_type=F32)
                      + jnp.dot(lo, lower, preferred_element_type=F32))
                parts[s] = rc - xs + run
                run = run + rc[:, 0:1]
            between = jnp.concatenate(parts, axis=1)
            a = jnp.where(mask, jnp.exp(ls + between), 0.0)
            acc[h] += jnp.dot(a.astype(BF16), v_ref[0, :, sl], preferred_element_type=F32)
            carry[h] = jnp.broadcast_to(run, (tq, LANES))

    @pl.when(j == nsteps - 1)
    def _():
        for h in range(heads):
            o_ref[0, :, h * HEAD_DIM:(h + 1) * HEAD_DIM] = acc[h].astype(o_ref.dtype)


def sb_attention(q_arr, qc, k_arr, kc, v_arr, vc, t, length, tq, tk):
    b = q_arr.shape[0]
    w = H_SB * HEAD_DIM
    off = length - t
    nq = t // tq
    last = lambda qi: (off + qi * tq + tq - 2) // tk
    nsteps = last(nq - 1) + 1
    kmap = lambda col: (lambda bi, qi, j: (bi, jnp.maximum(last(qi) - j, 0), col))
    return pl.pallas_call(
        functools.partial(_sb_kernel, off=off, tq=tq, tk=tk, nsteps=nsteps, heads=H_SB),
        grid=(b, nq, nsteps),
        in_specs=[pl.BlockSpec((1, tq, w), lambda bi, qi, j: (bi, qi, qc)),
                  pl.BlockSpec((1, tk, w), kmap(kc)),
                  pl.BlockSpec((1, tk, w), kmap(vc))],
        out_specs=pl.BlockSpec((1, tq, w), lambda bi, qi, j: (bi, qi, 0)),
        out_shape=jax.ShapeDtypeStruct((b, t, w), BF16),
        scratch_shapes=[pltpu.VMEM((H_SB, tq, HEAD_DIM), F32),
                        pltpu.VMEM((H_SB, tq, LANES), F32)],
        compiler_params=_cparams(("parallel", "parallel", "arbitrary")),
    )(q_arr, k_arr, v_arr)


def _fox_kernel(q_ref, k_ref, v_ref, fq_ref, fk_ref, o_ref, acc, mrun, lrun,
                *, off, tq, tk, nsteps, heads):
    qi = pl.program_id(1)
    j = pl.program_id(2)
    kb_last = (off + qi * tq + tq - 1) // tk
    scale = HEAD_DIM ** -0.5

    @pl.when(j == 0)
    def _():
        acc[...] = jnp.zeros_like(acc)
        mrun[...] = jnp.full_like(mrun, NEG_BIG)
        lrun[...] = jnp.zeros_like(lrun)

    @pl.when(j <= kb_last)
    def _():
        qpos = off + qi * tq + lax.broadcasted_iota(jnp.int32, (tq, 1), 0)
        kpos = j * tk + lax.broadcasted_iota(jnp.int32, (1, tk), 1)
        mask = kpos <= qpos
        fq = fq_ref[0]
        fk = fk_ref[0]
        for h in range(heads):
            sl = slice(h * HEAD_DIM, (h + 1) * HEAD_DIM)
            s = lax.dot_general(q_ref[0, :, sl], k_ref[0, :, sl], NT_DIMS,
                                preferred_element_type=F32) * scale
            s = s + fq[:, h:h + 1] - fk[h:h + 1, :]
            s = jnp.where(mask, s, NEG_BIG)
            m_old = mrun[h][:, 0:1]
            m_new = jnp.maximum(m_old, jnp.max(s, axis=-1, keepdims=True))
            p = jnp.where(mask, jnp.exp(s - m_new), 0.0)
            alpha = jnp.exp(m_old - m_new)
            l_new = alpha * lrun[h][:, 0:1] + jnp.sum(p, axis=-1, keepdims=True)
            acc[h] = alpha * acc[h] + jnp.dot(p.astype(BF16), v_ref[0, :, sl],
                                              preferred_element_type=F32)
            mrun[h] = jnp.broadcast_to(m_new, (tq, LANES))
            lrun[h] = jnp.broadcast_to(l_new, (tq, LANES))

    @pl.when(j == nsteps - 1)
    def _():
        for h in range(heads):
            o_ref[0, :, h * HEAD_DIM:(h + 1) * HEAD_DIM] = (
                acc[h] / lrun[h][:, 0:1]).astype(o_ref.dtype)


def fox_attention(q_arr, qc, k_arr, kc, v_arr, vc, f_q, f_kt, t, length, tq, tk):
    b = q_arr.shape[0]
    w = H_FOX * HEAD_DIM
    off = length - t
    nq = t // tq
    last = lambda qi: (off + qi * tq + tq - 1) // tk
    nsteps = last(nq - 1) + 1
    kblk = lambda qi, j: jnp.minimum(j, last(qi))
    return pl.pallas_call(
        functools.partial(_fox_kernel, off=off, tq=tq, tk=tk, nsteps=nsteps, heads=H_FOX),
        grid=(b, nq, nsteps),
        in_specs=[pl.BlockSpec((1, tq, w), lambda bi, qi, j: (bi, qi, qc)),
                  pl.BlockSpec((1, tk, w), lambda bi, qi, j: (bi, kblk(qi, j), kc)),
                  pl.BlockSpec((1, tk, w), lambda bi, qi, j: (bi, kblk(qi, j), vc)),
                  pl.BlockSpec((1, tq, H_FOX), lambda bi, qi, j: (bi, qi, 0)),
                  pl.BlockSpec((1, H_FOX, tk), lambda bi, qi, j: (bi, 0, kblk(qi, j)))],
        out_specs=pl.BlockSpec((1, tq, w), lambda bi, qi, j: (bi, qi, 0)),
        out_shape=jax.ShapeDtypeStruct((b, t, w), BF16),
        scratch_shapes=[pltpu.VMEM((H_FOX, tq, HEAD_DIM), F32),
                        pltpu.VMEM((H_FOX, tq, LANES), F32),
                        pltpu.VMEM((H_FOX, tq, LANES), F32)],
        compiler_params=_cparams(("parallel", "parallel", "arbitrary")),
    )(q_arr, k_arr, v_arr, f_q, f_kt)


def _moe_kernel(x_ref, g_ref, wr_ref, wg_ref, wu_ref, wd_ref, gf_ref, o_ref, hb, comb,
                *, n_groups, per_group, final_norm):
    e = pl.program_id(1)
    n_exp = n_groups * per_group
    lane = lax.broadcasted_iota(jnp.int32, comb.shape, 1)

    @pl.when(e == 0)
    def _():
        x = x_ref[...]
        h = _rms(x, g_ref[...])
        hb[...] = h.astype(BF16)
        logits = jnp.dot(h, wr_ref[...], precision=lax.Precision.HIGHEST,
                         preferred_element_type=F32)
        gl = jnp.where(lane < n_groups, logits, -jnp.inf)
        gmax = jnp.max(gl, axis=-1, keepdims=True)
        gsel = jnp.min(jnp.where(gl == gmax, lane, LANES), axis=-1, keepdims=True)
        p_group = 1.0 / jnp.sum(jnp.where(lane < n_groups, jnp.exp(gl - gmax), 0.0),
                                axis=-1, keepdims=True)
        first = n_groups + gsel * per_group
        el = jnp.where((lane >= first) & (lane < first + per_group), logits, -jnp.inf)
        v1 = jnp.max(el, axis=-1, keepdims=True)
        i1 = jnp.min(jnp.where(el == v1, lane, LANES), axis=-1, keepdims=True)
        el2 = jnp.where(lane == i1, -jnp.inf, el)
        v2 = jnp.max(el2, axis=-1, keepdims=True)
        i2 = jnp.min(jnp.where(el2 == v2, lane, LANES), axis=-1, keepdims=True)
        ratio = jnp.exp(v2 - v1)
        gate1 = p_group / (1.0 + ratio)
        comb[...] = jnp.where(lane == i1, gate1, jnp.where(lane == i2, gate1 * ratio, 0.0))
        o_ref[...] = x

    c = jnp.sum(jnp.where(lane == e + n_groups, comb[...], 0.0), axis=-1, keepdims=True)
    hv = hb[...]
    gate = jnp.dot(hv, wg_ref[0], preferred_element_type=F32)
    up = jnp.dot(hv, wu_ref[0], preferred_element_type=F32)
    act = gate * (1.0 / (1.0 + jnp.exp(-gate))) * up * c
    o_ref[...] += jnp.dot(act.astype(BF16), wd_ref[0], preferred_element_type=F32)

    if final_norm:
        @pl.when(e == n_exp - 1)
        def _():
            o_ref[...] = _rms(o_ref[...], gf_ref[...])


def moe_block(x, g, w_group, w_router, w_gate, w_up, w_down, g_final=None):
    n, d = x.shape
    n_groups, per_group = w_router.shape[1], w_router.shape[2]
    n_exp, _, f = w_gate.shape
    w_route = jnp.concatenate(
        [w_group, w_router.reshape(d, n_exp),
         jnp.zeros((d, LANES - n_groups - n_exp), F32)], axis=1)
    final_norm = g_final is not None
    gf = (g_final if final_norm else g).reshape(1, d)
    tm = min(1024, n)
    return pl.pallas_call(
        functools.partial(_moe_kernel, n_groups=n_groups, per_group=per_group,
                          final_norm=final_norm),
        grid=(n // tm, n_exp),
        in_specs=[pl.BlockSpec((tm, d), lambda i, e: (i, 0)),
                  pl.BlockSpec((1, d), lambda i, e: (0, 0)),
                  pl.BlockSpec((d, LANES), lambda i, e: (0, 0)),
                  pl.BlockSpec((1, d, f), lambda i, e: (e, 0, 0)),
                  pl.BlockSpec((1, d, f), lambda i, e: (e, 0, 0)),
                  pl.BlockSpec((1, f, d), lambda i, e: (e, 0, 0)),
                  pl.BlockSpec((1, d), lambda i, e: (0, 0))],
        out_specs=pl.BlockSpec((tm, d), lambda i, e: (i, 0)),
        out_shape=jax.ShapeDtypeStruct((n, d), F32),
        scratch_shapes=[pltpu.VMEM((tm, d), BF16), pltpu.VMEM((tm, LANES), F32)],
        compiler_params=_cparams(("parallel", "arbitrary")),
    )(x, g.reshape(1, d), w_route, w_gate.astype(BF16), w_up.astype(BF16),
      w_down.astype(BF16), gf)


def _rotary_kernel(x_ref, c_ref, s_ref, of_ref, ob_ref, *, half, width):
    cos = c_ref[...]
    sin = s_ref[...]
    lane = lax.broadcasted_iota(jnp.int32, cos.shape, 1)
    for gi in range(width // LANES):
        sl = slice(gi * LANES, (gi + 1) * LANES)
        xs = x_ref[0, :, sl]
        if 2 * half == LANES:
            partner = pltpu.roll(xs, half, 1)
        else:
            partner = jnp.where(lane % (2 * half) < half,
                                pltpu.roll(xs, LANES - half, 1), pltpu.roll(xs, half, 1))
        o = xs * cos + partner * sin
        of_ref[0, :, sl] = o
        ob_ref[0, :, sl] = o.astype(BF16)


def _rope_tables(pos, head_dim):
    half = head_dim // 2
    inv_freq = ROPE_THETA ** (-jnp.arange(half, dtype=F32) / half)
    ang = pos.astype(F32)[:, None] * inv_freq[None, :]
    cos, sin = jnp.cos(ang), jnp.sin(ang)
    reps = LANES // head_dim
    return (jnp.tile(jnp.concatenate([cos, cos], axis=1), (1, reps)),
            jnp.tile(jnp.concatenate([-sin, sin], axis=1), (1, reps)))


def rotary(x3, colblock, width, head_dim, pos):
    b, t, _ = x3.shape
    tm = min(512, t)
    cos, sin = _rope_tables(pos, head_dim)
    return pl.pallas_call(
        functools.partial(_rotary_kernel, half=head_dim // 2, width=width),
        grid=(b, t // tm),
        in_specs=[pl.BlockSpec((1, tm, width), lambda bi, i: (bi, i, colblock)),
                  pl.BlockSpec((tm, LANES), lambda bi, i: (i, 0)),
                  pl.BlockSpec((tm, LANES), lambda bi, i: (i, 0))],
        out_specs=[pl.BlockSpec((1, tm, width), lambda bi, i: (bi, i, 0)),
                   pl.BlockSpec((1, tm, width), lambda bi, i: (bi, i, 0))],
        out_shape=[jax.ShapeDtypeStruct((b, t, width), F32),
                   jax.ShapeDtypeStruct((b, t, width), BF16)],
        compiler_params=_cparams(("parallel", "parallel")),
    )(x3, cos, sin)


def _sort_key(score):
    bits = pltpu.bitcast(score + 0.0, jnp.int32)
    return bits ^ ((bits >> 31) & 0x7FFFFFFF)


def _dsa_mask_kernel(qi_ref, w_ref, ki_ref, m_ref, keys, *, off, length, tq, lp, ksel, cw):
    qb = pl.program_id(1)
    qchunk = (off + qb * tq + lax.broadcasted_iota(jnp.int32, (tq, 1), 0)) // CHUNK
    w = w_ref[0] * (H_IDX ** -0.5)
    ngroups = lp // LANES

    for c in range(lp // cw):
        kk = ki_ref[0, c * cw:(c + 1) * cw, :]
        score = jnp.zeros((tq, cw), F32)
        for h in range(H_IDX):
            lg = lax.dot_general(qi_ref[0, :, h * D_IDX:(h + 1) * D_IDX], kk, NT_DIMS,
                                 preferred_element_type=F32) * (D_IDX ** -0.5)
            score = score + jnp.maximum(lg, 0.0) * w[:, h:h + 1]
        kpos = c * cw + lax.broadcasted_iota(jnp.int32, (1, cw), 1)
        admissible = (kpos // CHUNK <= qchunk) & (kpos < length)
        keys[:, c * cw:(c + 1) * cw] = _sort_key(jnp.where(admissible, score, -jnp.inf))

    def count(pred):
        tot = jnp.zeros((tq, LANES), F32)
        for s in range(ngroups):
            tot = tot + jnp.where(pred(keys[:, s * LANES:(s + 1) * LANES]), 1.0, 0.0)
        return jnp.sum(tot, axis=-1, keepdims=True)

    kf = float(ksel)
    t0 = jnp.where(count(lambda x: x >= 0) >= kf, 0, INT32_MIN).astype(jnp.int32)

    def bit_step(i, t):
        cand = t + lax.shift_left(jnp.int32(1), 30 - i)
        return jnp.where(count(lambda x: x >= cand) >= kf, cand, t)

    thr = lax.fori_loop(0, 31, bit_step, t0)
    n_ge = count(lambda x: x >= thr)
    n_gt = count(lambda x: x > thr)
    tied = jnp.max(jnp.where((n_ge > kf) & (thr > NEG_INF_KEY), 1.0, 0.0)) > 0.0

    @pl.when(jnp.logical_not(tied))
    def _():
        floor = jnp.maximum(thr, NEG_INF_KEY + 1)
        for s in range(ngroups):
            sl = slice(s * LANES, (s + 1) * LANES)
            m_ref[0, :, sl] = jnp.where(keys[:, sl] >= floor, 1, 0).astype(m_ref.dtype)

    @pl.when(tied)
    def _():
        need = kf - n_gt
        row = lax.broadcasted_iota(jnp.int32, (LANES, LANES), 0)
        col = lax.broadcasted_iota(jnp.int32, (LANES, LANES), 1)
        before = jnp.where(row < col, 1.0, 0.0).astype(BF16)
        run = jnp.zeros((tq, 1), F32)
        for s in range(ngroups):
            sl = slice(s * LANES, (s + 1) * LANES)
            ks = keys[:, sl]
            eq = jnp.where(ks == thr, 1.0, 0.0)
            rank = run + jnp.dot(eq.astype(BF16), before, preferred_element_type=F32)
            sel = ((ks > thr) | ((ks == thr) & (rank < need))) & (ks > NEG_INF_KEY)
            m_ref[0, :, sl] = jnp.where(sel, 1, 0).astype(m_ref.dtype)
            run = run + jnp.sum(eq, axis=-1, keepdims=True)


def dsa_mask(qi, w_idx, ki, t, length, tq):
    b, lp, _ = ki.shape
    ksel = min(TOPK_MAX, length // 4)
    off = length - t
    return pl.pallas_call(
        functools.partial(_dsa_mask_kernel, off=off, length=length, tq=tq, lp=lp,
                          ksel=ksel, cw=256),
        grid=(b, t // tq),
        in_specs=[pl.BlockSpec((1, tq, H_IDX * D_IDX), lambda bi, i: (bi, i, 0)),
                  pl.BlockSpec((1, tq, H_IDX), lambda bi, i: (bi, i, 0)),
                  pl.BlockSpec((1, lp, D_IDX), lambda bi, i: (bi, 0, 0))],
        out_specs=pl.BlockSpec((1, tq, lp), lambda bi, i: (bi, i, 0)),
        out_shape=jax.ShapeDtypeStruct((b, t, lp), jnp.int8),
        scratch_shapes=[pltpu.VMEM((tq, lp), jnp.int32)],
        compiler_params=_cparams(("parallel", "parallel")),
    )(qi, w_idx, ki)


def _dsa_attn_kernel(q_ref, k_ref, v_ref, m_ref, o_ref, qs, acc, mrun, lrun,
                     *, off, length, tq, tk, nsteps):
    qb = pl.program_id(1)
    j = pl.program_id(2)
    rep = H_DSA // KV_DSA
    scale = HEAD_DIM ** -0.5
    reach = jnp.minimum(((off + qb * tq + tq - 1) // CHUNK + 1) * CHUNK, length)
    nkb = (reach + tk - 1) // tk

    @pl.when(j == 0)
    def _():
        for g in range(KV_DSA):
            qs[g] = jnp.concatenate(
                [q_ref[0, :, (g * rep + r) * HEAD_DIM:(g * rep + r + 1) * HEAD_DIM]
                 for r in range(rep)], axis=0)
        acc[...] = jnp.zeros_like(acc)
        mrun[...] = jnp.full_like(mrun, NEG_BIG)
        lrun[...] = jnp.zeros_like(lrun)

    @pl.when(j < nkb)
    def _():
        sel = m_ref[0].astype(F32) > 0.5
        sel = jnp.concatenate([sel] * rep, axis=0)
        for g in range(KV_DSA):
            sl = slice(g * HEAD_DIM, (g + 1) * HEAD_DIM)
            s = lax.dot_general(qs[g], k_ref[0, :, sl], NT_DIMS,
                                preferred_element_type=F32) * scale
            s = jnp.where(sel, s, NEG_BIG)
            m_old = mrun[g][:, 0:1]
            m_new = jnp.maximum(m_old, jnp.max(s, axis=-1, keepdims=True))
            p = jnp.where(sel, jnp.exp(s - m_new), 0.0)
            alpha = jnp.exp(m_old - m_new)
            l_new = alpha * lrun[g][:, 0:1] + jnp.sum(p, axis=-1, keepdims=True)
            acc[g] = alpha * acc[g] + jnp.dot(p.astype(BF16), v_ref[0, :, sl],
                                              preferred_element_type=F32)
            mrun[g] = jnp.broadcast_to(m_new, (rep * tq, LANES))
            lrun[g] = jnp.broadcast_to(l_new, (rep * tq, LANES))

    @pl.when(j == nsteps - 1)
    def _():
        for g in range(KV_DSA):
            o = acc[g] / lrun[g][:, 0:1]
            for r in range(rep):
                hd = g * rep + r
                o_ref[0, :, hd * HEAD_DIM:(hd + 1) * HEAD_DIM] = (
                    o[r * tq:(r + 1) * tq]).astype(o_ref.dtype)


def dsa_attention(q, k_arr, kc, v_arr, vc, mask, t, length, tq, tk):
    b = q.shape[0]
    wq = H_DSA * HEAD_DIM
    wk = KV_DSA * HEAD_DIM
    rep = H_DSA // KV_DSA
    off = length - t
    nq = t // tq
    need = lambda qb: (jnp.minimum(((off + qb * tq + tq - 1) // CHUNK + 1) * CHUNK, length)
                       + tk - 1) // tk
    nsteps = (min(((off + t - 1) // CHUNK + 1) * CHUNK, length) + tk - 1) // tk
    kblk = lambda qb, j: jnp.minimum(j, need(qb) - 1)
    return pl.pallas_call(
        functools.partial(_dsa_attn_kernel, off=off, length=length, tq=tq, tk=tk,
                          nsteps=nsteps),
        grid=(b, nq, nsteps),
        in_specs=[pl.BlockSpec((1, tq, wq), lambda bi, qb, j: (bi, qb, 0)),
                  pl.BlockSpec((1, tk, wk), lambda bi, qb, j: (bi, kblk(qb, j), kc)),
                  pl.BlockSpec((1, tk, wk), lambda bi, qb, j: (bi, kblk(qb, j), vc)),
                  pl.BlockSpec((1, tq, tk), lambda bi, qb, j: (bi, qb, kblk(qb, j)))],
        out_specs=pl.BlockSpec((1, tq, wq), lambda bi, qb, j: (bi, qb, 0)),
        out_shape=jax.ShapeDtypeStruct((b, t, wq), BF16),
        scratch_shapes=[pltpu.VMEM((KV_DSA, rep * tq, HEAD_DIM), BF16),
                        pltpu.VMEM((KV_DSA, rep * tq, HEAD_DIM), F32),
                        pltpu.VMEM((KV_DSA, rep * tq, LANES), F32),
                        pltpu.VMEM((KV_DSA, rep * tq, LANES), F32)],
        compiler_params=_cparams(("parallel", "parallel", "arbitrary")),
    )(q, k_arr, v_arr, mask)


def _pad_cols(w, mult):
    e = w.shape[1]
    return jnp.pad(w, ((0, 0), (0, _round_up(e, mult) - e)))


def _with_past(past, new, lp):
    b = new.shape[0]
    parts = [] if past is None else [past.reshape(b, past.shape[1], -1).astype(BF16)]
    parts.append(new.astype(BF16))
    rows = sum(p.shape[1] for p in parts)
    if rows < lp:
        parts.append(jnp.zeros((b, lp - rows, new.shape[2]), BF16))
    return parts[0] if len(parts) == 1 else jnp.concatenate(parts, axis=1)


def _mixer_ab(x2, b, t, past, g, w_in, b_forget, w_out):
    n, d = x2.shape
    wsb = H_SB * HEAD_DIM
    wfx = H_FOX * HEAD_DIM
    pf, pb = norm_matmul(x2, g, _pad_cols(w_in, 256).astype(BF16))
    pf3 = pf.reshape(b, t, -1)
    pb3 = pb.reshape(b, t, -1)
    c_fx = 3 * wsb
    k_sb = pf3[:, :, wsb:2 * wsb].reshape(b, t, H_SB, HEAD_DIM)
    v_sb = pf3[:, :, 2 * wsb:3 * wsb].reshape(b, t, H_SB, HEAD_DIM)
    k_fx = pf3[:, :, c_fx + wfx:c_fx + 2 * wfx].reshape(b, t, H_FOX, HEAD_DIM)
    v_fx = pf3[:, :, c_fx + 2 * wfx:c_fx + 3 * wfx].reshape(b, t, H_FOX, HEAD_DIM)
    f_logit = pf3[:, :, c_fx + 3 * wfx:c_fx + 3 * wfx + H_FOX]

    p = 0 if past is None else past[0].shape[1]
    length = p + t
    lp = _round_up(length, KEY_PAD)
    new_t = jnp.swapaxes(f_logit, 1, 2)
    parts = [new_t] if past is None else [jnp.swapaxes(past[4].astype(F32), 1, 2), new_t]
    if length < lp:
        parts.append(jnp.zeros((b, H_FOX, lp - length), F32))
    pre = (parts[0] if len(parts) == 1 else jnp.concatenate(parts, axis=2)).reshape(b * H_FOX, lp)
    b_col = jnp.tile(b_forget.astype(F32), b).reshape(b * H_FOX, 1)
    logf_t, cum_t = logf_cumsum(pre, b_col, p, length)
    logf = jnp.swapaxes(logf_t.reshape(b, H_FOX, lp)[:, :, p:length], 1, 2)
    cum_kt = cum_t.reshape(b, H_FOX, lp)
    cum_q = jnp.swapaxes(cum_kt[:, :, p:length], 1, 2)

    tq = min(256, t)
    tk = 256
    if past is None:
        o_sb = sb_attention(pb3, 0, pb3, 1, pb3, 2, t, length, tq, tk)
        o_fx = fox_attention(pb3, 3, pb3, 4, pb3, 5, cum_q, cum_kt, t, length, tq, tk)
    else:
        ks = _with_past(past[0], pb3[:, :, wsb:2 * wsb], lp)
        vs = _with_past(past[1], pb3[:, :, 2 * wsb:3 * wsb], lp)
        kf = _with_past(past[2], pb3[:, :, c_fx + wfx:c_fx + 2 * wfx], lp)
        vf = _with_past(past[3], pb3[:, :, c_fx + 2 * wfx:c_fx + 3 * wfx], lp)
        o_sb = sb_attention(pb3, 0, ks, 0, vs, 0, t, length, tq, tk)
        o_fx = fox_attention(pb3, 3, kf, 0, vf, 0, cum_q, cum_kt, t, length, tq, tk)
    o = jnp.concatenate([o_sb, o_fx], axis=-1).reshape(n, wsb + wfx)
    x_new = matmul_res(o, w_out.astype(BF16), x2)
    return x_new, (k_sb, v_sb, k_fx, v_fx, logf)


def _mixer_dsa(x2, b, t, past, g, w_in, w_out):
    n, d = x2.shape
    wq = H_DSA * HEAD_DIM
    wk = KV_DSA * HEAD_DIM
    wi = H_IDX * D_IDX
    pf, pb = norm_matmul(x2, g, _pad_cols(w_in, 256).astype(BF16))
    pf3 = pf.reshape(b, t, -1)
    pb3 = pb.reshape(b, t, -1)
    p = 0 if past is None else past[0].shape[1]
    length = p + t
    lp = _round_up(length, KEY_PAD)
    pos = p + jnp.arange(t)

    c_k, c_v, c_qi, c_ki = wq, wq + wk, wq + 2 * wk, wq + 2 * wk + wi
    _, q_b = rotary(pf3, 0, wq, HEAD_DIM, pos)
    k_f, k_b = rotary(pf3, c_k // wk, wk, HEAD_DIM, pos)
    _, qi_b = rotary(pf3, c_qi // wi, wi, D_IDX, pos)
    ki_f, ki_b = rotary(pf3, c_ki // LANES, LANES, D_IDX, pos)
    k_rows = k_f.reshape(b, t, KV_DSA, HEAD_DIM)
    v_rows = pf3[:, :, c_v:c_v + wk].reshape(b, t, KV_DSA, HEAD_DIM)
    ki_rows = ki_f[:, :, :D_IDX]
    w_idx = pf3[:, :, c_ki + D_IDX:c_ki + D_IDX + H_IDX]

    tq = min(128, t)
    tk = KEY_PAD
    ki_all = _with_past(None if past is None else past[2], ki_b[:, :, :D_IDX], lp)
    mask = dsa_mask(qi_b, w_idx, ki_all, t, length, tq)
    if past is None:
        o = dsa_attention(q_b, k_b, 0, pb3, c_v // wk, mask, t, length, tq, tk)
    else:
        k_all = _with_past(past[0], k_b, lp)
        v_all = _with_past(past[1], pb3[:, :, c_v:c_v + wk], lp)
        o = dsa_attention(q_b, k_all, 0, v_all, 0, mask, t, length, tq, tk)
    x_new = matmul_res(o.reshape(n, wq), w_out.astype(BF16), x2)
    return x_new, (k_rows, v_rows, ki_rows)


def _trunk(x, past_ab, past_dsa, norm_mix, norm_ffn, norm_final, w_in_ab, b_forget, w_out_ab,
           w_in_dsa, w_out_dsa, moe_w_group, moe_w_router, moe_w_gate, moe_w_up, moe_w_down):
    b, t, d = x.shape
    x2 = x.reshape(b * t, d)
    x2, rows_ab = _mixer_ab(x2, b, t, past_ab, norm_mix[0], w_in_ab, b_forget, w_out_ab)
    x2 = moe_block(x2, norm_ffn[0], moe_w_group[0], moe_w_router[0], moe_w_gate[0],
                   moe_w_up[0], moe_w_down[0])
    x2, rows_dsa = _mixer_dsa(x2, b, t, past_dsa, norm_mix[1], w_in_dsa, w_out_dsa)
    y = moe_block(x2, norm_ffn[1], moe_w_group[1], moe_w_router[1], moe_w_gate[1],
                  moe_w_up[1], moe_w_down[1], g_final=norm_final)
    return y.reshape(b, t, d), rows_ab, rows_dsa


def kernel(x_prompt, x_sample, cache_sb_k, cache_sb_v, cache_fox_k, cache_fox_v, cache_fox_logf,
           cache_dsa_k, cache_dsa_v, cache_dsa_idx_k, norm_mix, norm_ffn, norm_final,
           w_in_ab, b_forget, w_out_ab, w_in_dsa, w_out_dsa,
           moe_w_group, moe_w_router, moe_w_gate, moe_w_up, moe_w_down):
    weights = (norm_mix, norm_ffn, norm_final, w_in_ab, b_forget, w_out_ab, w_in_dsa, w_out_dsa,
               moe_w_group, moe_w_router, moe_w_gate, moe_w_up, moe_w_down)
    y_p, ab_p, dsa_p = _trunk(x_prompt, None, None, *weights)
    y_s, ab_s, dsa_s = _trunk(
        x_sample, (cache_sb_k, cache_sb_v, cache_fox_k, cache_fox_v, cache_fox_logf),
        (cache_dsa_k, cache_dsa_v, cache_dsa_idx_k), *weights)
    return (y_p, y_s) + ab_p + dsa_p + ab_s + dsa_s
```

```python
import functools

import jax
import jax.numpy as jnp
import numpy as np
from jax import lax
from jax.experimental import pallas as pl
from jax.experimental.pallas import tpu as pltpu

CHUNK = 64
HEAD_DIM = 128
H_SB = 4
H_FOX = 4
H_DSA = 8
KV_DSA = 2
H_IDX = 4
D_IDX = 64
TOPK_MAX = 256
ROPE_THETA = 10000.0
EPS = 1e-6

LANES = 128
BF16_ROWS = 16

KEY_PAD = 512
LOG2E = 1.4426950408889634
M_INIT = -1e38
SB_DEAD_LOG2 = -150.0
INT32_MIN = -(2 ** 31)
NEG_INF_CODE = int(np.int32(np.uint32(0xFF800000) ^ np.uint32(0x7FFFFFFF)))

F32 = jnp.float32
BF16 = jnp.bfloat16
NT_DIMS = (((1,), (1,)), ((), ()))


def _round_up(a, b):
    return (a + b - 1) // b * b


def _cparams(semantics, vmem_mib=48):
    return pltpu.CompilerParams(dimension_semantics=semantics,
                                vmem_limit_bytes=vmem_mib * 1024 * 1024)


def _rms(x, g):
    return x * lax.rsqrt(jnp.mean(x * x, axis=-1, keepdims=True) + EPS) * g


def _log_sigmoid(z):
    return jnp.minimum(z, 0.0) - jnp.log1p(jnp.exp(-jnp.abs(z)))


def _split3(x):
    hi = x.astype(BF16)
    r1 = x - hi.astype(F32)
    mid = r1.astype(BF16)
    lo = (r1 - mid.astype(F32)).astype(BF16)
    return hi, mid, lo


def _norm_matmul_kernel(x_ref, g_ref, w_ref, of_ref, ob_ref, *, tn):
    h = _rms(x_ref[...], g_ref[...]).astype(BF16)
    for j in range(w_ref.shape[1] // tn):
        sl = slice(j * tn, (j + 1) * tn)
        acc = jnp.dot(h, w_ref[:, sl], preferred_element_type=F32)
        of_ref[:, sl] = acc
        ob_ref[:, sl] = acc.astype(BF16)


def norm_matmul(x, g, w, tn=256):
    n, d = x.shape
    e = w.shape[1]
    tm = min(256, n)
    return pl.pallas_call(
        functools.partial(_norm_matmul_kernel, tn=tn),
        grid=(n // tm,),
        in_specs=[pl.BlockSpec((tm, d), lambda i: (i, 0)),
                  pl.BlockSpec((1, d), lambda i: (0, 0)),
                  pl.BlockSpec((d, e), lambda i: (0, 0))],
        out_specs=[pl.BlockSpec((tm, e), lambda i: (i, 0)),
                   pl.BlockSpec((tm, e), lambda i: (i, 0))],
        out_shape=[jax.ShapeDtypeStruct((n, e), F32),
                   jax.ShapeDtypeStruct((n, e), BF16)],
        compiler_params=_cparams(("parallel",)),
    )(x, g.reshape(1, d), w)


def _matmul_res_kernel(a_ref, w_ref, r_ref, o_ref):
    o_ref[...] = r_ref[...] + jnp.dot(a_ref[...], w_ref[...], preferred_element_type=F32)


def matmul_res(a, w, res):
    n, k = a.shape
    d = w.shape[1]
    tm = min(512, n)
    return pl.pallas_call(
        _matmul_res_kernel,
        grid=(n // tm,),
        in_specs=[pl.BlockSpec((tm, k), lambda i: (i, 0)),
                  pl.BlockSpec((k, d), lambda i: (0, 0)),
                  pl.BlockSpec((tm, d), lambda i: (i, 0))],
        out_specs=pl.BlockSpec((tm, d), lambda i: (i, 0)),
        out_shape=jax.ShapeDtypeStruct((n, d), F32),
        compiler_params=_cparams(("parallel",)),
    )(a, w, res)


def _logf_cumsum_kernel(pre_ref, b_ref, logf_ref, hi_ref, mid_ref, lo_ref, carry,
                        *, past, length, cw):
    j = pl.program_id(0)

    @pl.when(j == 0)
    def _():
        carry[...] = jnp.zeros_like(carry)

    pre = pre_ref[...]
    pos = j * cw + lax.broadcasted_iota(jnp.int32, pre.shape, 1)
    x = jnp.where(pos >= past, _log_sigmoid(pre + b_ref[...]), pre)
    x = jnp.where(pos < length, x, 0.0)
    logf_ref[...] = x
    row = lax.broadcasted_iota(jnp.int32, (cw, cw), 0)
    col = lax.broadcasted_iota(jnp.int32, (cw, cw), 1)
    upper = jnp.where(row <= col, 1.0, 0.0).astype(BF16)
    hi, mid, lo = _split3(x)
    cs = (jnp.dot(hi, upper, preferred_element_type=F32)
          + jnp.dot(mid, upper, preferred_element_type=F32)
          + jnp.dot(lo, upper, preferred_element_type=F32))
    cum = cs + carry[:, 0:1]
    carry[...] = jnp.broadcast_to(cum[:, cw - 1:cw], carry.shape)
    hi_ref[...], mid_ref[...], lo_ref[...] = _split3(cum * LOG2E)


def logf_cumsum(pre, b_col, past, length):
    r, lp = pre.shape
    cw = KEY_PAD
    blk = pl.BlockSpec((r, cw), lambda j: (0, j))
    return pl.pallas_call(
        functools.partial(_logf_cumsum_kernel, past=past, length=length, cw=cw),
        grid=(lp // cw,),
        in_specs=[blk, pl.BlockSpec((r, 1), lambda j: (0, 0))],
        out_specs=[blk, blk, blk, blk],
        out_shape=[jax.ShapeDtypeStruct((r, lp), F32)] + [jax.ShapeDtypeStruct((r, lp), BF16)] * 3,
        scratch_shapes=[pltpu.VMEM((r, LANES), F32)],
        compiler_params=_cparams(("arbitrary",)),
    )(pre, b_col)


def _softmax_rows(s_parts, m_old, l_old):
    mx = s_parts[0]
    for sp in s_parts[1:]:
        mx = jnp.maximum(mx, sp)
    m_new = jnp.maximum(m_old, jnp.max(mx, axis=-1, keepdims=True))
    alpha = jnp.exp2(m_old - m_new)
    ps = [jnp.exp2(sp - m_new) for sp in s_parts]
    tot = ps[0]
    for p in ps[1:]:
        tot = tot + p
    return ps, m_new, alpha * l_old + tot, alpha


def _sb_kernel(q_ref, k_ref, v_ref, o_ref, s_scr, b_scr, hi_scr, lo_scr, p_scr, tri, acc, carry,
               *, off, tq, tk, heads):
    qi = pl.program_id(1)
    q0 = off + qi * tq
    kb_last = (q0 + tq - 2) // tk
    ng = tk // LANES
    nchunk = tq // BF16_ROWS

    acc[...] = jnp.zeros_like(acc)
    carry[...] = jnp.zeros_like(carry)
    row = lax.broadcasted_iota(jnp.int32, (tk, tk), 0)
    col = lax.broadcasted_iota(jnp.int32, (tk, tk), 1)
    tri[...] = jnp.where(row > col, 1.0, 0.0).astype(BF16)

    def block(state):
        kb, _ = state
        k0 = pl.multiple_of(kb * tk, tk)
        kpos = k0 + lax.broadcasted_iota(jnp.int32, (1, LANES), 1)
        for h in range(heads):
            sl = slice(h * HEAD_DIM, (h + 1) * HEAD_DIM)
            s_scr[h] = lax.dot_general(q_ref[0, :, sl], k_ref[0, pl.ds(k0, tk), sl], NT_DIMS,
                                       preferred_element_type=F32)
        for h in range(heads):
            for c in range(nchunk):
                rows = slice(c * BF16_ROWS, (c + 1) * BF16_ROWS)
                qpos = q0 + c * BF16_ROWS + lax.broadcasted_iota(jnp.int32, (BF16_ROWS, 1), 0)
                for g in range(ng):
                    ls = slice(g * LANES, (g + 1) * LANES)
                    z = s_scr[h, rows, ls]
                    lsig = jnp.minimum(z, 0.0) - jnp.log2(1.0 + jnp.exp2(-jnp.abs(z)))
                    l1m = jnp.where(kpos + g * LANES < qpos, lsig - z, 0.0)
                    hi = l1m.astype(BF16)
                    hi_scr[h, rows, ls] = hi
                    lo_scr[h, rows, ls] = (l1m - hi.astype(F32)).astype(BF16)
                    s_scr[h, rows, ls] = lsig
        for h in range(heads):
            b_scr[h] = (jnp.dot(hi_scr[h], tri[...], preferred_element_type=F32)
                        + jnp.dot(lo_scr[h], tri[...], preferred_element_type=F32))
        for h in range(heads):
            for c in range(nchunk):
                rows = slice(c * BF16_ROWS, (c + 1) * BF16_ROWS)
                qpos = q0 + c * BF16_ROWS + lax.broadcasted_iota(jnp.int32, (BF16_ROWS, 1), 0)
                run = carry[h, rows, :]
                for g in range(ng):
                    ls = slice(g * LANES, (g + 1) * LANES)
                    a = jnp.exp2(s_scr[h, rows, ls] + b_scr[h, rows, ls] + run)
                    a = jnp.where(kpos + g * LANES < qpos, a, 0.0)
                    p_scr[h, rows, ls] = a.astype(BF16)
                first = (b_scr[h, rows, 0:1] + hi_scr[h, rows, 0:1].astype(F32)
                         + lo_scr[h, rows, 0:1].astype(F32))
                carry[h, rows, :] = run + first
        for h in range(heads):
            sl = slice(h * HEAD_DIM, (h + 1) * HEAD_DIM)
            acc[h] += jnp.dot(p_scr[h], v_ref[0, pl.ds(k0, tk), sl], preferred_element_type=F32)
        return kb - 1, jnp.max(carry[...]) > SB_DEAD_LOG2

    lax.while_loop(lambda st: jnp.logical_and(st[0] >= 0, st[1]), block,
                   (kb_last, jnp.bool_(True)))
    for h in range(heads):
        o_ref[0, :, h * HEAD_DIM:(h + 1) * HEAD_DIM] = acc[h].astype(o_ref.dtype)


def sb_attention(q, k_arr, kc, v_arr, vc, t, length, tq, tk):
    b = q.shape[0]
    lp = k_arr.shape[1]
    w = H_SB * HEAD_DIM
    return pl.pallas_call(
        functools.partial(_sb_kernel, off=length - t, tq=tq, tk=tk, heads=H_SB),
        grid=(b, t // tq),
        in_specs=[pl.BlockSpec((1, tq, w), lambda bi, qi: (bi, qi, 0)),
                  pl.BlockSpec((1, lp, w), lambda bi, qi: (bi, 0, kc)),
                  pl.BlockSpec((1, lp, w), lambda bi, qi: (bi, 0, vc))],
        out_specs=pl.BlockSpec((1, tq, w), lambda bi, qi: (bi, qi, 0)),
        out_shape=jax.ShapeDtypeStruct((b, t, w), BF16),
        scratch_shapes=[pltpu.VMEM((H_SB, tq, tk), F32),
                        pltpu.VMEM((H_SB, tq, tk), F32),
                        pltpu.VMEM((H_SB, tq, tk), BF16),
                        pltpu.VMEM((H_SB, tq, tk), BF16),
                        pltpu.VMEM((H_SB, tq, tk), BF16),
                        pltpu.VMEM((tk, tk), BF16),
                        pltpu.VMEM((H_SB, tq, HEAD_DIM), F32),
                        pltpu.VMEM((H_SB, tq, LANES), F32)],
        compiler_params=_cparams(("parallel", "arbitrary")),
    )(q, k_arr, v_arr)


FOX_AUG = 2 * HEAD_DIM


def _fox_kernel(q_ref, k_ref, v_ref, o_ref, s_scr, p_scr, acc, m_scr, l_scr, a_scr,
                *, off, tq, tk, heads):
    qi = pl.program_id(1)
    q0 = off + qi * tq
    kb_last = (q0 + tq - 1) // tk
    ng = tk // LANES
    nchunk = tq // BF16_ROWS

    acc[...] = jnp.zeros_like(acc)
    m_scr[...] = jnp.full_like(m_scr, M_INIT)
    l_scr[...] = jnp.zeros_like(l_scr)

    def block(kb, _):
        k0 = pl.multiple_of(kb * tk, tk)
        for h in range(heads):
            sl = slice(h * FOX_AUG, (h + 1) * FOX_AUG)
            s_scr[h] = lax.dot_general(q_ref[0, :, sl], k_ref[0, pl.ds(k0, tk), sl], NT_DIMS,
                                       preferred_element_type=F32)

        @pl.when(k0 + tk - 1 > q0)
        def _():
            qpos = q0 + lax.broadcasted_iota(jnp.int32, (tq, 1), 0)
            kpos = k0 + lax.broadcasted_iota(jnp.int32, (1, tk), 1)
            for h in range(heads):
                s_scr[h] = jnp.where(kpos <= qpos, s_scr[h], -jnp.inf)

        for h in range(heads):
            for c in range(nchunk):
                rows = slice(c * BF16_ROWS, (c + 1) * BF16_ROWS)
                parts = [s_scr[h, rows, g * LANES:(g + 1) * LANES] for g in range(ng)]
                ps, m_new, l_new, alpha = _softmax_rows(parts, m_scr[h, rows, :], l_scr[h, rows, :])
                m_scr[h, rows, :] = m_new
                l_scr[h, rows, :] = l_new
                a_scr[h, rows, :] = alpha
                for g in range(ng):
                    p_scr[h, rows, g * LANES:(g + 1) * LANES] = ps[g].astype(BF16)
        for h in range(heads):
            sl = slice(h * HEAD_DIM, (h + 1) * HEAD_DIM)
            acc[h] = a_scr[h] * acc[h] + jnp.dot(p_scr[h], v_ref[0, pl.ds(k0, tk), sl],
                                                 preferred_element_type=F32)
        return 0

    lax.fori_loop(0, kb_last + 1, block, 0)
    for h in range(heads):
        l = jnp.sum(l_scr[h], axis=-1, keepdims=True)
        o_ref[0, :, h * HEAD_DIM:(h + 1) * HEAD_DIM] = (acc[h] / l).astype(o_ref.dtype)


def _fox_augment(x, bias_cols):
    b, r, h, d = x.shape
    pad = jnp.zeros((b, r, h, FOX_AUG - d - bias_cols.shape[-1]), BF16)
    return jnp.concatenate([x, bias_cols, pad], axis=-1).reshape(b, r, h * FOX_AUG)


def fox_attention(q_aug, k_aug, v_arr, vc, t, length, tq, tk):
    b = q_aug.shape[0]
    lp = k_aug.shape[1]
    w = H_FOX * HEAD_DIM
    wa = H_FOX * FOX_AUG
    return pl.pallas_call(
        functools.partial(_fox_kernel, off=length - t, tq=tq, tk=tk, heads=H_FOX),
        grid=(b, t // tq),
        in_specs=[pl.BlockSpec((1, tq, wa), lambda bi, qi: (bi, qi, 0)),
                  pl.BlockSpec((1, lp, wa), lambda bi, qi: (bi, 0, 0)),
                  pl.BlockSpec((1, lp, w), lambda bi, qi: (bi, 0, vc))],
        out_specs=pl.BlockSpec((1, tq, w), lambda bi, qi: (bi, qi, 0)),
        out_shape=jax.ShapeDtypeStruct((b, t, w), BF16),
        scratch_shapes=[pltpu.VMEM((H_FOX, tq, tk), F32),
                        pltpu.VMEM((H_FOX, tq, tk), BF16),
                        pltpu.VMEM((H_FOX, tq, HEAD_DIM), F32),
                        pltpu.VMEM((H_FOX, tq, LANES), F32),
                        pltpu.VMEM((H_FOX, tq, LANES), F32),
                        pltpu.VMEM((H_FOX, tq, LANES), F32)],
        compiler_params=_cparams(("parallel", "arbitrary")),
    )(q_aug, k_aug, v_arr)


def _moe_kernel(x_ref, g_ref, wr_ref, wg_ref, wu_ref, wd_ref, gf_ref, o_ref, hb, comb,
                *, n_groups, per_group, final_norm):
    e = pl.program_id(1)
    n_exp = n_groups * per_group
    lane = lax.broadcasted_iota(jnp.int32, comb.shape, 1)

    @pl.when(e == 0)
    def _():
        x = x_ref[...]
        h = _rms(x, g_ref[...])
        hb[...] = h.astype(BF16)
        logits = jnp.dot(h, wr_ref[...], precision=lax.Precision.HIGHEST,
                         preferred_element_type=F32)
        gl = jnp.where(lane < n_groups, logits, -jnp.inf)
        gmax = jnp.max(gl, axis=-1, keepdims=True)
        gsel = jnp.min(jnp.where(gl == gmax, lane, LANES), axis=-1, keepdims=True)
        p_group = 1.0 / jnp.sum(jnp.where(lane < n_groups, jnp.exp(gl - gmax), 0.0),
                                axis=-1, keepdims=True)
        first = n_groups + gsel * per_group
        el = jnp.where((lane >= first) & (lane < first + per_group), logits, -jnp.inf)
        v1 = jnp.max(el, axis=-1, keepdims=True)
        i1 = jnp.min(jnp.where(el == v1, lane, LANES), axis=-1, keepdims=True)
        el2 = jnp.where(lane == i1, -jnp.inf, el)
        v2 = jnp.max(el2, axis=-1, keepdims=True)
        i2 = jnp.min(jnp.where(el2 == v2, lane, LANES), axis=-1, keepdims=True)
        ratio = jnp.exp(v2 - v1)
        gate1 = p_group / (1.0 + ratio)
        comb[...] = jnp.where(lane == i1, gate1, jnp.where(lane == i2, gate1 * ratio, 0.0))
        o_ref[...] = x

    c = jnp.sum(jnp.where(lane == e + n_groups, comb[...], 0.0), axis=-1, keepdims=True)
    hv = hb[...]
    gate = jnp.dot(hv, wg_ref[0], preferred_element_type=F32)
    up = jnp.dot(hv, wu_ref[0], preferred_element_type=F32)
    act = gate * (1.0 / (1.0 + jnp.exp(-gate))) * up * c
    o_ref[...] += jnp.dot(act.astype(BF16), wd_ref[0], preferred_element_type=F32)

    if final_norm:
        @pl.when(e == n_exp - 1)
        def _():
            o_ref[...] = _rms(o_ref[...], gf_ref[...])


def moe_block(x, g, w_group, w_router, w_gate, w_up, w_down, g_final=None):
    n, d = x.shape
    n_groups, per_group = w_router.shape[1], w_router.shape[2]
    n_exp, _, f = w_gate.shape
    w_route = jnp.concatenate(
        [w_group, w_router.reshape(d, n_exp),
         jnp.zeros((d, LANES - n_groups - n_exp), F32)], axis=1)
    final_norm = g_final is not None
    gf = (g_final if final_norm else g).reshape(1, d)
    tm = min(1024, n)
    return pl.pallas_call(
        functools.partial(_moe_kernel, n_groups=n_groups, per_group=per_group,
                          final_norm=final_norm),
        grid=(n // tm, n_exp),
        in_specs=[pl.BlockSpec((tm, d), lambda i, e: (i, 0)),
                  pl.BlockSpec((1, d), lambda i, e: (0, 0)),
                  pl.BlockSpec((d, LANES), lambda i, e: (0, 0)),
                  pl.BlockSpec((1, d, f), lambda i, e: (e, 0, 0)),
                  pl.BlockSpec((1, d, f), lambda i, e: (e, 0, 0)),
                  pl.BlockSpec((1, f, d), lambda i, e: (e, 0, 0)),
                  pl.BlockSpec((1, d), lambda i, e: (0, 0))],
        out_specs=pl.BlockSpec((tm, d), lambda i, e: (i, 0)),
        out_shape=jax.ShapeDtypeStruct((n, d), F32),
        scratch_shapes=[pltpu.VMEM((tm, d), BF16), pltpu.VMEM((tm, LANES), F32)],
        compiler_params=_cparams(("parallel", "arbitrary")),
    )(x, g.reshape(1, d), w_route, w_gate.astype(BF16), w_up.astype(BF16),
      w_down.astype(BF16), gf)


def _rotary_kernel(x_ref, c_ref, s_ref, of_ref, ob_ref, *, half, width, bf16_scale):
    cos = c_ref[...]
    sin = s_ref[...]
    lane = lax.broadcasted_iota(jnp.int32, cos.shape, 1)
    for gi in range(width // LANES):
        sl = slice(gi * LANES, (gi + 1) * LANES)
        xs = x_ref[0, :, sl]
        if 2 * half == LANES:
            partner = pltpu.roll(xs, half, 1)
        else:
            partner = jnp.where(lane % (2 * half) < half,
                                pltpu.roll(xs, LANES - half, 1), pltpu.roll(xs, half, 1))
        o = xs * cos + partner * sin
        of_ref[0, :, sl] = o
        ob_ref[0, :, sl] = (o * bf16_scale).astype(BF16)


def _rope_tables(pos, head_dim):
    half = head_dim // 2
    inv_freq = ROPE_THETA ** (-jnp.arange(half, dtype=F32) / half)
    ang = pos.astype(F32)[:, None] * inv_freq[None, :]
    cos, sin = jnp.cos(ang), jnp.sin(ang)
    reps = LANES // head_dim
    return (jnp.tile(jnp.concatenate([cos, cos], axis=1), (1, reps)),
            jnp.tile(jnp.concatenate([-sin, sin], axis=1), (1, reps)))


def rotary(x3, colblock, width, head_dim, pos, bf16_scale=1.0):
    b, t, _ = x3.shape
    tm = min(512, t)
    cos, sin = _rope_tables(pos, head_dim)
    return pl.pallas_call(
        functools.partial(_rotary_kernel, half=head_dim // 2, width=width, bf16_scale=bf16_scale),
        grid=(b, t // tm),
        in_specs=[pl.BlockSpec((1, tm, width), lambda bi, i: (bi, i, colblock)),
                  pl.BlockSpec((tm, LANES), lambda bi, i: (i, 0)),
                  pl.BlockSpec((tm, LANES), lambda bi, i: (i, 0))],
        out_specs=[pl.BlockSpec((1, tm, width), lambda bi, i: (bi, i, 0)),
                   pl.BlockSpec((1, tm, width), lambda bi, i: (bi, i, 0))],
        out_shape=[jax.ShapeDtypeStruct((b, t, width), F32),
                   jax.ShapeDtypeStruct((b, t, width), BF16)],
        compiler_params=_cparams(("parallel", "parallel")),
    )(x3, cos, sin)


def _code_to_float(code):
    return pltpu.bitcast(code ^ ((code >> 31) & 0x7FFFFFFF), F32)


def _dsa_reach(q_first, tq, length):
    return jnp.minimum(((q_first + tq - 1) // CHUNK + 1) * CHUNK, length)


def _dsa_mask_kernel(qi_ref, w_ref, ki_ref, m_ref, sc, *, off, length, tq, ksel, cw):
    qb = pl.program_id(1)
    q0 = off + qb * tq
    qchunk = (q0 + lax.broadcasted_iota(jnp.int32, (tq, 1), 0)) // CHUNK
    nck = (_dsa_reach(q0, tq, length) + cw - 1) // cw
    w = w_ref[0] * (H_IDX ** -0.5)
    ng = cw // LANES
    kf = float(ksel)

    def fill(c, _):
        k0 = pl.multiple_of(c * cw, cw)
        kk = ki_ref[0, pl.ds(k0, cw), :]
        score = jnp.zeros((tq, cw), F32)
        for h in range(H_IDX):
            lg = lax.dot_general(qi_ref[0, :, h * D_IDX:(h + 1) * D_IDX], kk, NT_DIMS,
                                 preferred_element_type=F32) * (D_IDX ** -0.5)
            score = score + jnp.maximum(lg, 0.0) * w[:, h:h + 1]
        kpos = k0 + lax.broadcasted_iota(jnp.int32, (1, cw), 1)
        admissible = (kpos // CHUNK <= qchunk) & (kpos < length)
        sc[c] = jnp.where(admissible, score + 0.0, -jnp.inf)
        return 0

    lax.fori_loop(0, nck, fill, 0)

    def blank(c, _):
        m_ref[0, c] = jnp.full((tq, cw), -jnp.inf, m_ref.dtype)
        return 0

    lax.fori_loop(nck, m_ref.shape[1], blank, 0)

    def count(pred):
        def body(c, tot):
            for g in range(ng):
                tot = tot + jnp.where(pred(sc[c, :, g * LANES:(g + 1) * LANES]), 1.0, 0.0)
            return tot
        tot = lax.fori_loop(0, nck, body, jnp.zeros((tq, LANES), F32))
        return jnp.sum(tot, axis=-1, keepdims=True)

    code0 = jnp.where(count(lambda x: x >= 0.0) >= kf, 0, INT32_MIN).astype(jnp.int32)

    def bit_step(i, code):
        cand = code + lax.shift_left(jnp.int32(1), 30 - i)
        cf = _code_to_float(cand)
        return jnp.where(count(lambda x: x >= cf) >= kf, cand, code)

    code = lax.fori_loop(0, 31, bit_step, code0)
    thr = _code_to_float(jnp.maximum(code, NEG_INF_CODE + 1))
    n_ge = count(lambda x: x >= thr)
    n_gt = count(lambda x: x > thr)
    tied = jnp.max(n_ge) > kf

    @pl.when(jnp.logical_not(tied))
    def _():
        def emit(c, _):
            m_ref[0, c] = jnp.where(sc[c] >= thr, 0.0, -jnp.inf).astype(m_ref.dtype)
            return 0
        lax.fori_loop(0, nck, emit, 0)

    @pl.when(tied)
    def _():
        need = kf - n_gt
        row = lax.broadcasted_iota(jnp.int32, (LANES, LANES), 0)
        col = lax.broadcasted_iota(jnp.int32, (LANES, LANES), 1)
        before = jnp.where(row < col, 1.0, 0.0).astype(BF16)

        def emit(c, run):
            for g in range(ng):
                sl = slice(g * LANES, (g + 1) * LANES)
                x = sc[c, :, sl]
                eq = jnp.where(x == thr, 1.0, 0.0)
                rank = run + jnp.dot(eq.astype(BF16), before, preferred_element_type=F32)
                sel = (x > thr) | ((x == thr) & (rank < need))
                m_ref[0, c, :, sl] = jnp.where(sel, 0.0, -jnp.inf).astype(m_ref.dtype)
                run = run + jnp.sum(eq, axis=-1, keepdims=True)
            return run
        lax.fori_loop(0, nck, emit, jnp.zeros((tq, 1), F32))


def dsa_mask(qi, w_idx, ki, t, length, tq, cw):
    b, lp, _ = ki.shape
    ksel = min(TOPK_MAX, length // 4)
    nch = lp // cw
    return pl.pallas_call(
        functools.partial(_dsa_mask_kernel, off=length - t, length=length, tq=tq,
                          ksel=ksel, cw=cw),
        grid=(b, t // tq),
        in_specs=[pl.BlockSpec((1, tq, H_IDX * D_IDX), lambda bi, i: (bi, i, 0)),
                  pl.BlockSpec((1, tq, H_IDX), lambda bi, i: (bi, i, 0)),
                  pl.BlockSpec((1, lp, D_IDX), lambda bi, i: (bi, 0, 0))],
        out_specs=pl.BlockSpec((1, nch, tq, cw), lambda bi, i: (bi, 0, i, 0)),
        out_shape=jax.ShapeDtypeStruct((b, nch, t, cw), BF16),
        scratch_shapes=[pltpu.VMEM((nch, tq, cw), F32)],
        compiler_params=_cparams(("parallel", "arbitrary")),
    )(qi, w_idx, ki)


def _dsa_attn_kernel(q_ref, k_ref, v_ref, m_ref, o_ref, qs, s_scr, p_scr, acc, m_scr, l_scr, a_scr,
                     *, off, length, tq, tk):
    qb = pl.program_id(1)
    rep = H_DSA // KV_DSA
    nkb = (_dsa_reach(off + qb * tq, tq, length) + tk - 1) // tk
    ng = tk // LANES
    nchunk = tq // BF16_ROWS

    for g in range(KV_DSA):
        for r in range(rep):
            hd = g * rep + r
            qs[g, r * tq:(r + 1) * tq, :] = q_ref[0, :, hd * HEAD_DIM:(hd + 1) * HEAD_DIM]
    acc[...] = jnp.zeros_like(acc)
    m_scr[...] = jnp.full_like(m_scr, M_INIT)
    l_scr[...] = jnp.zeros_like(l_scr)

    def block(kb, _):
        k0 = pl.multiple_of(kb * tk, tk)
        for g in range(KV_DSA):
            sl = slice(g * HEAD_DIM, (g + 1) * HEAD_DIM)
            s_scr[g] = lax.dot_general(qs[g], k_ref[0, pl.ds(k0, tk), sl], NT_DIMS,
                                       preferred_element_type=F32)
        for c in range(nchunk):
            bias = [m_ref[0, kb, c * BF16_ROWS:(c + 1) * BF16_ROWS,
                          gl * LANES:(gl + 1) * LANES].astype(F32) for gl in range(ng)]
            for g in range(KV_DSA):
                for r in range(rep):
                    rows = slice(r * tq + c * BF16_ROWS, r * tq + (c + 1) * BF16_ROWS)
                    parts = [s_scr[g, rows, gl * LANES:(gl + 1) * LANES] + bias[gl]
                             for gl in range(ng)]
                    ps, m_new, l_new, alpha = _softmax_rows(parts, m_scr[g, rows, :],
                                                            l_scr[g, rows, :])
                    m_scr[g, rows, :] = m_new
                    l_scr[g, rows, :] = l_new
                    a_scr[g, rows, :] = alpha
                    for gl in range(ng):
                        p_scr[g, rows, gl * LANES:(gl + 1) * LANES] = ps[gl].astype(BF16)
        for g in range(KV_DSA):
            sl = slice(g * HEAD_DIM, (g + 1) * HEAD_DIM)
            acc[g] = a_scr[g] * acc[g] + jnp.dot(p_scr[g], v_ref[0, pl.ds(k0, tk), sl],
                                                 preferred_element_type=F32)
        return 0

    lax.fori_loop(0, nkb, block, 0)
    for g in range(KV_DSA):
        o = acc[g] / jnp.sum(l_scr[g], axis=-1, keepdims=True)
        for r in range(rep):
            hd = g * rep + r
            o_ref[0, :, hd * HEAD_DIM:(hd + 1) * HEAD_DIM] = (
                o[r * tq:(r + 1) * tq]).astype(o_ref.dtype)


def dsa_attention(q, k_arr, kc, v_arr, vc, bias, t, length, tq, tk):
    b = q.shape[0]
    lp = k_arr.shape[1]
    wq = H_DSA * HEAD_DIM
    wk = KV_DSA * HEAD_DIM
    rep = H_DSA // KV_DSA
    nch = lp // tk
    return pl.pallas_call(
        functools.partial(_dsa_attn_kernel, off=length - t, length=length, tq=tq, tk=tk),
        grid=(b, t // tq),
        in_specs=[pl.BlockSpec((1, tq, wq), lambda bi, qb: (bi, qb, 0)),
                  pl.BlockSpec((1, lp, wk), lambda bi, qb: (bi, 0, kc)),
                  pl.BlockSpec((1, lp, wk), lambda bi, qb: (bi, 0, vc)),
                  pl.BlockSpec((1, nch, tq, tk), lambda bi, qb: (bi, 0, qb, 0))],
        out_specs=pl.BlockSpec((1, tq, wq), lambda bi, qb: (bi, qb, 0)),
        out_shape=jax.ShapeDtypeStruct((b, t, wq), BF16),
        scratch_shapes=[pltpu.VMEM((KV_DSA, rep * tq, HEAD_DIM), BF16),
                        pltpu.VMEM((KV_DSA, rep * tq, tk), F32),
                        pltpu.VMEM((KV_DSA, rep * tq, tk), BF16),
                        pltpu.VMEM((KV_DSA, rep * tq, HEAD_DIM), F32),
                        pltpu.VMEM((KV_DSA, rep * tq, LANES), F32),
                        pltpu.VMEM((KV_DSA, rep * tq, LANES), F32),
                        pltpu.VMEM((KV_DSA, rep * tq, LANES), F32)],
        compiler_params=_cparams(("parallel", "arbitrary")),
    )(q, k_arr, v_arr, bias)


def _pad_cols(w, mult):
    e = w.shape[1]
    return jnp.pad(w, ((0, 0), (0, _round_up(e, mult) - e)))


def _with_past(past, new, lp):
    b = new.shape[0]
    parts = [] if past is None else [past.reshape(b, past.shape[1], -1).astype(BF16)]
    parts.append(new.astype(BF16))
    rows = sum(p.shape[1] for p in parts)
    if rows < lp:
        parts.append(jnp.zeros((b, lp - rows, new.shape[2]), BF16))
    return parts[0] if len(parts) == 1 else jnp.concatenate(parts, axis=1)


def _mixer_ab(x2, b, t, past, g, w_in, b_forget, w_out):
    n, d = x2.shape
    wsb = H_SB * HEAD_DIM
    wfx = H_FOX * HEAD_DIM
    pf, pb = norm_matmul(x2, g, _pad_cols(w_in, 256).astype(BF16))
    pf3 = pf.reshape(b, t, -1)
    pb3 = pb.reshape(b, t, -1)
    c_fx = 3 * wsb
    k_sb = pf3[:, :, wsb:2 * wsb].reshape(b, t, H_SB, HEAD_DIM)
    v_sb = pf3[:, :, 2 * wsb:3 * wsb].reshape(b, t, H_SB, HEAD_DIM)
    k_fx = pf3[:, :, c_fx + wfx:c_fx + 2 * wfx].reshape(b, t, H_FOX, HEAD_DIM)
    v_fx = pf3[:, :, c_fx + 2 * wfx:c_fx + 3 * wfx].reshape(b, t, H_FOX, HEAD_DIM)
    f_logit = pf3[:, :, c_fx + 3 * wfx:c_fx + 3 * wfx + H_FOX]

    p = 0 if past is None else past[0].shape[1]
    length = p + t
    lp = _round_up(length, KEY_PAD)
    new_t = jnp.swapaxes(f_logit, 1, 2)
    parts = [new_t] if past is None else [jnp.swapaxes(past[4].astype(F32), 1, 2), new_t]
    if length < lp:
        parts.append(jnp.zeros((b, H_FOX, lp - length), F32))
    pre = (parts[0] if len(parts) == 1 else jnp.concatenate(parts, axis=2)).reshape(b * H_FOX, lp)
    b_col = jnp.tile(b_forget.astype(F32), b).reshape(b * H_FOX, 1)
    logf_t, *cum_parts = logf_cumsum(pre, b_col, p, length)
    logf = jnp.swapaxes(logf_t.reshape(b, H_FOX, lp)[:, :, p:length], 1, 2)

    qscale = LOG2E * HEAD_DIM ** -0.5
    q_sb = (pf3[:, :, 0:wsb] * qscale).astype(BF16)
    f_parts = jnp.stack([jnp.swapaxes(c.reshape(b, H_FOX, lp), 1, 2) for c in cum_parts],
                        axis=-1)
    ones3 = jnp.ones((b, lp, H_FOX, 3), BF16)
    q_fx = (pf3[:, :, c_fx:c_fx + wfx] * qscale).astype(BF16).reshape(b, t, H_FOX, HEAD_DIM)
    q_aug = _fox_augment(q_fx, jnp.concatenate([f_parts[:, p:length], ones3[:, :t]], axis=-1))
    k_bias = jnp.concatenate([ones3, -f_parts], axis=-1)

    tq = min(256, t)
    if past is None:
        kf = pb3[:, :, c_fx + wfx:c_fx + 2 * wfx]
        ks_arr, ks_c, vs_arr, vs_c, vf_arr, vf_c = pb3, 1, pb3, 2, pb3, 5
    else:
        ks_arr, ks_c = _with_past(past[0], pb3[:, :, wsb:2 * wsb], lp), 0
        vs_arr, vs_c = _with_past(past[1], pb3[:, :, 2 * wsb:3 * wsb], lp), 0
        kf = _with_past(past[2], pb3[:, :, c_fx + wfx:c_fx + 2 * wfx], lp)
        vf_arr, vf_c = _with_past(past[3], pb3[:, :, c_fx + 2 * wfx:c_fx + 3 * wfx], lp), 0
    k_aug = _fox_augment(kf.reshape(b, lp, H_FOX, HEAD_DIM), k_bias)
    o_sb = sb_attention(q_sb, ks_arr, ks_c, vs_arr, vs_c, t, length, tq, 256)
    o_fx = fox_attention(q_aug, k_aug, vf_arr, vf_c, t, length, tq, 512)
    o = jnp.concatenate([o_sb, o_fx], axis=-1).reshape(n, wsb + wfx)
    x_new = matmul_res(o, w_out.astype(BF16), x2)
    return x_new, (k_sb, v_sb, k_fx, v_fx, logf)


def _mixer_dsa(x2, b, t, past, g, w_in, w_out):
    n, d = x2.shape
    wq = H_DSA * HEAD_DIM
    wk = KV_DSA * HEAD_DIM
    wi = H_IDX * D_IDX
    pf, pb = norm_matmul(x2, g, _pad_cols(w_in, 256).astype(BF16))
    pf3 = pf.reshape(b, t, -1)
    pb3 = pb.reshape(b, t, -1)
    p = 0 if past is None else past[0].shape[1]
    length = p + t
    lp = _round_up(length, KEY_PAD)
    pos = p + jnp.arange(t)

    c_k, c_v, c_qi, c_ki = wq, wq + wk, wq + 2 * wk, wq + 2 * wk + wi
    _, q_b = rotary(pf3, 0, wq, HEAD_DIM, pos, bf16_scale=LOG2E * HEAD_DIM ** -0.5)
    k_f, k_b = rotary(pf3, c_k // wk, wk, HEAD_DIM, pos)
    _, qi_b = rotary(pf3, c_qi // wi, wi, D_IDX, pos)
    ki_f, ki_b = rotary(pf3, c_ki // LANES, LANES, D_IDX, pos)
    k_rows = k_f.reshape(b, t, KV_DSA, HEAD_DIM)
    v_rows = pf3[:, :, c_v:c_v + wk].reshape(b, t, KV_DSA, HEAD_DIM)
    ki_rows = ki_f[:, :, :D_IDX]
    w_idx = pf3[:, :, c_ki + D_IDX:c_ki + D_IDX + H_IDX]

    tq = min(128, t)
    tk = KEY_PAD
    ki_all = _with_past(None if past is None else past[2], ki_b[:, :, :D_IDX], lp)
    bias = dsa_mask(qi_b, w_idx, ki_all, t, length, tq, tk)
    if past is None:
        o = dsa_attention(q_b, k_b, 0, pb3, c_v // wk, bias, t, length, tq, tk)
    else:
        k_all = _with_past(past[0], k_b, lp)
        v_all = _with_past(past[1], pb3[:, :, c_v:c_v + wk], lp)
        o = dsa_attention(q_b, k_all, 0, v_all, 0, bias, t, length, tq, tk)
    x_new = matmul_res(o.reshape(n, wq), w_out.astype(BF16), x2)
    return x_new, (k_rows, v_rows, ki_rows)


def _trunk(x, past_ab, past_dsa, norm_mix, norm_ffn, norm_final, w_in_ab, b_forget, w_out_ab,
           w_in_dsa, w_out_dsa, moe_w_group, moe_w_router, moe_w_gate, moe_w_up, moe_w_down):
    b, t, d = x.shape
    x2 = x.reshape(b * t, d)
    x2, rows_ab = _mixer_ab(x2, b, t, past_ab, norm_mix[0], w_in_ab, b_forget, w_out_ab)
    x2 = moe_block(x2, norm_ffn[0], moe_w_group[0], moe_w_router[0], moe_w_gate[0],
                   moe_w_up[0], moe_w_down[0])
    x2, rows_dsa = _mixer_dsa(x2, b, t, past_dsa, norm_mix[1], w_in_dsa, w_out_dsa)
    y = moe_block(x2, norm_ffn[1], moe_w_group[1], moe_w_router[1], moe_w_gate[1],
                  moe_w_up[1], moe_w_down[1], g_final=norm_final)
    return y.reshape(b, t, d), rows_ab, rows_dsa


def kernel(x_prompt, x_sample, cache_sb_k, cache_sb_v, cache_fox_k, cache_fox_v, cache_fox_logf,
           cache_dsa_k, cache_dsa_v, cache_dsa_idx_k, norm_mix, norm_ffn, norm_final,
           w_in_ab, b_forget, w_out_ab, w_in_dsa, w_out_dsa,
           moe_w_group, moe_w_router, moe_w_gate, moe_w_up, moe_w_down):
    weights = (norm_mix, norm_ffn, norm_final, w_in_ab, b_forget, w_out_ab, w_in_dsa, w_out_dsa,
               moe_w_group, moe_w_router, moe_w_gate, moe_w_up, moe_w_down)
    y_p, ab_p, dsa_p = _trunk(x_prompt, None, None, *weights)
    y_s, ab_s, dsa_s = _trunk(
        x_sample, (cache_sb_k, cache_sb_v, cache_fox_k, cache_fox_v, cache_fox_logf),
        (cache_dsa_k, cache_dsa_v, cache_dsa_idx_k), *weights)
    return (y_p, y_s) + ab_p + dsa_p + ab_s + dsa_s
```

```python
import functools
from typing import NamedTuple, Optional

import jax
import jax.numpy as jnp
import numpy as np
from jax import lax
from jax.experimental import pallas as pl
from jax.experimental.pallas import tpu as pltpu

CHUNK = 64
HEAD_DIM = 128
H_SB = 4
H_FOX = 4
H_DSA = 8
KV_DSA = 2
H_IDX = 4
D_IDX = 64
TOPK_MAX = 256
ROPE_THETA = 10000.0
EPS = 1e-6

LANES = 128
BF16_ROWS = 16

KEY_PAD = 512
LOG2E = 1.4426950408889634
M_INIT = -1e38
SB_DEAD_LOG2 = -150.0
INT32_MIN = -(2 ** 31)
NEG_INF_CODE = int(np.int32(np.uint32(0xFF800000) ^ np.uint32(0x7FFFFFFF)))

F32 = jnp.float32
BF16 = jnp.bfloat16
NT_DIMS = (((1,), (1,)), ((), ()))


def _round_up(a, b):
    return (a + b - 1) // b * b


def _cparams(semantics, vmem_mib=48):
    return pltpu.CompilerParams(dimension_semantics=semantics,
                                vmem_limit_bytes=vmem_mib * 1024 * 1024)


def _rms(x, g):
    return x * lax.rsqrt(jnp.mean(x * x, axis=-1, keepdims=True) + EPS) * g


def _log_sigmoid(z):
    return jnp.minimum(z, 0.0) - jnp.log1p(jnp.exp(-jnp.abs(z)))


def _split3(x):
    hi = x.astype(BF16)
    r1 = x - hi.astype(F32)
    mid = r1.astype(BF16)
    lo = (r1 - mid.astype(F32)).astype(BF16)
    return hi, mid, lo


class Segment(NamedTuple):
    start: int
    width: int
    rope: Optional[int] = None
    half: int = 0
    f32: bool = False
    bf16_scale: Optional[float] = None


PROJ_TILE = 256


def _rotate(x, cos, sin, half):
    if 2 * half == LANES:
        partner = pltpu.roll(x, half, 1)
    else:
        lane = lax.broadcasted_iota(jnp.int32, x.shape, 1)
        partner = jnp.where(lane % (2 * half) < half,
                            pltpu.roll(x, LANES - half, 1), pltpu.roll(x, half, 1))
    return x * cos + partner * sin


def _proj_kernel(x_ref, g_ref, w_ref, *refs, plan, n_tab):
    tabs, outs = refs[:n_tab], list(refs[n_tab:])
    h = _rms(x_ref[...], g_ref[...]).astype(BF16)
    for seg in plan:
        o_f32 = outs.pop(0) if seg.f32 else None
        o_b16 = outs.pop(0) if seg.bf16_scale is not None else None
        tile = min(PROJ_TILE, seg.width)
        for j in range(seg.width // tile):
            acc = jnp.dot(h, w_ref[:, seg.start + j * tile:seg.start + (j + 1) * tile],
                          preferred_element_type=F32)
            if seg.rope is not None:
                cos, sin = tabs[2 * seg.rope][...], tabs[2 * seg.rope + 1][...]
                acc = jnp.concatenate(
                    [_rotate(acc[:, gi * LANES:(gi + 1) * LANES], cos, sin, seg.half)
                     for gi in range(tile // LANES)], axis=1)
            cols = slice(j * tile, (j + 1) * tile)
            if o_f32 is not None:
                o_f32[:, cols] = acc
            if o_b16 is not None:
                o_b16[:, cols] = (acc * seg.bf16_scale).astype(BF16)


def norm_project(x, g, w, plan, tables=()):
    n, d = x.shape
    e = w.shape[1]
    tm = min(256, n)
    flat_tabs = [t for pair in tables for t in pair]
    tab_specs = []
    for tab in flat_tabs:
        period = tab.shape[0] // tm
        tab_specs.append(pl.BlockSpec((tm, LANES), lambda i, period=period: (i % period, 0)))
    out_specs, out_shape = [], []
    for seg in plan:
        for want, dt in ((seg.f32, F32), (seg.bf16_scale is not None, BF16)):
            if want:
                out_specs.append(pl.BlockSpec((tm, seg.width), lambda i: (i, 0)))
                out_shape.append(jax.ShapeDtypeStruct((n, seg.width), dt))
    return pl.pallas_call(
        functools.partial(_proj_kernel, plan=tuple(plan), n_tab=len(flat_tabs)),
        grid=(n // tm,),
        in_specs=[pl.BlockSpec((tm, d), lambda i: (i, 0)),
                  pl.BlockSpec((1, d), lambda i: (0, 0)),
                  pl.BlockSpec((d, e), lambda i: (0, 0))] + tab_specs,
        out_specs=out_specs,
        out_shape=out_shape,
        compiler_params=_cparams(("parallel",)),
    )(x, g.reshape(1, d), w, *flat_tabs)


def _matmul_res_kernel(*refs, n_in):
    a_refs, w_ref, r_ref, o_ref = refs[:n_in], refs[n_in], refs[n_in + 1], refs[n_in + 2]
    acc = r_ref[...]
    k0 = 0
    for a_ref in a_refs:
        k = a_ref.shape[1]
        acc = acc + jnp.dot(a_ref[...], w_ref[k0:k0 + k, :], preferred_element_type=F32)
        k0 += k
    o_ref[...] = acc


def matmul_res(a_list, w, res):
    n, d = res.shape
    tm = min(512, n)
    return pl.pallas_call(
        functools.partial(_matmul_res_kernel, n_in=len(a_list)),
        grid=(n // tm,),
        in_specs=[pl.BlockSpec((tm, a.shape[1]), lambda i: (i, 0)) for a in a_list]
        + [pl.BlockSpec(w.shape, lambda i: (0, 0)),
           pl.BlockSpec((tm, d), lambda i: (i, 0))],
        out_specs=pl.BlockSpec((tm, d), lambda i: (i, 0)),
        out_shape=jax.ShapeDtypeStruct((n, d), F32),
        compiler_params=_cparams(("parallel",)),
    )(*a_list, w, res)


def _logf_cumsum_kernel(pre_ref, b_ref, logf_ref, hi_ref, mid_ref, lo_ref, carry,
                        *, past, length, cw):
    j = pl.program_id(0)

    @pl.when(j == 0)
    def _():
        carry[...] = jnp.zeros_like(carry)

    pre = pre_ref[...]
    pos = j * cw + lax.broadcasted_iota(jnp.int32, pre.shape, 1)
    x = jnp.where(pos >= past, _log_sigmoid(pre + b_ref[...]), pre)
    x = jnp.where(pos < length, x, 0.0)
    logf_ref[...] = x
    row = lax.broadcasted_iota(jnp.int32, (cw, cw), 0)
    col = lax.broadcasted_iota(jnp.int32, (cw, cw), 1)
    upper = jnp.where(row <= col, 1.0, 0.0).astype(BF16)
    hi, mid, lo = _split3(x)
    cs = (jnp.dot(hi, upper, preferred_element_type=F32)
          + jnp.dot(mid, upper, preferred_element_type=F32)
          + jnp.dot(lo, upper, preferred_element_type=F32))
    cum = cs + carry[:, 0:1]
    carry[...] = jnp.broadcast_to(cum[:, cw - 1:cw], carry.shape)
    hi_ref[...], mid_ref[...], lo_ref[...] = _split3(cum * LOG2E)


def logf_cumsum(pre, b_col, past, length):
    r, lp = pre.shape
    cw = KEY_PAD
    blk = pl.BlockSpec((r, cw), lambda j: (0, j))
    return pl.pallas_call(
        functools.partial(_logf_cumsum_kernel, past=past, length=length, cw=cw),
        grid=(lp // cw,),
        in_specs=[blk, pl.BlockSpec((r, 1), lambda j: (0, 0))],
        out_specs=[blk, blk, blk, blk],
        out_shape=[jax.ShapeDtypeStruct((r, lp), F32)] + [jax.ShapeDtypeStruct((r, lp), BF16)] * 3,
        scratch_shapes=[pltpu.VMEM((r, LANES), F32)],
        compiler_params=_cparams(("arbitrary",)),
    )(pre, b_col)


def _softmax_rows(s_parts, m_old, l_old):
    mx = s_parts[0]
    for sp in s_parts[1:]:
        mx = jnp.maximum(mx, sp)
    m_new = jnp.maximum(m_old, jnp.max(mx, axis=-1, keepdims=True))
    alpha = jnp.exp2(m_old - m_new)
    ps = [jnp.exp2(sp - m_new) for sp in s_parts]
    tot = ps[0]
    for p in ps[1:]:
        tot = tot + p
    return ps, m_new, alpha * l_old + tot, alpha


def _sb_kernel(q_ref, k_ref, v_ref, o_ref, s_scr, b_scr, hi_scr, lo_scr, p_scr, tri, acc, carry,
               *, off, tq, tk, heads):
    qi = pl.program_id(1)
    q0 = off + qi * tq
    kb_last = (q0 + tq - 2) // tk
    ng = tk // LANES
    nchunk = tq // BF16_ROWS

    acc[...] = jnp.zeros_like(acc)
    carry[...] = jnp.zeros_like(carry)
    row = lax.broadcasted_iota(jnp.int32, (tk, tk), 0)
    col = lax.broadcasted_iota(jnp.int32, (tk, tk), 1)
    tri[...] = jnp.where(row > col, 1.0, 0.0).astype(BF16)

    def block(state):
        kb, _ = state
        k0 = pl.multiple_of(kb * tk, tk)
        kpos = k0 + lax.broadcasted_iota(jnp.int32, (1, LANES), 1)
        for h in range(heads):
            sl = slice(h * HEAD_DIM, (h + 1) * HEAD_DIM)
            s_scr[h] = lax.dot_general(q_ref[0, :, sl], k_ref[0, pl.ds(k0, tk), sl], NT_DIMS,
                                       preferred_element_type=F32)
        for h in range(heads):
            for c in range(nchunk):
                rows = slice(c * BF16_ROWS, (c + 1) * BF16_ROWS)
                qpos = q0 + c * BF16_ROWS + lax.broadcasted_iota(jnp.int32, (BF16_ROWS, 1), 0)
                for g in range(ng):
                    ls = slice(g * LANES, (g + 1) * LANES)
                    z = s_scr[h, rows, ls]
                    lsig = jnp.minimum(z, 0.0) - jnp.log2(1.0 + jnp.exp2(-jnp.abs(z)))
                    l1m = jnp.where(kpos + g * LANES < qpos, lsig - z, 0.0)
                    hi = l1m.astype(BF16)
                    hi_scr[h, rows, ls] = hi
                    lo_scr[h, rows, ls] = (l1m - hi.astype(F32)).astype(BF16)
                    s_scr[h, rows, ls] = lsig
        for h in range(heads):
            b_scr[h] = (jnp.dot(hi_scr[h], tri[...], preferred_element_type=F32)
                        + jnp.dot(lo_scr[h], tri[...], preferred_element_type=F32))
        for h in range(heads):
            for c in range(nchunk):
                rows = slice(c * BF16_ROWS, (c + 1) * BF16_ROWS)
                qpos = q0 + c * BF16_ROWS + lax.broadcasted_iota(jnp.int32, (BF16_ROWS, 1), 0)
                run = carry[h, rows, :]
                for g in range(ng):
                    ls = slice(g * LANES, (g + 1) * LANES)
                    a = jnp.exp2(s_scr[h, rows, ls] + b_scr[h, rows, ls] + run)
                    a = jnp.where(kpos + g * LANES < qpos, a, 0.0)
                    p_scr[h, rows, ls] = a.astype(BF16)
                first = (b_scr[h, rows, 0:1] + hi_scr[h, rows, 0:1].astype(F32)
                         + lo_scr[h, rows, 0:1].astype(F32))
                carry[h, rows, :] = run + first
        for h in range(heads):
            sl = slice(h * HEAD_DIM, (h + 1) * HEAD_DIM)
            acc[h] += jnp.dot(p_scr[h], v_ref[0, pl.ds(k0, tk), sl], preferred_element_type=F32)
        return kb - 1, jnp.max(carry[...]) > SB_DEAD_LOG2

    lax.while_loop(lambda st: jnp.logical_and(st[0] >= 0, st[1]), block,
                   (kb_last, jnp.bool_(True)))
    for h in range(heads):
        o_ref[0, :, h * HEAD_DIM:(h + 1) * HEAD_DIM] = acc[h].astype(o_ref.dtype)


def sb_attention(q, k_arr, kc, v_arr, vc, t, length, tq, tk):
    b = q.shape[0]
    lp = k_arr.shape[1]
    w = H_SB * HEAD_DIM
    return pl.pallas_call(
        functools.partial(_sb_kernel, off=length - t, tq=tq, tk=tk, heads=H_SB),
        grid=(b, t // tq),
        in_specs=[pl.BlockSpec((1, tq, w), lambda bi, qi: (bi, qi, 0)),
                  pl.BlockSpec((1, lp, w), lambda bi, qi: (bi, 0, kc)),
                  pl.BlockSpec((1, lp, w), lambda bi, qi: (bi, 0, vc))],
        out_specs=pl.BlockSpec((1, tq, w), lambda bi, qi: (bi, qi, 0)),
        out_shape=jax.ShapeDtypeStruct((b, t, w), BF16),
        scratch_shapes=[pltpu.VMEM((H_SB, tq, tk), F32),
                        pltpu.VMEM((H_SB, tq, tk), F32),
                        pltpu.VMEM((H_SB, tq, tk), BF16),
                        pltpu.VMEM((H_SB, tq, tk), BF16),
                        pltpu.VMEM((H_SB, tq, tk), BF16),
                        pltpu.VMEM((tk, tk), BF16),
                        pltpu.VMEM((H_SB, tq, HEAD_DIM), F32),
                        pltpu.VMEM((H_SB, tq, LANES), F32)],
        compiler_params=_cparams(("parallel", "arbitrary")),
    )(q, k_arr, v_arr)


def _fox_kernel(q_ref, qb_ref, k_ref, kb_ref, v_ref, o_ref, s_scr, p_scr, acc, m_scr, l_scr, a_scr,
                *, off, tq, tk, heads):
    qi = pl.program_id(1)
    q0 = off + qi * tq
    kb_last = (q0 + tq - 1) // tk
    ng = tk // LANES
    nchunk = tq // BF16_ROWS

    acc[...] = jnp.zeros_like(acc)
    m_scr[...] = jnp.full_like(m_scr, M_INIT)
    l_scr[...] = jnp.zeros_like(l_scr)

    def block(kb, _):
        k0 = pl.multiple_of(kb * tk, tk)
        for h in range(heads):
            sl = slice(h * HEAD_DIM, (h + 1) * HEAD_DIM)
            q_aug = jnp.concatenate([q_ref[0, :, sl], qb_ref[0, :, sl]], axis=1)
            k_aug = jnp.concatenate([k_ref[0, pl.ds(k0, tk), sl], kb_ref[0, pl.ds(k0, tk), sl]],
                                    axis=1)
            s_scr[h] = lax.dot_general(q_aug, k_aug, NT_DIMS, preferred_element_type=F32)

        @pl.when(k0 + tk - 1 > q0)
        def _():
            qpos = q0 + lax.broadcasted_iota(jnp.int32, (tq, 1), 0)
            kpos = k0 + lax.broadcasted_iota(jnp.int32, (1, tk), 1)
            for h in range(heads):
                s_scr[h] = jnp.where(kpos <= qpos, s_scr[h], -jnp.inf)

        for h in range(heads):
            for c in range(nchunk):
                rows = slice(c * BF16_ROWS, (c + 1) * BF16_ROWS)
                parts = [s_scr[h, rows, g * LANES:(g + 1) * LANES] for g in range(ng)]
                ps, m_new, l_new, alpha = _softmax_rows(parts, m_scr[h, rows, :], l_scr[h, rows, :])
                m_scr[h, rows, :] = m_new
                l_scr[h, rows, :] = l_new
                a_scr[h, rows, :] = alpha
                for g in range(ng):
                    p_scr[h, rows, g * LANES:(g + 1) * LANES] = ps[g].astype(BF16)
        for h in range(heads):
            sl = slice(h * HEAD_DIM, (h + 1) * HEAD_DIM)
            acc[h] = a_scr[h] * acc[h] + jnp.dot(p_scr[h], v_ref[0, pl.ds(k0, tk), sl],
                                                 preferred_element_type=F32)
        return 0

    lax.fori_loop(0, kb_last + 1, block, 0)
    for h in range(heads):
        l = jnp.sum(l_scr[h], axis=-1, keepdims=True)
        o_ref[0, :, h * HEAD_DIM:(h + 1) * HEAD_DIM] = (acc[h] / l).astype(o_ref.dtype)


def _fox_bias_cols(cols):
    b, r, h, n = cols.shape
    pad = jnp.zeros((b, r, h, HEAD_DIM - n), BF16)
    return jnp.concatenate([cols, pad], axis=-1).reshape(b, r, h * HEAD_DIM)


def fox_attention(q, q_bias, k_arr, kc, k_bias, v_arr, vc, t, length, tq, tk):
    b = q.shape[0]
    lp = k_arr.shape[1]
    w = H_FOX * HEAD_DIM
    return pl.pallas_call(
        functools.partial(_fox_kernel, off=length - t, tq=tq, tk=tk, heads=H_FOX),
        grid=(b, t // tq),
        in_specs=[pl.BlockSpec((1, tq, w), lambda bi, qi: (bi, qi, 0)),
                  pl.BlockSpec((1, tq, w), lambda bi, qi: (bi, qi, 0)),
                  pl.BlockSpec((1, lp, w), lambda bi, qi: (bi, 0, kc)),
                  pl.BlockSpec((1, lp, w), lambda bi, qi: (bi, 0, 0)),
                  pl.BlockSpec((1, lp, w), lambda bi, qi: (bi, 0, vc))],
        out_specs=pl.BlockSpec((1, tq, w), lambda bi, qi: (bi, qi, 0)),
        out_shape=jax.ShapeDtypeStruct((b, t, w), BF16),
        scratch_shapes=[pltpu.VMEM((H_FOX, tq, tk), F32),
                        pltpu.VMEM((H_FOX, tq, tk), BF16),
                        pltpu.VMEM((H_FOX, tq, HEAD_DIM), F32),
                        pltpu.VMEM((H_FOX, tq, LANES), F32),
                        pltpu.VMEM((H_FOX, tq, LANES), F32),
                        pltpu.VMEM((H_FOX, tq, LANES), F32)],
        compiler_params=_cparams(("parallel", "arbitrary")),
    )(q, q_bias, k_arr, k_bias, v_arr)


def _moe_kernel(x_ref, g_ref, wr_ref, wg_ref, wu_ref, wd_ref, gf_ref, o_ref, hb, comb,
                *, n_groups, per_group, final_norm):
    e = pl.program_id(1)
    n_exp = n_groups * per_group
    lane = lax.broadcasted_iota(jnp.int32, comb.shape, 1)

    @pl.when(e == 0)
    def _():
        x = x_ref[...]
        h = _rms(x, g_ref[...])
        hb[...] = h.astype(BF16)
        logits = jnp.dot(h, wr_ref[...], precision=lax.Precision.HIGHEST,
                         preferred_element_type=F32)
        gl = jnp.where(lane < n_groups, logits, -jnp.inf)
        gmax = jnp.max(gl, axis=-1, keepdims=True)
        gsel = jnp.min(jnp.where(gl == gmax, lane, LANES), axis=-1, keepdims=True)
        p_group = 1.0 / jnp.sum(jnp.where(lane < n_groups, jnp.exp(gl - gmax), 0.0),
                                axis=-1, keepdims=True)
        first = n_groups + gsel * per_group
        el = jnp.where((lane >= first) & (lane < first + per_group), logits, -jnp.inf)
        v1 = jnp.max(el, axis=-1, keepdims=True)
        i1 = jnp.min(jnp.where(el == v1, lane, LANES), axis=-1, keepdims=True)
        el2 = jnp.where(lane == i1, -jnp.inf, el)
        v2 = jnp.max(el2, axis=-1, keepdims=True)
        i2 = jnp.min(jnp.where(el2 == v2, lane, LANES), axis=-1, keepdims=True)
        ratio = jnp.exp(v2 - v1)
        gate1 = p_group / (1.0 + ratio)
        comb[...] = jnp.where(lane == i1, gate1, jnp.where(lane == i2, gate1 * ratio, 0.0))
        o_ref[...] = x

    c = jnp.sum(jnp.where(lane == e + n_groups, comb[...], 0.0), axis=-1, keepdims=True)
    hv = hb[...]
    gate = jnp.dot(hv, wg_ref[0], preferred_element_type=F32)
    up = jnp.dot(hv, wu_ref[0], preferred_element_type=F32)
    act = gate * (1.0 / (1.0 + jnp.exp(-gate))) * up * c
    o_ref[...] += jnp.dot(act.astype(BF16), wd_ref[0], preferred_element_type=F32)

    if final_norm:
        @pl.when(e == n_exp - 1)
        def _():
            o_ref[...] = _rms(o_ref[...], gf_ref[...])


def moe_block(x, g, w_group, w_router, w_gate, w_up, w_down, g_final=None):
    n, d = x.shape
    n_groups, per_group = w_router.shape[1], w_router.shape[2]
    n_exp, _, f = w_gate.shape
    w_route = jnp.concatenate(
        [w_group, w_router.reshape(d, n_exp),
         jnp.zeros((d, LANES - n_groups - n_exp), F32)], axis=1)
    final_norm = g_final is not None
    gf = (g_final if final_norm else g).reshape(1, d)
    tm = min(1024, n)
    return pl.pallas_call(
        functools.partial(_moe_kernel, n_groups=n_groups, per_group=per_group,
                          final_norm=final_norm),
        grid=(n // tm, n_exp),
        in_specs=[pl.BlockSpec((tm, d), lambda i, e: (i, 0)),
                  pl.BlockSpec((1, d), lambda i, e: (0, 0)),
                  pl.BlockSpec((d, LANES), lambda i, e: (0, 0)),
                  pl.BlockSpec((1, d, f), lambda i, e: (e, 0, 0)),
                  pl.BlockSpec((1, d, f), lambda i, e: (e, 0, 0)),
                  pl.BlockSpec((1, f, d), lambda i, e: (e, 0, 0)),
                  pl.BlockSpec((1, d), lambda i, e: (0, 0))],
        out_specs=pl.BlockSpec((tm, d), lambda i, e: (i, 0)),
        out_shape=jax.ShapeDtypeStruct((n, d), F32),
        scratch_shapes=[pltpu.VMEM((tm, d), BF16), pltpu.VMEM((tm, LANES), F32)],
        compiler_params=_cparams(("parallel", "arbitrary")),
    )(x, g.reshape(1, d), w_route, w_gate.astype(BF16), w_up.astype(BF16),
      w_down.astype(BF16), gf)


def _rope_tables(pos, head_dim, rows, live_lanes=LANES):
    half = head_dim // 2
    inv_freq = ROPE_THETA ** (-jnp.arange(half, dtype=F32) / half)
    ang = pos.astype(F32)[:, None] * inv_freq[None, :]
    cos, sin = jnp.cos(ang), jnp.sin(ang)
    reps = LANES // head_dim
    cos = jnp.tile(jnp.concatenate([cos, cos], axis=1), (1, reps))
    sin = jnp.tile(jnp.concatenate([-sin, sin], axis=1), (1, reps))
    live = jnp.arange(LANES) < live_lanes
    cos = jnp.where(live, cos, 1.0)
    sin = jnp.where(live, sin, 0.0)
    reps_rows = max(rows // pos.shape[0], 1)
    return jnp.tile(cos, (reps_rows, 1)), jnp.tile(sin, (reps_rows, 1))


def _code_to_float(code):
    return pltpu.bitcast(code ^ ((code >> 31) & 0x7FFFFFFF), F32)


def _dsa_reach(q_first, tq, length):
    return jnp.minimum(((q_first + tq - 1) // CHUNK + 1) * CHUNK, length)


def _dsa_mask_kernel(qi_ref, w_ref, ki_ref, m_ref, sc, *, off, length, tq, ksel, cw):
    qb = pl.program_id(1)
    q0 = off + qb * tq
    qchunk = (q0 + lax.broadcasted_iota(jnp.int32, (tq, 1), 0)) // CHUNK
    nck = (_dsa_reach(q0, tq, length) + cw - 1) // cw
    w = w_ref[0] * (H_IDX ** -0.5)
    ng = cw // LANES
    kf = float(ksel)

    def fill(c, _):
        k0 = pl.multiple_of(c * cw, cw)
        kk = ki_ref[0, pl.ds(k0, cw), :]
        score = jnp.zeros((tq, cw), F32)
        for h in range(H_IDX):
            lg = lax.dot_general(qi_ref[0, :, h * D_IDX:(h + 1) * D_IDX], kk, NT_DIMS,
                                 preferred_element_type=F32) * (D_IDX ** -0.5)
            score = score + jnp.maximum(lg, 0.0) * w[:, h:h + 1]
        kpos = k0 + lax.broadcasted_iota(jnp.int32, (1, cw), 1)
        admissible = (kpos // CHUNK <= qchunk) & (kpos < length)
        sc[c] = jnp.where(admissible, score + 0.0, -jnp.inf)
        return 0

    lax.fori_loop(0, nck, fill, 0)

    def blank(c, _):
        m_ref[0, c] = jnp.full((tq, cw), -jnp.inf, m_ref.dtype)
        return 0

    lax.fori_loop(nck, m_ref.shape[1], blank, 0)

    def count(pred):
        def body(c, tot):
            for g in range(ng):
                tot = tot + jnp.where(pred(sc[c, :, g * LANES:(g + 1) * LANES]), 1.0, 0.0)
            return tot
        tot = lax.fori_loop(0, nck, body, jnp.zeros((tq, LANES), F32))
        return jnp.sum(tot, axis=-1, keepdims=True)

    code0 = jnp.where(count(lambda x: x >= 0.0) >= kf, 0, INT32_MIN).astype(jnp.int32)

    def bit_step(i, code):
        cand = code + lax.shift_left(jnp.int32(1), 30 - i)
        cf = _code_to_float(cand)
        return jnp.where(count(lambda x: x >= cf) >= kf, cand, code)

    code = lax.fori_loop(0, 31, bit_step, code0)
    thr = _code_to_float(jnp.maximum(code, NEG_INF_CODE + 1))
    n_ge = count(lambda x: x >= thr)
    n_gt = count(lambda x: x > thr)
    tied = jnp.max(n_ge) > kf

    @pl.when(jnp.logical_not(tied))
    def _():
        def emit(c, _):
            m_ref[0, c] = jnp.where(sc[c] >= thr, 0.0, -jnp.inf).astype(m_ref.dtype)
            return 0
        lax.fori_loop(0, nck, emit, 0)

    @pl.when(tied)
    def _():
        need = kf - n_gt
        row = lax.broadcasted_iota(jnp.int32, (LANES, LANES), 0)
        col = lax.broadcasted_iota(jnp.int32, (LANES, LANES), 1)
        before = jnp.where(row < col, 1.0, 0.0).astype(BF16)

        def emit(c, run):
            for g in range(ng):
                sl = slice(g * LANES, (g + 1) * LANES)
                x = sc[c, :, sl]
                eq = jnp.where(x == thr, 1.0, 0.0)
                rank = run + jnp.dot(eq.astype(BF16), before, preferred_element_type=F32)
                sel = (x > thr) | ((x == thr) & (rank < need))
                m_ref[0, c, :, sl] = jnp.where(sel, 0.0, -jnp.inf).astype(m_ref.dtype)
                run = run + jnp.sum(eq, axis=-1, keepdims=True)
            return run
        lax.fori_loop(0, nck, emit, jnp.zeros((tq, 1), F32))


def dsa_mask(qi, w_idx, ki, t, length, tq, cw):
    b, lp, _ = ki.shape
    ksel = min(TOPK_MAX, length // 4)
    nch = lp // cw
    return pl.pallas_call(
        functools.partial(_dsa_mask_kernel, off=length - t, length=length, tq=tq,
                          ksel=ksel, cw=cw),
        grid=(b, t // tq),
        in_specs=[pl.BlockSpec((1, tq, H_IDX * D_IDX), lambda bi, i: (bi, i, 0)),
                  pl.BlockSpec((1, tq, H_IDX), lambda bi, i: (bi, i, 0)),
                  pl.BlockSpec((1, lp, D_IDX), lambda bi, i: (bi, 0, 0))],
        out_specs=pl.BlockSpec((1, nch, tq, cw), lambda bi, i: (bi, 0, i, 0)),
        out_shape=jax.ShapeDtypeStruct((b, nch, t, cw), BF16),
        scratch_shapes=[pltpu.VMEM((nch, tq, cw), F32)],
        compiler_params=_cparams(("parallel", "arbitrary")),
    )(qi, w_idx, ki)


def _dsa_attn_kernel(q_ref, k_ref, v_ref, m_ref, o_ref, qs, s_scr, p_scr, acc, m_scr, l_scr, a_scr,
                     *, off, length, tq, tk):
    qb = pl.program_id(1)
    rep = H_DSA // KV_DSA
    nkb = (_dsa_reach(off + qb * tq, tq, length) + tk - 1) // tk
    ng = tk // LANES
    nchunk = tq // BF16_ROWS

    for g in range(KV_DSA):
        for r in range(rep):
            hd = g * rep + r
            qs[g, r * tq:(r + 1) * tq, :] = q_ref[0, :, hd * HEAD_DIM:(hd + 1) * HEAD_DIM]
    acc[...] = jnp.zeros_like(acc)
    m_scr[...] = jnp.full_like(m_scr, M_INIT)
    l_scr[...] = jnp.zeros_like(l_scr)

    def block(kb, _):
        k0 = pl.multiple_of(kb * tk, tk)
        for g in range(KV_DSA):
            sl = slice(g * HEAD_DIM, (g + 1) * HEAD_DIM)
            s_scr[g] = lax.dot_general(qs[g], k_ref[0, pl.ds(k0, tk), sl], NT_DIMS,
                                       preferred_element_type=F32)
        for c in range(nchunk):
            bias = [m_ref[0, kb, c * BF16_ROWS:(c + 1) * BF16_ROWS,
                          gl * LANES:(gl + 1) * LANES].astype(F32) for gl in range(ng)]
            for g in range(KV_DSA):
                for r in range(rep):
                    rows = slice(r * tq + c * BF16_ROWS, r * tq + (c + 1) * BF16_ROWS)
                    parts = [s_scr[g, rows, gl * LANES:(gl + 1) * LANES] + bias[gl]
                             for gl in range(ng)]
                    ps, m_new, l_new, alpha = _softmax_rows(parts, m_scr[g, rows, :],
                                                            l_scr[g, rows, :])
                    m_scr[g, rows, :] = m_new
                    l_scr[g, rows, :] = l_new
                    a_scr[g, rows, :] = alpha
                    for gl in range(ng):
                        p_scr[g, rows, gl * LANES:(gl + 1) * LANES] = ps[gl].astype(BF16)
        for g in range(KV_DSA):
            sl = slice(g * HEAD_DIM, (g + 1) * HEAD_DIM)
            acc[g] = a_scr[g] * acc[g] + jnp.dot(p_scr[g], v_ref[0, pl.ds(k0, tk), sl],
                                                 preferred_element_type=F32)
        return 0

    lax.fori_loop(0, nkb, block, 0)
    for g in range(KV_DSA):
        o = acc[g] / jnp.sum(l_scr[g], axis=-1, keepdims=True)
        for r in range(rep):
            hd = g * rep + r
            o_ref[0, :, hd * HEAD_DIM:(hd + 1) * HEAD_DIM] = (
                o[r * tq:(r + 1) * tq]).astype(o_ref.dtype)


def dsa_attention(q, k_arr, kc, v_arr, vc, bias, t, length, tq, tk):
    b = q.shape[0]
    lp = k_arr.shape[1]
    wq = H_DSA * HEAD_DIM
    wk = KV_DSA * HEAD_DIM
    rep = H_DSA // KV_DSA
    nch = lp // tk
    return pl.pallas_call(
        functools.partial(_dsa_attn_kernel, off=length - t, length=length, tq=tq, tk=tk),
        grid=(b, t // tq),
        in_specs=[pl.BlockSpec((1, tq, wq), lambda bi, qb: (bi, qb, 0)),
                  pl.BlockSpec((1, lp, wk), lambda bi, qb: (bi, 0, kc)),
                  pl.BlockSpec((1, lp, wk), lambda bi, qb: (bi, 0, vc)),
                  pl.BlockSpec((1, nch, tq, tk), lambda bi, qb: (bi, 0, qb, 0))],
        out_specs=pl.BlockSpec((1, tq, wq), lambda bi, qb: (bi, qb, 0)),
        out_shape=jax.ShapeDtypeStruct((b, t, wq), BF16),
        scratch_shapes=[pltpu.VMEM((KV_DSA, rep * tq, HEAD_DIM), BF16),
                        pltpu.VMEM((KV_DSA, rep * tq, tk), F32),
                        pltpu.VMEM((KV_DSA, rep * tq, tk), BF16),
                        pltpu.VMEM((KV_DSA, rep * tq, HEAD_DIM), F32),
                        pltpu.VMEM((KV_DSA, rep * tq, LANES), F32),
                        pltpu.VMEM((KV_DSA, rep * tq, LANES), F32),
                        pltpu.VMEM((KV_DSA, rep * tq, LANES), F32)],
        compiler_params=_cparams(("parallel", "arbitrary")),
    )(q, k_arr, v_arr, bias)


def _with_past(past, new, lp):
    b = new.shape[0]
    parts = [] if past is None else [past.reshape(b, past.shape[1], -1).astype(BF16)]
    parts.append(new.astype(BF16))
    rows = sum(p.shape[1] for p in parts)
    if rows < lp:
        parts.append(jnp.zeros((b, lp - rows, new.shape[2]), BF16))
    return parts[0] if len(parts) == 1 else jnp.concatenate(parts, axis=1)


def _mixer_ab(x2, b, t, past, g, w_in, b_forget, w_out):
    wsb = H_SB * HEAD_DIM
    wfx = H_FOX * HEAD_DIM
    qscale = LOG2E * HEAD_DIM ** -0.5
    c_fx = 3 * wsb
    c_gate = c_fx + 3 * wfx
    plan = [Segment(0, wsb, bf16_scale=qscale),
            Segment(wsb, wsb, f32=True, bf16_scale=1.0),
            Segment(2 * wsb, wsb, f32=True, bf16_scale=1.0),
            Segment(c_fx, wfx, bf16_scale=qscale),
            Segment(c_fx + wfx, wfx, f32=True, bf16_scale=1.0),
            Segment(c_fx + 2 * wfx, wfx, f32=True, bf16_scale=1.0),
            Segment(c_gate, LANES, f32=True)]
    w_pad = jnp.pad(w_in, ((0, 0), (0, c_gate + LANES - w_in.shape[1]))).astype(BF16)
    (q_sb, k_sb, k_sb_b, v_sb, v_sb_b, q_fx, k_fx, k_fx_b, v_fx, v_fx_b, gate) = norm_project(
        x2, g, w_pad, plan)
    rows = tuple(a.reshape(b, t, -1, HEAD_DIM) for a in (k_sb, v_sb, k_fx, v_fx))
    f_logit = gate.reshape(b, t, LANES)[:, :, :H_FOX]

    p = 0 if past is None else past[0].shape[1]
    length = p + t
    lp = _round_up(length, KEY_PAD)
    new_t = jnp.swapaxes(f_logit, 1, 2)
    parts = [new_t] if past is None else [jnp.swapaxes(past[4].astype(F32), 1, 2), new_t]
    if length < lp:
        parts.append(jnp.zeros((b, H_FOX, lp - length), F32))
    pre = (parts[0] if len(parts) == 1 else jnp.concatenate(parts, axis=2)).reshape(b * H_FOX, lp)
    b_col = jnp.tile(b_forget.astype(F32), b).reshape(b * H_FOX, 1)
    logf_t, *cum_parts = logf_cumsum(pre, b_col, p, length)
    logf = jnp.swapaxes(logf_t.reshape(b, H_FOX, lp)[:, :, p:length], 1, 2)

    f_parts = jnp.stack([jnp.swapaxes(c.reshape(b, H_FOX, lp), 1, 2) for c in cum_parts],
                        axis=-1)
    ones3 = jnp.ones((b, lp, H_FOX, 3), BF16)
    q_bias = _fox_bias_cols(jnp.concatenate([f_parts[:, p:length], ones3[:, :t]], axis=-1))
    k_bias = _fox_bias_cols(jnp.concatenate([ones3, -f_parts], axis=-1))

    as3 = lambda a: a.reshape(b, t, -1)
    ks, vs, kf, vf = (_with_past(None if past is None else past[i], as3(a), lp)
                      for i, a in enumerate((k_sb_b, v_sb_b, k_fx_b, v_fx_b)))
    tq = min(256, t)
    o_sb = sb_attention(as3(q_sb), ks, 0, vs, 0, t, length, tq, 256)
    o_fx = fox_attention(as3(q_fx), q_bias, kf, 0, k_bias, vf, 0, t, length, tq, 512)
    x_new = matmul_res([o_sb.reshape(b * t, wsb), o_fx.reshape(b * t, wfx)],
                       w_out.astype(BF16), x2)
    return x_new, rows + (logf,)


def _mixer_dsa(x2, b, t, past, g, w_in, w_out):
    n = b * t
    wq = H_DSA * HEAD_DIM
    wk = KV_DSA * HEAD_DIM
    wi = H_IDX * D_IDX
    p = 0 if past is None else past[0].shape[1]
    length = p + t
    lp = _round_up(length, KEY_PAD)
    pos = p + jnp.arange(t)
    tab_rows = max(t, min(256, n))
    tables = [_rope_tables(pos, HEAD_DIM, tab_rows),
              _rope_tables(pos, D_IDX, tab_rows),
              _rope_tables(pos, D_IDX, tab_rows, live_lanes=D_IDX)]
    c_k, c_v, c_qi, c_ki = wq, wq + wk, wq + 2 * wk, wq + 2 * wk + wi
    plan = [Segment(0, wq, rope=0, half=HEAD_DIM // 2, bf16_scale=LOG2E * HEAD_DIM ** -0.5),
            Segment(c_k, wk, rope=0, half=HEAD_DIM // 2, f32=True, bf16_scale=1.0),
            Segment(c_v, wk, f32=True, bf16_scale=1.0),
            Segment(c_qi, wi, rope=1, half=D_IDX // 2, bf16_scale=1.0),
            Segment(c_ki, LANES, rope=2, half=D_IDX // 2, f32=True, bf16_scale=1.0)]
    w_pad = jnp.pad(w_in, ((0, 0), (0, c_ki + LANES - w_in.shape[1]))).astype(BF16)
    q_b, k_f, k_b, v_f, v_b, qi_b, kw_f, kw_b = norm_project(x2, g, w_pad, plan, tables)
    as3 = lambda a: a.reshape(b, t, -1)
    k_rows = k_f.reshape(b, t, KV_DSA, HEAD_DIM)
    v_rows = v_f.reshape(b, t, KV_DSA, HEAD_DIM)
    ki_rows = as3(kw_f)[:, :, :D_IDX]
    w_idx = as3(kw_f)[:, :, D_IDX:D_IDX + H_IDX]

    tk = KEY_PAD
    ki_all = _with_past(None if past is None else past[2], as3(kw_b)[:, :, :D_IDX], lp)
    k_all = _with_past(None if past is None else past[0], as3(k_b), lp)
    v_all = _with_past(None if past is None else past[1], as3(v_b), lp)
    bias = dsa_mask(as3(qi_b), w_idx, ki_all, t, length, min(256, t), tk)
    o = dsa_attention(as3(q_b), k_all, 0, v_all, 0, bias, t, length, min(128, t), tk)
    x_new = matmul_res([o.reshape(n, wq)], w_out.astype(BF16), x2)
    return x_new, (k_rows, v_rows, ki_rows)


def _trunk(x, past_ab, past_dsa, norm_mix, norm_ffn, norm_final, w_in_ab, b_forget, w_out_ab,
           w_in_dsa, w_out_dsa, moe_w_group, moe_w_router, moe_w_gate, moe_w_up, moe_w_down):
    b, t, d = x.shape
    x2 = x.reshape(b * t, d)
    x2, rows_ab = _mixer_ab(x2, b, t, past_ab, norm_mix[0], w_in_ab, b_forget, w_out_ab)
    x2 = moe_block(x2, norm_ffn[0], moe_w_group[0], moe_w_router[0], moe_w_gate[0],
                   moe_w_up[0], moe_w_down[0])
    x2, rows_dsa = _mixer_dsa(x2, b, t, past_dsa, norm_mix[1], w_in_dsa, w_out_dsa)
    y = moe_block(x2, norm_ffn[1], moe_w_group[1], moe_w_router[1], moe_w_gate[1],
                  moe_w_up[1], moe_w_down[1], g_final=norm_final)
    return y.reshape(b, t, d), rows_ab, rows_dsa


def kernel(x_prompt, x_sample, cache_sb_k, cache_sb_v, cache_fox_k, cache_fox_v, cache_fox_logf,
           cache_dsa_k, cache_dsa_v, cache_dsa_idx_k, norm_mix, norm_ffn, norm_final,
           w_in_ab, b_forget, w_out_ab, w_in_dsa, w_out_dsa,
           moe_w_group, moe_w_router, moe_w_gate, moe_w_up, moe_w_down):
    weights = (norm_mix, norm_ffn, norm_final, w_in_ab, b_forget, w_out_ab, w_in_dsa, w_out_dsa,
               moe_w_group, moe_w_router, moe_w_gate, moe_w_up, moe_w_down)
    y_p, ab_p, dsa_p = _trunk(x_prompt, None, None, *weights)
    y_s, ab_s, dsa_s = _trunk(
        x_sample, (cache_sb_k, cache_sb_v, cache_fox_k, cache_fox_v, cache_fox_logf),
        (cache_dsa_k, cache_dsa_v, cache_dsa_idx_k), *weights)
    return (y_p, y_s) + ab_p + dsa_p + ab_s + dsa_s
```

```python
import functools
from typing import NamedTuple, Optional

import jax
import jax.numpy as jnp
import numpy as np
from jax import lax
from jax.experimental import pallas as pl
from jax.experimental.pallas import tpu as pltpu

CHUNK = 64
HEAD_DIM = 128
H_SB = 4
H_FOX = 4
H_DSA = 8
KV_DSA = 2
H_IDX = 4
D_IDX = 64
TOPK_MAX = 256
ROPE_THETA = 10000.0
EPS = 1e-6

LANES = 128
BF16_ROWS = 16

KEY_PAD = 512
LOG2E = 1.4426950408889634
M_INIT = -1e38
SB_DEAD_LOG2 = -150.0
INT32_MIN = -(2 ** 31)
NEG_INF_CODE = int(np.int32(np.uint32(0xFF800000) ^ np.uint32(0x7FFFFFFF)))

F32 = jnp.float32
BF16 = jnp.bfloat16
NT_DIMS = (((1,), (1,)), ((), ()))


def _round_up(a, b):
    return (a + b - 1) // b * b


def _cparams(semantics, vmem_mib=48):
    return pltpu.CompilerParams(dimension_semantics=semantics,
                                vmem_limit_bytes=vmem_mib * 1024 * 1024)


def _rms(x, g):
    return x * lax.rsqrt(jnp.mean(x * x, axis=-1, keepdims=True) + EPS) * g


def _log_sigmoid(z):
    return jnp.minimum(z, 0.0) - jnp.log1p(jnp.exp(-jnp.abs(z)))


def _split3(x):
    hi = x.astype(BF16)
    r1 = x - hi.astype(F32)
    mid = r1.astype(BF16)
    lo = (r1 - mid.astype(F32)).astype(BF16)
    return hi, mid, lo


class Segment(NamedTuple):
    start: int
    width: int
    rope: Optional[int] = None
    half: int = 0
    f32: bool = False
    bf16_scale: Optional[float] = None
    heads_as_rows: bool = False


PROJ_TILE = 256


def _rotate(x, cos, sin, half):
    if 2 * half == LANES:
        partner = pltpu.roll(x, half, 1)
    else:
        lane = lax.broadcasted_iota(jnp.int32, x.shape, 1)
        partner = jnp.where(lane % (2 * half) < half,
                            pltpu.roll(x, LANES - half, 1), pltpu.roll(x, half, 1))
    return x * cos + partner * sin


def _proj_kernel(x_ref, g_ref, w_ref, *refs, plan, n_tab):
    tabs, outs = refs[:n_tab], list(refs[n_tab:])
    h = _rms(x_ref[...], g_ref[...]).astype(BF16)
    for seg in plan:
        o_f32 = outs.pop(0) if seg.f32 else None
        o_b16 = outs.pop(0) if seg.bf16_scale is not None else None
        tile = min(PROJ_TILE, seg.width)
        for j in range(seg.width // tile):
            acc = jnp.dot(h, w_ref[:, seg.start + j * tile:seg.start + (j + 1) * tile],
                          preferred_element_type=F32)
            if seg.rope is not None:
                cos, sin = tabs[2 * seg.rope][...], tabs[2 * seg.rope + 1][...]
                acc = jnp.concatenate(
                    [_rotate(acc[:, gi * LANES:(gi + 1) * LANES], cos, sin, seg.half)
                     for gi in range(tile // LANES)], axis=1)
            cols = slice(j * tile, (j + 1) * tile)
            if o_f32 is not None and seg.heads_as_rows:
                heads = seg.width // LANES
                for gi in range(tile // LANES):
                    head = j * (tile // LANES) + gi
                    o_f32[pl.ds(head, acc.shape[0], stride=heads), :] = (
                        acc[:, gi * LANES:(gi + 1) * LANES])
            elif o_f32 is not None:
                o_f32[:, cols] = acc
            if o_b16 is not None:
                o_b16[:, cols] = (acc * seg.bf16_scale).astype(BF16)


def norm_project(x, g, w, plan, tables=()):
    n, d = x.shape
    e = w.shape[1]
    tm = min(256, n)
    flat_tabs = [t for pair in tables for t in pair]
    tab_specs = []
    for tab in flat_tabs:
        period = tab.shape[0] // tm
        tab_specs.append(pl.BlockSpec((tm, LANES), lambda i, period=period: (i % period, 0)))
    out_specs, out_shape = [], []
    for seg in plan:
        if seg.f32:
            heads = seg.width // LANES if seg.heads_as_rows else 1
            out_specs.append(pl.BlockSpec((tm * heads, seg.width // heads), lambda i: (i, 0)))
            out_shape.append(jax.ShapeDtypeStruct((n * heads, seg.width // heads), F32))
        if seg.bf16_scale is not None:
            out_specs.append(pl.BlockSpec((tm, seg.width), lambda i: (i, 0)))
            out_shape.append(jax.ShapeDtypeStruct((n, seg.width), BF16))
    return pl.pallas_call(
        functools.partial(_proj_kernel, plan=tuple(plan), n_tab=len(flat_tabs)),
        grid=(n // tm,),
        in_specs=[pl.BlockSpec((tm, d), lambda i: (i, 0)),
                  pl.BlockSpec((1, d), lambda i: (0, 0)),
                  pl.BlockSpec((d, e), lambda i: (0, 0))] + tab_specs,
        out_specs=out_specs,
        out_shape=out_shape,
        compiler_params=_cparams(("parallel",)),
    )(x, g.reshape(1, d), w, *flat_tabs)


def _matmul_res_kernel(*refs, n_in):
    a_refs, w_ref, r_ref, o_ref = refs[:n_in], refs[n_in], refs[n_in + 1], refs[n_in + 2]
    acc = r_ref[...]
    k0 = 0
    for a_ref in a_refs:
        k = a_ref.shape[1]
        acc = acc + jnp.dot(a_ref[...], w_ref[k0:k0 + k, :], preferred_element_type=F32)
        k0 += k
    o_ref[...] = acc


def matmul_res(a_list, w, res):
    n, d = res.shape
    tm = min(512, n)
    return pl.pallas_call(
        functools.partial(_matmul_res_kernel, n_in=len(a_list)),
        grid=(n // tm,),
        in_specs=[pl.BlockSpec((tm, a.shape[1]), lambda i: (i, 0)) for a in a_list]
        + [pl.BlockSpec(w.shape, lambda i: (0, 0)),
           pl.BlockSpec((tm, d), lambda i: (i, 0))],
        out_specs=pl.BlockSpec((tm, d), lambda i: (i, 0)),
        out_shape=jax.ShapeDtypeStruct((n, d), F32),
        compiler_params=_cparams(("parallel",)),
    )(*a_list, w, res)


def _logf_cumsum_kernel(pre_ref, b_ref, logf_ref, hi_ref, mid_ref, lo_ref, carry,
                        *, past, length, cw):
    j = pl.program_id(0)

    @pl.when(j == 0)
    def _():
        carry[...] = jnp.zeros_like(carry)

    pre = pre_ref[...]
    pos = j * cw + lax.broadcasted_iota(jnp.int32, pre.shape, 1)
    x = jnp.where(pos >= past, _log_sigmoid(pre + b_ref[...]), pre)
    x = jnp.where(pos < length, x, 0.0)
    logf_ref[...] = x
    row = lax.broadcasted_iota(jnp.int32, (cw, cw), 0)
    col = lax.broadcasted_iota(jnp.int32, (cw, cw), 1)
    upper = jnp.where(row <= col, 1.0, 0.0).astype(BF16)
    hi, mid, lo = _split3(x)
    cs = (jnp.dot(hi, upper, preferred_element_type=F32)
          + jnp.dot(mid, upper, preferred_element_type=F32)
          + jnp.dot(lo, upper, preferred_element_type=F32))
    cum = cs + carry[:, 0:1]
    carry[...] = jnp.broadcast_to(cum[:, cw - 1:cw], carry.shape)
    hi_ref[...], mid_ref[...], lo_ref[...] = _split3(cum * LOG2E)


def logf_cumsum(pre, b_col, past, length):
    r, lp = pre.shape
    cw = KEY_PAD
    blk = pl.BlockSpec((r, cw), lambda j: (0, j))
    return pl.pallas_call(
        functools.partial(_logf_cumsum_kernel, past=past, length=length, cw=cw),
        grid=(lp // cw,),
        in_specs=[blk, pl.BlockSpec((r, 1), lambda j: (0, 0))],
        out_specs=[blk, blk, blk, blk],
        out_shape=[jax.ShapeDtypeStruct((r, lp), F32)] + [jax.ShapeDtypeStruct((r, lp), BF16)] * 3,
        scratch_shapes=[pltpu.VMEM((r, LANES), F32)],
        compiler_params=_cparams(("arbitrary",)),
    )(pre, b_col)


def _softmax_rows(s_parts, m_old, l_old):
    mx = s_parts[0]
    for sp in s_parts[1:]:
        mx = jnp.maximum(mx, sp)
    m_new = jnp.maximum(m_old, jnp.max(mx, axis=-1, keepdims=True))
    alpha = jnp.exp2(m_old - m_new)
    ps = [jnp.exp2(sp - m_new) for sp in s_parts]
    tot = ps[0]
    for p in ps[1:]:
        tot = tot + p
    return ps, m_new, alpha * l_old + tot, alpha


def _sb_kernel(q_ref, k_ref, v_ref, o_ref, s_scr, b_scr, hi_scr, lo_scr, p_scr, tri, acc, carry,
               *, off, tq, tk, heads):
    qi = pl.program_id(1)
    q0 = off + qi * tq
    kb_last = (q0 + tq - 2) // tk
    ng = tk // LANES
    nchunk = tq // BF16_ROWS

    acc[...] = jnp.zeros_like(acc)
    carry[...] = jnp.zeros_like(carry)
    row = lax.broadcasted_iota(jnp.int32, (tk, tk), 0)
    col = lax.broadcasted_iota(jnp.int32, (tk, tk), 1)
    tri[...] = jnp.where(row > col, 1.0, 0.0).astype(BF16)

    def block(state):
        kb, _ = state
        k0 = pl.multiple_of(kb * tk, tk)
        kpos = k0 + lax.broadcasted_iota(jnp.int32, (1, LANES), 1)
        for h in range(heads):
            sl = slice(h * HEAD_DIM, (h + 1) * HEAD_DIM)
            s_scr[h] = lax.dot_general(q_ref[0, :, sl], k_ref[0, pl.ds(k0, tk), sl], NT_DIMS,
                                       preferred_element_type=F32)
        for h in range(heads):
            for c in range(nchunk):
                rows = slice(c * BF16_ROWS, (c + 1) * BF16_ROWS)
                qpos = q0 + c * BF16_ROWS + lax.broadcasted_iota(jnp.int32, (BF16_ROWS, 1), 0)
                for g in range(ng):
                    ls = slice(g * LANES, (g + 1) * LANES)
                    z = s_scr[h, rows, ls]
                    lsig = jnp.minimum(z, 0.0) - jnp.log2(1.0 + jnp.exp2(-jnp.abs(z)))
                    l1m = jnp.where(kpos + g * LANES < qpos, lsig - z, 0.0)
                    hi = l1m.astype(BF16)
                    hi_scr[h, rows, ls] = hi
                    lo_scr[h, rows, ls] = (l1m - hi.astype(F32)).astype(BF16)
                    s_scr[h, rows, ls] = lsig
        for h in range(heads):
            b_scr[h] = (jnp.dot(hi_scr[h], tri[...], preferred_element_type=F32)
                        + jnp.dot(lo_scr[h], tri[...], preferred_element_type=F32))
        for h in range(heads):
            for c in range(nchunk):
                rows = slice(c * BF16_ROWS, (c + 1) * BF16_ROWS)
                qpos = q0 + c * BF16_ROWS + lax.broadcasted_iota(jnp.int32, (BF16_ROWS, 1), 0)
                run = carry[h, rows, :]
                for g in range(ng):
                    ls = slice(g * LANES, (g + 1) * LANES)
                    a = jnp.exp2(s_scr[h, rows, ls] + b_scr[h, rows, ls] + run)
                    a = jnp.where(kpos + g * LANES < qpos, a, 0.0)
                    p_scr[h, rows, ls] = a.astype(BF16)
                first = (b_scr[h, rows, 0:1] + hi_scr[h, rows, 0:1].astype(F32)
                         + lo_scr[h, rows, 0:1].astype(F32))
                carry[h, rows, :] = run + first
        for h in range(heads):
            sl = slice(h * HEAD_DIM, (h + 1) * HEAD_DIM)
            acc[h] += jnp.dot(p_scr[h], v_ref[0, pl.ds(k0, tk), sl], preferred_element_type=F32)
        return kb - 1, jnp.max(carry[...]) > SB_DEAD_LOG2

    lax.while_loop(lambda st: jnp.logical_and(st[0] >= 0, st[1]), block,
                   (kb_last, jnp.bool_(True)))
    for h in range(heads):
        o_ref[0, :, h * HEAD_DIM:(h + 1) * HEAD_DIM] = acc[h].astype(o_ref.dtype)


def sb_attention(q, k_arr, kc, v_arr, vc, t, length, tq, tk):
    b = q.shape[0]
    lp = k_arr.shape[1]
    w = H_SB * HEAD_DIM
    return pl.pallas_call(
        functools.partial(_sb_kernel, off=length - t, tq=tq, tk=tk, heads=H_SB),
        grid=(b, t // tq),
        in_specs=[pl.BlockSpec((1, tq, w), lambda bi, qi: (bi, qi, 0)),
                  pl.BlockSpec((1, lp, w), lambda bi, qi: (bi, 0, kc)),
                  pl.BlockSpec((1, lp, w), lambda bi, qi: (bi, 0, vc))],
        out_specs=pl.BlockSpec((1, tq, w), lambda bi, qi: (bi, qi, 0)),
        out_shape=jax.ShapeDtypeStruct((b, t, w), BF16),
        scratch_shapes=[pltpu.VMEM((H_SB, tq, tk), F32),
                        pltpu.VMEM((H_SB, tq, tk), F32),
                        pltpu.VMEM((H_SB, tq, tk), BF16),
                        pltpu.VMEM((H_SB, tq, tk), BF16),
                        pltpu.VMEM((H_SB, tq, tk), BF16),
                        pltpu.VMEM((tk, tk), BF16),
                        pltpu.VMEM((H_SB, tq, HEAD_DIM), F32),
                        pltpu.VMEM((H_SB, tq, LANES), F32)],
        compiler_params=_cparams(("parallel", "arbitrary")),
    )(q, k_arr, v_arr)


def _fox_kernel(q_ref, qb_ref, k_ref, kb_ref, v_ref, o_ref, s_scr, p_scr, acc, m_scr, l_scr, a_scr,
                *, off, tq, tk, heads):
    qi = pl.program_id(1)
    q0 = off + qi * tq
    kb_last = (q0 + tq - 1) // tk
    ng = tk // LANES
    nchunk = tq // BF16_ROWS

    acc[...] = jnp.zeros_like(acc)
    m_scr[...] = jnp.full_like(m_scr, M_INIT)
    l_scr[...] = jnp.zeros_like(l_scr)

    def block(kb, _):
        k0 = pl.multiple_of(kb * tk, tk)
        for h in range(heads):
            sl = slice(h * HEAD_DIM, (h + 1) * HEAD_DIM)
            q_aug = jnp.concatenate([q_ref[0, :, sl], qb_ref[0, :, sl]], axis=1)
            k_aug = jnp.concatenate([k_ref[0, pl.ds(k0, tk), sl], kb_ref[0, pl.ds(k0, tk), sl]],
                                    axis=1)
            s_scr[h] = lax.dot_general(q_aug, k_aug, NT_DIMS, preferred_element_type=F32)

        @pl.when(k0 + tk - 1 > q0)
        def _():
            qpos = q0 + lax.broadcasted_iota(jnp.int32, (tq, 1), 0)
            kpos = k0 + lax.broadcasted_iota(jnp.int32, (1, tk), 1)
            for h in range(heads):
                s_scr[h] = jnp.where(kpos <= qpos, s_scr[h], -jnp.inf)

        for h in range(heads):
            for c in range(nchunk):
                rows = slice(c * BF16_ROWS, (c + 1) * BF16_ROWS)
                parts = [s_scr[h, rows, g * LANES:(g + 1) * LANES] for g in range(ng)]
                ps, m_new, l_new, alpha = _softmax_rows(parts, m_scr[h, rows, :], l_scr[h, rows, :])
                m_scr[h, rows, :] = m_new
                l_scr[h, rows, :] = l_new
                a_scr[h, rows, :] = alpha
                for g in range(ng):
                    p_scr[h, rows, g * LANES:(g + 1) * LANES] = ps[g].astype(BF16)
        for h in range(heads):
            sl = slice(h * HEAD_DIM, (h + 1) * HEAD_DIM)
            acc[h] = a_scr[h] * acc[h] + jnp.dot(p_scr[h], v_ref[0, pl.ds(k0, tk), sl],
                                                 preferred_element_type=F32)
        return 0

    lax.fori_loop(0, kb_last + 1, block, 0)
    for h in range(heads):
        l = jnp.sum(l_scr[h], axis=-1, keepdims=True)
        o_ref[0, :, h * HEAD_DIM:(h + 1) * HEAD_DIM] = (acc[h] / l).astype(o_ref.dtype)


def _fox_bias_cols(parts, first, sign):
    b, r, h, n = parts.shape
    place = np.zeros((n + 1, HEAD_DIM), np.float32)
    for j in range(n):
        place[j, first + j] = sign
        place[n, (first + n + j) % (2 * n)] = 1.0
    src = jnp.concatenate([parts, jnp.ones((b, r, h, 1), BF16)], axis=-1)
    out = jnp.einsum('brhn,nc->brhc', src, jnp.asarray(place, BF16),
                     preferred_element_type=F32)
    return out.astype(BF16).reshape(b, r, h * HEAD_DIM)


def fox_attention(q, q_bias, k_arr, kc, k_bias, v_arr, vc, t, length, tq, tk):
    b = q.shape[0]
    lp = k_arr.shape[1]
    w = H_FOX * HEAD_DIM
    return pl.pallas_call(
        functools.partial(_fox_kernel, off=length - t, tq=tq, tk=tk, heads=H_FOX),
        grid=(b, t // tq),
        in_specs=[pl.BlockSpec((1, tq, w), lambda bi, qi: (bi, qi, 0)),
                  pl.BlockSpec((1, tq, w), lambda bi, qi: (bi, qi, 0)),
                  pl.BlockSpec((1, lp, w), lambda bi, qi: (bi, 0, kc)),
                  pl.BlockSpec((1, lp, w), lambda bi, qi: (bi, 0, 0)),
                  pl.BlockSpec((1, lp, w), lambda bi, qi: (bi, 0, vc))],
        out_specs=pl.BlockSpec((1, tq, w), lambda bi, qi: (bi, qi, 0)),
        out_shape=jax.ShapeDtypeStruct((b, t, w), BF16),
        scratch_shapes=[pltpu.VMEM((H_FOX, tq, tk), F32),
                        pltpu.VMEM((H_FOX, tq, tk), BF16),
                        pltpu.VMEM((H_FOX, tq, HEAD_DIM), F32),
                        pltpu.VMEM((H_FOX, tq, LANES), F32),
                        pltpu.VMEM((H_FOX, tq, LANES), F32),
                        pltpu.VMEM((H_FOX, tq, LANES), F32)],
        compiler_params=_cparams(("parallel", "arbitrary")),
    )(q, q_bias, k_arr, k_bias, v_arr)


def _moe_kernel(x_ref, g_ref, wr_ref, wg_ref, wu_ref, wd_ref, gf_ref, o_ref, hb, comb,
                *, n_groups, per_group, final_norm):
    e = pl.program_id(1)
    n_exp = n_groups * per_group
    lane = lax.broadcasted_iota(jnp.int32, comb.shape, 1)

    @pl.when(e == 0)
    def _():
        x = x_ref[...]
        h = _rms(x, g_ref[...])
        hb[...] = h.astype(BF16)
        logits = jnp.dot(h, wr_ref[...], precision=lax.Precision.HIGHEST,
                         preferred_element_type=F32)
        gl = jnp.where(lane < n_groups, logits, -jnp.inf)
        gmax = jnp.max(gl, axis=-1, keepdims=True)
        gsel = jnp.min(jnp.where(gl == gmax, lane, LANES), axis=-1, keepdims=True)
        p_group = 1.0 / jnp.sum(jnp.where(lane < n_groups, jnp.exp(gl - gmax), 0.0),
                                axis=-1, keepdims=True)
        first = n_groups + gsel * per_group
        el = jnp.where((lane >= first) & (lane < first + per_group), logits, -jnp.inf)
        v1 = jnp.max(el, axis=-1, keepdims=True)
        i1 = jnp.min(jnp.where(el == v1, lane, LANES), axis=-1, keepdims=True)
        el2 = jnp.where(lane == i1, -jnp.inf, el)
        v2 = jnp.max(el2, axis=-1, keepdims=True)
        i2 = jnp.min(jnp.where(el2 == v2, lane, LANES), axis=-1, keepdims=True)
        ratio = jnp.exp(v2 - v1)
        gate1 = p_group / (1.0 + ratio)
        comb[...] = jnp.where(lane == i1, gate1, jnp.where(lane == i2, gate1 * ratio, 0.0))
        o_ref[...] = x

    c = jnp.sum(jnp.where(lane == e + n_groups, comb[...], 0.0), axis=-1, keepdims=True)
    hv = hb[...]
    gate = jnp.dot(hv, wg_ref[0], preferred_element_type=F32)
    up = jnp.dot(hv, wu_ref[0], preferred_element_type=F32)
    act = gate * (1.0 / (1.0 + jnp.exp(-gate))) * up * c
    o_ref[...] += jnp.dot(act.astype(BF16), wd_ref[0], preferred_element_type=F32)

    if final_norm:
        @pl.when(e == n_exp - 1)
        def _():
            o_ref[...] = _rms(o_ref[...], gf_ref[...])


def moe_block(x, g, w_group, w_router, w_gate, w_up, w_down, g_final=None):
    n, d = x.shape
    n_groups, per_group = w_router.shape[1], w_router.shape[2]
    n_exp, _, f = w_gate.shape
    w_route = jnp.concatenate(
        [w_group, w_router.reshape(d, n_exp),
         jnp.zeros((d, LANES - n_groups - n_exp), F32)], axis=1)
    final_norm = g_final is not None
    gf = (g_final if final_norm else g).reshape(1, d)
    tm = min(1024, n)
    return pl.pallas_call(
        functools.partial(_moe_kernel, n_groups=n_groups, per_group=per_group,
                          final_norm=final_norm),
        grid=(n // tm, n_exp),
        in_specs=[pl.BlockSpec((tm, d), lambda i, e: (i, 0)),
                  pl.BlockSpec((1, d), lambda i, e: (0, 0)),
                  pl.BlockSpec((d, LANES), lambda i, e: (0, 0)),
                  pl.BlockSpec((1, d, f), lambda i, e: (e, 0, 0)),
                  pl.BlockSpec((1, d, f), lambda i, e: (e, 0, 0)),
                  pl.BlockSpec((1, f, d), lambda i, e: (e, 0, 0)),
                  pl.BlockSpec((1, d), lambda i, e: (0, 0))],
        out_specs=pl.BlockSpec((tm, d), lambda i, e: (i, 0)),
        out_shape=jax.ShapeDtypeStruct((n, d), F32),
        scratch_shapes=[pltpu.VMEM((tm, d), BF16), pltpu.VMEM((tm, LANES), F32)],
        compiler_params=_cparams(("parallel", "arbitrary")),
    )(x, g.reshape(1, d), w_route, w_gate.astype(BF16), w_up.astype(BF16),
      w_down.astype(BF16), gf)


def _rope_tables(pos, head_dim, rows, live_lanes=LANES):
    half = head_dim // 2
    inv_freq = ROPE_THETA ** (-jnp.arange(half, dtype=F32) / half)
    ang = pos.astype(F32)[:, None] * inv_freq[None, :]
    cos, sin = jnp.cos(ang), jnp.sin(ang)
    reps = LANES // head_dim
    cos = jnp.tile(jnp.concatenate([cos, cos], axis=1), (1, reps))
    sin = jnp.tile(jnp.concatenate([-sin, sin], axis=1), (1, reps))
    live = jnp.arange(LANES) < live_lanes
    cos = jnp.where(live, cos, 1.0)
    sin = jnp.where(live, sin, 0.0)
    reps_rows = max(rows // pos.shape[0], 1)
    return jnp.tile(cos, (reps_rows, 1)), jnp.tile(sin, (reps_rows, 1))


def _code_to_float(code):
    return pltpu.bitcast(code ^ ((code >> 31) & 0x7FFFFFFF), F32)


def _dsa_reach(q_first, tq, length):
    return jnp.minimum(((q_first + tq - 1) // CHUNK + 1) * CHUNK, length)


def _dsa_select(qi_ref, w_ref, ki_ref, bias, sc, *, off, length, tq, ksel, cw):
    qb = pl.program_id(1)
    q0 = off + qb * tq
    qchunk = (q0 + lax.broadcasted_iota(jnp.int32, (tq, 1), 0)) // CHUNK
    nck = (_dsa_reach(q0, tq, length) + cw - 1) // cw
    w = w_ref[0] * (H_IDX ** -0.5)
    ng = cw // LANES
    kf = float(ksel)

    def fill(c, _):
        k0 = pl.multiple_of(c * cw, cw)
        kk = ki_ref[0, pl.ds(k0, cw), :]
        score = jnp.zeros((tq, cw), F32)
        for h in range(H_IDX):
            lg = lax.dot_general(qi_ref[0, :, h * D_IDX:(h + 1) * D_IDX], kk, NT_DIMS,
                                 preferred_element_type=F32) * (D_IDX ** -0.5)
            score = score + jnp.maximum(lg, 0.0) * w[:, h:h + 1]
        kpos = k0 + lax.broadcasted_iota(jnp.int32, (1, cw), 1)
        admissible = (kpos // CHUNK <= qchunk) & (kpos < length)
        sc[c] = jnp.where(admissible, score + 0.0, -jnp.inf)
        return 0

    lax.fori_loop(0, nck, fill, 0)

    def count(pred):
        def body(c, tot):
            for g in range(ng):
                tot = tot + jnp.where(pred(sc[c, :, g * LANES:(g + 1) * LANES]), 1.0, 0.0)
            return tot
        tot = lax.fori_loop(0, nck, body, jnp.zeros((tq, LANES), F32))
        return jnp.sum(tot, axis=-1, keepdims=True)

    code0 = jnp.where(count(lambda x: x >= 0.0) >= kf, 0, INT32_MIN).astype(jnp.int32)

    def bit_step(i, code):
        cand = code + lax.shift_left(jnp.int32(1), 30 - i)
        cf = _code_to_float(cand)
        return jnp.where(count(lambda x: x >= cf) >= kf, cand, code)

    code = lax.fori_loop(0, 31, bit_step, code0)
    thr = _code_to_float(jnp.maximum(code, NEG_INF_CODE + 1))
    n_ge = count(lambda x: x >= thr)
    n_gt = count(lambda x: x > thr)
    tied = jnp.max(n_ge) > kf

    @pl.when(jnp.logical_not(tied))
    def _():
        def emit(c, _):
            bias[c] = jnp.where(sc[c] >= thr, 0.0, -jnp.inf).astype(bias.dtype)
            return 0
        lax.fori_loop(0, nck, emit, 0)

    @pl.when(tied)
    def _():
        need = kf - n_gt
        row = lax.broadcasted_iota(jnp.int32, (LANES, LANES), 0)
        col = lax.broadcasted_iota(jnp.int32, (LANES, LANES), 1)
        before = jnp.where(row < col, 1.0, 0.0).astype(BF16)

        def emit(c, run):
            for g in range(ng):
                sl = slice(g * LANES, (g + 1) * LANES)
                x = sc[c, :, sl]
                eq = jnp.where(x == thr, 1.0, 0.0)
                rank = run + jnp.dot(eq.astype(BF16), before, preferred_element_type=F32)
                sel = (x > thr) | ((x == thr) & (rank < need))
                bias[c, :, sl] = jnp.where(sel, 0.0, -jnp.inf).astype(bias.dtype)
                run = run + jnp.sum(eq, axis=-1, keepdims=True)
            return run
        lax.fori_loop(0, nck, emit, jnp.zeros((tq, 1), F32))


def _dsa_attend(q_ref, k_ref, v_ref, bias, o_ref, qs, s_scr, p_scr, acc, m_scr, l_scr, a_scr,
                *, off, length, tq, tk):
    qb = pl.program_id(1)
    rep = H_DSA // KV_DSA
    nkb = (_dsa_reach(off + qb * tq, tq, length) + tk - 1) // tk
    ng = tk // LANES
    nchunk = tq // BF16_ROWS

    for g in range(KV_DSA):
        for r in range(rep):
            hd = g * rep + r
            qs[g, r * tq:(r + 1) * tq, :] = q_ref[0, :, hd * HEAD_DIM:(hd + 1) * HEAD_DIM]
    acc[...] = jnp.zeros_like(acc)
    m_scr[...] = jnp.full_like(m_scr, M_INIT)
    l_scr[...] = jnp.zeros_like(l_scr)

    def scores(kb, dst):
        k0 = pl.multiple_of(kb * tk, tk)
        for g in range(KV_DSA):
            sl = slice(g * HEAD_DIM, (g + 1) * HEAD_DIM)
            dst[g] = lax.dot_general(qs[g], k_ref[0, pl.ds(k0, tk), sl], NT_DIMS,
                                     preferred_element_type=F32)

    def consume(kb, src):
        k0 = pl.multiple_of(kb * tk, tk)
        for c in range(nchunk):
            bias_c = [bias[kb, c * BF16_ROWS:(c + 1) * BF16_ROWS,
                           gl * LANES:(gl + 1) * LANES].astype(F32) for gl in range(ng)]
            for g in range(KV_DSA):
                for r in range(rep):
                    rows = slice(r * tq + c * BF16_ROWS, r * tq + (c + 1) * BF16_ROWS)
                    parts = [src[g, rows, gl * LANES:(gl + 1) * LANES] + bias_c[gl]
                             for gl in range(ng)]
                    ps, m_new, l_new, alpha = _softmax_rows(parts, m_scr[g, rows, :],
                                                            l_scr[g, rows, :])
                    m_scr[g, rows, :] = m_new
                    l_scr[g, rows, :] = l_new
                    a_scr[g, rows, :] = alpha
                    for gl in range(ng):
                        p_scr[g, rows, gl * LANES:(gl + 1) * LANES] = ps[gl].astype(BF16)
        for g in range(KV_DSA):
            sl = slice(g * HEAD_DIM, (g + 1) * HEAD_DIM)
            acc[g] = a_scr[g] * acc[g] + jnp.dot(p_scr[g], v_ref[0, pl.ds(k0, tk), sl],
                                                 preferred_element_type=F32)

    def block(kb, _):
        scores(kb, s_scr)
        consume(kb, s_scr)
        return 0

    lax.fori_loop(0, nkb, block, 0)
    for g in range(KV_DSA):
        o = acc[g] / jnp.sum(l_scr[g], axis=-1, keepdims=True)
        for r in range(rep):
            hd = g * rep + r
            o_ref[0, :, hd * HEAD_DIM:(hd + 1) * HEAD_DIM] = (
                o[r * tq:(r + 1) * tq]).astype(o_ref.dtype)


def _dsa_kernel(qi_ref, w_ref, ki_ref, q_ref, k_ref, v_ref, o_ref, sc, bias, qs, s_scr, p_scr, acc,
                m_scr, l_scr, a_scr, *, off, length, tq, tk, ksel):
    _dsa_select(qi_ref, w_ref, ki_ref, bias, sc, off=off, length=length, tq=tq, ksel=ksel, cw=tk)
    _dsa_attend(q_ref, k_ref, v_ref, bias, o_ref, qs, s_scr, p_scr, acc, m_scr, l_scr, a_scr,
                off=off, length=length, tq=tq, tk=tk)


def dsa_sparse_attention(qi, w_idx, ki, q, k_arr, v_arr, t, length, tq, tk):
    b, lp, _ = ki.shape
    wq = H_DSA * HEAD_DIM
    wk = KV_DSA * HEAD_DIM
    rep = H_DSA // KV_DSA
    nch = lp // tk
    per_q = lambda w: pl.BlockSpec((1, tq, w), lambda bi, qb: (bi, qb, 0))
    per_b = lambda w: pl.BlockSpec((1, lp, w), lambda bi, qb: (bi, 0, 0))
    return pl.pallas_call(
        functools.partial(_dsa_kernel, off=length - t, length=length, tq=tq, tk=tk,
                          ksel=min(TOPK_MAX, length // 4)),
        grid=(b, t // tq),
        in_specs=[per_q(H_IDX * D_IDX), per_q(H_IDX), per_b(D_IDX),
                  per_q(wq), per_b(wk), per_b(wk)],
        out_specs=per_q(wq),
        out_shape=jax.ShapeDtypeStruct((b, t, wq), BF16),
        scratch_shapes=[pltpu.VMEM((nch, tq, tk), F32),
                        pltpu.VMEM((nch, tq, tk), BF16),
                        pltpu.VMEM((KV_DSA, rep * tq, HEAD_DIM), BF16),
                        pltpu.VMEM((KV_DSA, rep * tq, tk), F32),
                        pltpu.VMEM((KV_DSA, rep * tq, tk), BF16),
                        pltpu.VMEM((KV_DSA, rep * tq, HEAD_DIM), F32),
                        pltpu.VMEM((KV_DSA, rep * tq, LANES), F32),
                        pltpu.VMEM((KV_DSA, rep * tq, LANES), F32),
                        pltpu.VMEM((KV_DSA, rep * tq, LANES), F32)],
        compiler_params=_cparams(("parallel", "arbitrary")),
    )(qi, w_idx, ki, q, k_arr, v_arr)


def _with_past(past, new, lp):
    b = new.shape[0]
    parts = [] if past is None else [past.reshape(b, past.shape[1], -1).astype(BF16)]
    parts.append(new.astype(BF16))
    rows = sum(p.shape[1] for p in parts)
    if rows < lp:
        parts.append(jnp.zeros((b, lp - rows, new.shape[2]), BF16))
    return parts[0] if len(parts) == 1 else jnp.concatenate(parts, axis=1)


def _mixer_ab(x2, b, t, past, g, w_in, b_forget, w_out):
    wsb = H_SB * HEAD_DIM
    wfx = H_FOX * HEAD_DIM
    qscale = LOG2E * HEAD_DIM ** -0.5
    c_fx = 3 * wsb
    c_gate = c_fx + 3 * wfx
    plan = [Segment(0, wsb, bf16_scale=qscale),
            Segment(wsb, wsb, f32=True, bf16_scale=1.0, heads_as_rows=True),
            Segment(2 * wsb, wsb, f32=True, bf16_scale=1.0, heads_as_rows=True),
            Segment(c_fx, wfx, bf16_scale=qscale),
            Segment(c_fx + wfx, wfx, f32=True, bf16_scale=1.0, heads_as_rows=True),
            Segment(c_fx + 2 * wfx, wfx, f32=True, bf16_scale=1.0, heads_as_rows=True),
            Segment(c_gate, LANES, f32=True)]
    w_pad = jnp.pad(w_in, ((0, 0), (0, c_gate + LANES - w_in.shape[1]))).astype(BF16)
    (q_sb, k_sb, k_sb_b, v_sb, v_sb_b, q_fx, k_fx, k_fx_b, v_fx, v_fx_b, gate) = norm_project(
        x2, g, w_pad, plan)
    rows = tuple(a.reshape(b, t, -1, HEAD_DIM) for a in (k_sb, v_sb, k_fx, v_fx))
    f_logit = gate.reshape(b, t, LANES)[:, :, :H_FOX]

    p = 0 if past is None else past[0].shape[1]
    length = p + t
    lp = _round_up(length, KEY_PAD)
    new_t = jnp.swapaxes(f_logit, 1, 2)
    parts = [new_t] if past is None else [jnp.swapaxes(past[4].astype(F32), 1, 2), new_t]
    if length < lp:
        parts.append(jnp.zeros((b, H_FOX, lp - length), F32))
    pre = (parts[0] if len(parts) == 1 else jnp.concatenate(parts, axis=2)).reshape(b * H_FOX, lp)
    b_col = jnp.tile(b_forget.astype(F32), b).reshape(b * H_FOX, 1)
    logf_t, *cum_parts = logf_cumsum(pre, b_col, p, length)
    logf = jnp.swapaxes(logf_t.reshape(b, H_FOX, lp)[:, :, p:length], 1, 2)

    f_parts = jnp.stack([jnp.swapaxes(c.reshape(b, H_FOX, lp), 1, 2) for c in cum_parts],
                        axis=-1)
    q_bias = _fox_bias_cols(f_parts[:, p:length], 0, 1.0)
    k_bias = _fox_bias_cols(f_parts, f_parts.shape[-1], -1.0)

    as3 = lambda a: a.reshape(b, t, -1)
    ks, vs, kf, vf = (_with_past(None if past is None else past[i], as3(a), lp)
                      for i, a in enumerate((k_sb_b, v_sb_b, k_fx_b, v_fx_b)))
    tq = min(256, t)
    o_sb = sb_attention(as3(q_sb), ks, 0, vs, 0, t, length, tq, 256)
    o_fx = fox_attention(as3(q_fx), q_bias, kf, 0, k_bias, vf, 0, t, length, tq, 512)
    x_new = matmul_res([o_sb.reshape(b * t, wsb), o_fx.reshape(b * t, wfx)],
                       w_out.astype(BF16), x2)
    return x_new, rows + (logf,)


def _mixer_dsa(x2, b, t, past, g, w_in, w_out):
    n = b * t
    wq = H_DSA * HEAD_DIM
    wk = KV_DSA * HEAD_DIM
    wi = H_IDX * D_IDX
    p = 0 if past is None else past[0].shape[1]
    length = p + t
    lp = _round_up(length, KEY_PAD)
    pos = p + jnp.arange(t)
    tab_rows = max(t, min(256, n))
    tables = [_rope_tables(pos, HEAD_DIM, tab_rows),
              _rope_tables(pos, D_IDX, tab_rows),
              _rope_tables(pos, D_IDX, tab_rows, live_lanes=D_IDX)]
    c_k, c_v, c_qi, c_ki = wq, wq + wk, wq + 2 * wk, wq + 2 * wk + wi
    plan = [Segment(0, wq, rope=0, half=HEAD_DIM // 2, bf16_scale=LOG2E * HEAD_DIM ** -0.5),
            Segment(c_k, wk, rope=0, half=HEAD_DIM // 2, f32=True, bf16_scale=1.0,
                    heads_as_rows=True),
            Segment(c_v, wk, f32=True, bf16_scale=1.0, heads_as_rows=True),
            Segment(c_qi, wi, rope=1, half=D_IDX // 2, bf16_scale=1.0),
            Segment(c_ki, LANES, rope=2, half=D_IDX // 2, f32=True, bf16_scale=1.0)]
    w_pad = jnp.pad(w_in, ((0, 0), (0, c_ki + LANES - w_in.shape[1]))).astype(BF16)
    q_b, k_f, k_b, v_f, v_b, qi_b, kw_f, kw_b = norm_project(x2, g, w_pad, plan, tables)
    as3 = lambda a: a.reshape(b, t, -1)
    k_rows = k_f.reshape(b, t, KV_DSA, HEAD_DIM)
    v_rows = v_f.reshape(b, t, KV_DSA, HEAD_DIM)
    ki_rows = as3(kw_f)[:, :, :D_IDX]
    w_idx = as3(kw_f)[:, :, D_IDX:D_IDX + H_IDX]

    tk = KEY_PAD
    ki_all = _with_past(None if past is None else past[2], as3(kw_b)[:, :, :D_IDX], lp)
    k_all = _with_past(None if past is None else past[0], as3(k_b), lp)
    v_all = _with_past(None if past is None else past[1], as3(v_b), lp)
    o = dsa_sparse_attention(as3(qi_b), w_idx, ki_all, as3(q_b), k_all, v_all, t, length,
                             min(128, t), tk)
    x_new = matmul_res([o.reshape(n, wq)], w_out.astype(BF16), x2)
    return x_new, (k_rows, v_rows, ki_rows)


def _trunk(x, past_ab, past_dsa, norm_mix, norm_ffn, norm_final, w_in_ab, b_forget, w_out_ab,
           w_in_dsa, w_out_dsa, moe_w_group, moe_w_router, moe_w_gate, moe_w_up, moe_w_down):
    b, t, d = x.shape
    x2 = x.reshape(b * t, d)
    x2, rows_ab = _mixer_ab(x2, b, t, past_ab, norm_mix[0], w_in_ab, b_forget, w_out_ab)
    x2 = moe_block(x2, norm_ffn[0], moe_w_group[0], moe_w_router[0], moe_w_gate[0],
                   moe_w_up[0], moe_w_down[0])
    x2, rows_dsa = _mixer_dsa(x2, b, t, past_dsa, norm_mix[1], w_in_dsa, w_out_dsa)
    y = moe_block(x2, norm_ffn[1], moe_w_group[1], moe_w_router[1], moe_w_gate[1],
                  moe_w_up[1], moe_w_down[1], g_final=norm_final)
    return y.reshape(b, t, d), rows_ab, rows_dsa


def kernel(x_prompt, x_sample, cache_sb_k, cache_sb_v, cache_fox_k, cache_fox_v, cache_fox_logf,
           cache_dsa_k, cache_dsa_v, cache_dsa_idx_k, norm_mix, norm_ffn, norm_final,
           w_in_ab, b_forget, w_out_ab, w_in_dsa, w_out_dsa,
           moe_w_group, moe_w_router, moe_w_gate, moe_w_up, moe_w_down):
    weights = (norm_mix, norm_ffn, norm_final, w_in_ab, b_forget, w_out_ab, w_in_dsa, w_out_dsa,
               moe_w_group, moe_w_router, moe_w_gate, moe_w_up, moe_w_down)
    y_p, ab_p, dsa_p = _trunk(x_prompt, None, None, *weights)
    y_s, ab_s, dsa_s = _trunk(
        x_sample, (cache_sb_k, cache_sb_v, cache_fox_k, cache_fox_v, cache_fox_logf),
        (cache_dsa_k, cache_dsa_v, cache_dsa_idx_k), *weights)
    return (y_p, y_s) + ab_p + dsa_p + ab_s + dsa_s
```

```python
import functools
from typing import NamedTuple, Optional

import jax
import jax.numpy as jnp
import numpy as np
from jax import lax
from jax.experimental import pallas as pl
from jax.experimental.pallas import tpu as pltpu

CHUNK = 64
HEAD_DIM = 128
H_SB = 4
H_FOX = 4
H_DSA = 8
KV_DSA = 2
H_IDX = 4
D_IDX = 64
TOPK_MAX = 256
ROPE_THETA = 10000.0
EPS = 1e-6

LANES = 128
BF16_ROWS = 16

KEY_PAD = 512
LOG2E = 1.4426950408889634
M_INIT = -1e38
SB_DEAD_LOG2 = -150.0
INT32_MIN = -(2 ** 31)
SELECT_ROWS = 128
NEG_INF_CODE = int(np.int32(np.uint32(0xFF800000) ^ np.uint32(0x7FFFFFFF)))

F32 = jnp.float32
BF16 = jnp.bfloat16
NT_DIMS = (((1,), (1,)), ((), ()))


def _round_up(a, b):
    return (a + b - 1) // b * b


def _cparams(semantics, vmem_mib=48):
    return pltpu.CompilerParams(dimension_semantics=semantics,
                                vmem_limit_bytes=vmem_mib * 1024 * 1024)


def _rms(x, g):
    return x * lax.rsqrt(jnp.mean(x * x, axis=-1, keepdims=True) + EPS) * g


def _log_sigmoid(z):
    return jnp.minimum(z, 0.0) - jnp.log1p(jnp.exp(-jnp.abs(z)))


def _split3(x):
    hi = x.astype(BF16)
    r1 = x - hi.astype(F32)
    mid = r1.astype(BF16)
    lo = (r1 - mid.astype(F32)).astype(BF16)
    return hi, mid, lo


class Segment(NamedTuple):
    start: int
    width: int
    rope: Optional[int] = None
    half: int = 0
    f32: bool = False
    bf16_scale: Optional[float] = None
    heads_as_rows: bool = False


PROJ_TILE = 256


def _rotate(x, cos, sin, half):
    if 2 * half == LANES:
        partner = pltpu.roll(x, half, 1)
    else:
        lane = lax.broadcasted_iota(jnp.int32, x.shape, 1)
        partner = jnp.where(lane % (2 * half) < half,
                            pltpu.roll(x, LANES - half, 1), pltpu.roll(x, half, 1))
    return x * cos + partner * sin


def _proj_kernel(x_ref, g_ref, w_ref, *refs, plan, n_tab):
    tabs, outs = refs[:n_tab], list(refs[n_tab:])
    h = _rms(x_ref[...], g_ref[...]).astype(BF16)
    for seg in plan:
        o_f32 = outs.pop(0) if seg.f32 else None
        o_b16 = outs.pop(0) if seg.bf16_scale is not None else None
        tile = min(PROJ_TILE, seg.width)
        for j in range(seg.width // tile):
            acc = jnp.dot(h, w_ref[:, seg.start + j * tile:seg.start + (j + 1) * tile],
                          preferred_element_type=F32)
            if seg.rope is not None:
                cos, sin = tabs[2 * seg.rope][...], tabs[2 * seg.rope + 1][...]
                acc = jnp.concatenate(
                    [_rotate(acc[:, gi * LANES:(gi + 1) * LANES], cos, sin, seg.half)
                     for gi in range(tile // LANES)], axis=1)
            cols = slice(j * tile, (j + 1) * tile)
            if o_f32 is not None and seg.heads_as_rows:
                heads = seg.width // LANES
                for gi in range(tile // LANES):
                    head = j * (tile // LANES) + gi
                    o_f32[pl.ds(head, acc.shape[0], stride=heads), :] = (
                        acc[:, gi * LANES:(gi + 1) * LANES])
            elif o_f32 is not None:
                o_f32[:, cols] = acc
            if o_b16 is not None:
                o_b16[:, cols] = (acc * seg.bf16_scale).astype(BF16)


def norm_project(x, g, w, plan, tables=()):
    n, d = x.shape
    e = w.shape[1]
    tm = min(256, n)
    flat_tabs = [t for pair in tables for t in pair]
    tab_specs = []
    for tab in flat_tabs:
        period = tab.shape[0] // tm
        tab_specs.append(pl.BlockSpec((tm, LANES), lambda i, period=period: (i % period, 0)))
    out_specs, out_shape = [], []
    for seg in plan:
        if seg.f32:
            heads = seg.width // LANES if seg.heads_as_rows else 1
            out_specs.append(pl.BlockSpec((tm * heads, seg.width // heads), lambda i: (i, 0)))
            out_shape.append(jax.ShapeDtypeStruct((n * heads, seg.width // heads), F32))
        if seg.bf16_scale is not None:
            out_specs.append(pl.BlockSpec((tm, seg.width), lambda i: (i, 0)))
            out_shape.append(jax.ShapeDtypeStruct((n, seg.width), BF16))
    return pl.pallas_call(
        functools.partial(_proj_kernel, plan=tuple(plan), n_tab=len(flat_tabs)),
        grid=(n // tm,),
        in_specs=[pl.BlockSpec((tm, d), lambda i: (i, 0)),
                  pl.BlockSpec((1, d), lambda i: (0, 0)),
                  pl.BlockSpec((d, e), lambda i: (0, 0))] + tab_specs,
        out_specs=out_specs,
        out_shape=out_shape,
        compiler_params=_cparams(("parallel",)),
    )(x, g.reshape(1, d), w, *flat_tabs)


def _logf_cumsum_kernel(pre_ref, b_ref, logf_ref, hi_ref, mid_ref, lo_ref, carry,
                        *, past, length, cw):
    j = pl.program_id(0)

    @pl.when(j == 0)
    def _():
        carry[...] = jnp.zeros_like(carry)

    pre = pre_ref[...]
    pos = j * cw + lax.broadcasted_iota(jnp.int32, pre.shape, 1)
    x = jnp.where(pos >= past, _log_sigmoid(pre + b_ref[...]), pre)
    x = jnp.where(pos < length, x, 0.0)
    logf_ref[...] = x
    row = lax.broadcasted_iota(jnp.int32, (cw, cw), 0)
    col = lax.broadcasted_iota(jnp.int32, (cw, cw), 1)
    upper = jnp.where(row <= col, 1.0, 0.0).astype(BF16)
    hi, mid, lo = _split3(x)
    cs = (jnp.dot(hi, upper, preferred_element_type=F32)
          + jnp.dot(mid, upper, preferred_element_type=F32)
          + jnp.dot(lo, upper, preferred_element_type=F32))
    cum = cs + carry[:, 0:1]
    carry[...] = jnp.broadcast_to(cum[:, cw - 1:cw], carry.shape)
    hi_ref[...], mid_ref[...], lo_ref[...] = _split3(cum * LOG2E)


def logf_cumsum(pre, b_col, past, length):
    r, lp = pre.shape
    cw = KEY_PAD
    blk = pl.BlockSpec((r, cw), lambda j: (0, j))
    return pl.pallas_call(
        functools.partial(_logf_cumsum_kernel, past=past, length=length, cw=cw),
        grid=(lp // cw,),
        in_specs=[blk, pl.BlockSpec((r, 1), lambda j: (0, 0))],
        out_specs=[blk, blk, blk, blk],
        out_shape=[jax.ShapeDtypeStruct((r, lp), F32)] + [jax.ShapeDtypeStruct((r, lp), BF16)] * 3,
        scratch_shapes=[pltpu.VMEM((r, LANES), F32)],
        compiler_params=_cparams(("arbitrary",)),
    )(pre, b_col)


def _softmax_rows(s_parts, m_old, l_old):
    mx = s_parts[0]
    for sp in s_parts[1:]:
        mx = jnp.maximum(mx, sp)
    m_new = jnp.maximum(m_old, jnp.max(mx, axis=-1, keepdims=True))
    alpha = jnp.exp2(m_old - m_new)
    ps = [jnp.exp2(sp - m_new) for sp in s_parts]
    tot = ps[0]
    for p in ps[1:]:
        tot = tot + p
    return ps, m_new, alpha * l_old + tot, alpha


def _sb_kernel(q_ref, k_ref, v_ref, o_ref, s_scr, b_scr, hi_scr, lo_scr, p_scr, tri, acc, carry,
               *, off, tq, tk, heads):
    qi = pl.program_id(1)
    q0 = off + qi * tq
    kb_last = (q0 + tq - 2) // tk
    ng = tk // LANES
    nchunk = tq // BF16_ROWS

    acc[...] = jnp.zeros_like(acc)
    carry[...] = jnp.zeros_like(carry)
    row = lax.broadcasted_iota(jnp.int32, (tk, tk), 0)
    col = lax.broadcasted_iota(jnp.int32, (tk, tk), 1)
    tri[...] = jnp.where(row > col, 1.0, 0.0).astype(BF16)

    def block(state):
        kb, _ = state
        k0 = pl.multiple_of(kb * tk, tk)
        kpos = k0 + lax.broadcasted_iota(jnp.int32, (1, LANES), 1)
        for h in range(heads):
            sl = slice(h * HEAD_DIM, (h + 1) * HEAD_DIM)
            s_scr[h] = lax.dot_general(q_ref[0, :, sl], k_ref[0, pl.ds(k0, tk), sl], NT_DIMS,
                                       preferred_element_type=F32)
        for h in range(heads):
            for c in range(nchunk):
                rows = slice(c * BF16_ROWS, (c + 1) * BF16_ROWS)
                qpos = q0 + c * BF16_ROWS + lax.broadcasted_iota(jnp.int32, (BF16_ROWS, 1), 0)
                for g in range(ng):
                    ls = slice(g * LANES, (g + 1) * LANES)
                    z = s_scr[h, rows, ls]
                    lsig = jnp.minimum(z, 0.0) - jnp.log2(1.0 + jnp.exp2(-jnp.abs(z)))
                    l1m = jnp.where(kpos + g * LANES < qpos, lsig - z, 0.0)
                    hi = l1m.astype(BF16)
                    hi_scr[h, rows, ls] = hi
                    lo_scr[h, rows, ls] = (l1m - hi.astype(F32)).astype(BF16)
                    s_scr[h, rows, ls] = lsig
        for h in range(heads):
            b_scr[h] = (jnp.dot(hi_scr[h], tri[...], preferred_element_type=F32)
                        + jnp.dot(lo_scr[h], tri[...], preferred_element_type=F32))
        for h in range(heads):
            for c in range(nchunk):
                rows = slice(c * BF16_ROWS, (c + 1) * BF16_ROWS)
                qpos = q0 + c * BF16_ROWS + lax.broadcasted_iota(jnp.int32, (BF16_ROWS, 1), 0)
                run = carry[h, rows, :]
                for g in range(ng):
                    ls = slice(g * LANES, (g + 1) * LANES)
                    a = jnp.exp2(s_scr[h, rows, ls] + b_scr[h, rows, ls] + run)
                    a = jnp.where(kpos + g * LANES < qpos, a, 0.0)
                    p_scr[h, rows, ls] = a.astype(BF16)
                first = (b_scr[h, rows, 0:1] + hi_scr[h, rows, 0:1].astype(F32)
                         + lo_scr[h, rows, 0:1].astype(F32))
                carry[h, rows, :] = run + first
        for h in range(heads):
            sl = slice(h * HEAD_DIM, (h + 1) * HEAD_DIM)
            acc[h] += jnp.dot(p_scr[h], v_ref[0, pl.ds(k0, tk), sl], preferred_element_type=F32)
        return kb - 1, jnp.max(carry[...]) > SB_DEAD_LOG2

    lax.while_loop(lambda st: jnp.logical_and(st[0] >= 0, st[1]), block,
                   (kb_last, jnp.bool_(True)))
    for h in range(heads):
        o_ref[0, :, h * HEAD_DIM:(h + 1) * HEAD_DIM] = acc[h].astype(o_ref.dtype)


def sb_attention(q, k_arr, kc, v_arr, vc, t, length, tq, tk):
    b = q.shape[0]
    lp = k_arr.shape[1]
    w = H_SB * HEAD_DIM
    return pl.pallas_call(
        functools.partial(_sb_kernel, off=length - t, tq=tq, tk=tk, heads=H_SB),
        grid=(b, t // tq),
        in_specs=[pl.BlockSpec((1, tq, w), lambda bi, qi: (bi, qi, 0)),
                  pl.BlockSpec((1, lp, w), lambda bi, qi: (bi, 0, kc)),
                  pl.BlockSpec((1, lp, w), lambda bi, qi: (bi, 0, vc))],
        out_specs=pl.BlockSpec((1, tq, w), lambda bi, qi: (bi, qi, 0)),
        out_shape=jax.ShapeDtypeStruct((b, t, w), BF16),
        scratch_shapes=[pltpu.VMEM((H_SB, tq, tk), F32),
                        pltpu.VMEM((H_SB, tq, tk), F32),
                        pltpu.VMEM((H_SB, tq, tk), BF16),
                        pltpu.VMEM((H_SB, tq, tk), BF16),
                        pltpu.VMEM((H_SB, tq, tk), BF16),
                        pltpu.VMEM((tk, tk), BF16),
                        pltpu.VMEM((H_SB, tq, HEAD_DIM), F32),
                        pltpu.VMEM((H_SB, tq, LANES), F32)],
        compiler_params=_cparams(("parallel", "arbitrary")),
    )(q, k_arr, v_arr)


def _fox_kernel(q_ref, qb_ref, k_ref, kb_ref, v_ref, o_ref, s_scr, p_scr, acc, m_scr, l_scr, a_scr,
                *, off, tq, tk, heads):
    qi = pl.program_id(1)
    q0 = off + qi * tq
    kb_last = (q0 + tq - 1) // tk
    ng = tk // LANES
    nchunk = tq // BF16_ROWS

    acc[...] = jnp.zeros_like(acc)
    m_scr[...] = jnp.full_like(m_scr, M_INIT)
    l_scr[...] = jnp.zeros_like(l_scr)

    def block(kb, _, masked):
        k0 = pl.multiple_of(kb * tk, tk)
        kpos = k0 + lax.broadcasted_iota(jnp.int32, (1, LANES), 1)
        for h in range(heads):
            sl = slice(h * HEAD_DIM, (h + 1) * HEAD_DIM)
            q_aug = jnp.concatenate([q_ref[0, :, sl], qb_ref[0, :, sl]], axis=1)
            k_aug = jnp.concatenate([k_ref[0, pl.ds(k0, tk), sl], kb_ref[0, pl.ds(k0, tk), sl]],
                                    axis=1)
            s_scr[h] = lax.dot_general(q_aug, k_aug, NT_DIMS, preferred_element_type=F32)
        for h in range(heads):
            for c in range(nchunk):
                rows = slice(c * BF16_ROWS, (c + 1) * BF16_ROWS)
                parts = [s_scr[h, rows, g * LANES:(g + 1) * LANES] for g in range(ng)]
                if masked:
                    qpos = q0 + c * BF16_ROWS + lax.broadcasted_iota(jnp.int32, (BF16_ROWS, 1), 0)
                    parts = [jnp.where(kpos + g * LANES <= qpos, parts[g], -jnp.inf)
                             for g in range(ng)]
                ps, m_new, l_new, alpha = _softmax_rows(parts, m_scr[h, rows, :], l_scr[h, rows, :])
                m_scr[h, rows, :] = m_new
                l_scr[h, rows, :] = l_new
                a_scr[h, rows, :] = alpha
                for g in range(ng):
                    p_scr[h, rows, g * LANES:(g + 1) * LANES] = ps[g].astype(BF16)
        for h in range(heads):
            sl = slice(h * HEAD_DIM, (h + 1) * HEAD_DIM)
            acc[h] = a_scr[h] * acc[h] + jnp.dot(p_scr[h], v_ref[0, pl.ds(k0, tk), sl],
                                                 preferred_element_type=F32)
        return 0

    n_plain = jnp.minimum((q0 + 1) // tk, kb_last + 1)
    lax.fori_loop(0, n_plain, functools.partial(block, masked=False), 0)
    lax.fori_loop(n_plain, kb_last + 1, functools.partial(block, masked=True), 0)
    for h in range(heads):
        l = jnp.sum(l_scr[h], axis=-1, keepdims=True)
        o_ref[0, :, h * HEAD_DIM:(h + 1) * HEAD_DIM] = (acc[h] / l).astype(o_ref.dtype)


def _fox_bias_cols(parts, first, sign):
    b, r, h, n = parts.shape
    place = np.zeros((n + 1, HEAD_DIM), np.float32)
    for j in range(n):
        place[j, first + j] = sign
        place[n, (first + n + j) % (2 * n)] = 1.0
    src = jnp.concatenate([parts, jnp.ones((b, r, h, 1), BF16)], axis=-1)
    out = jnp.einsum('brhn,nc->brhc', src, jnp.asarray(place, BF16),
                     preferred_element_type=F32)
    return out.astype(BF16).reshape(b, r, h * HEAD_DIM)


def fox_attention(q, q_bias, k_arr, kc, k_bias, v_arr, vc, t, length, tq, tk):
    b = q.shape[0]
    lp = k_arr.shape[1]
    w = H_FOX * HEAD_DIM
    return pl.pallas_call(
        functools.partial(_fox_kernel, off=length - t, tq=tq, tk=tk, heads=H_FOX),
        grid=(b, t // tq),
        in_specs=[pl.BlockSpec((1, tq, w), lambda bi, qi: (bi, qi, 0)),
                  pl.BlockSpec((1, tq, w), lambda bi, qi: (bi, qi, 0)),
                  pl.BlockSpec((1, lp, w), lambda bi, qi: (bi, 0, kc)),
                  pl.BlockSpec((1, lp, w), lambda bi, qi: (bi, 0, 0)),
                  pl.BlockSpec((1, lp, w), lambda bi, qi: (bi, 0, vc))],
        out_specs=pl.BlockSpec((1, tq, w), lambda bi, qi: (bi, qi, 0)),
        out_shape=jax.ShapeDtypeStruct((b, t, w), BF16),
        scratch_shapes=[pltpu.VMEM((H_FOX, tq, tk), F32),
                        pltpu.VMEM((H_FOX, tq, tk), BF16),
                        pltpu.VMEM((H_FOX, tq, HEAD_DIM), F32),
                        pltpu.VMEM((H_FOX, tq, LANES), F32),
                        pltpu.VMEM((H_FOX, tq, LANES), F32),
                        pltpu.VMEM((H_FOX, tq, LANES), F32)],
        compiler_params=_cparams(("parallel", "arbitrary")),
    )(q, q_bias, k_arr, k_bias, v_arr)


def _moe_kernel(*refs, n_mix, n_groups, per_group, final_norm):
    x_ref, mix_refs = refs[0], refs[1:1 + n_mix]
    wo_ref, g_ref, wr_ref, wg_ref, wu_ref, wd_ref, gf_ref, o_ref, hb, comb = refs[1 + n_mix:]
    e = pl.program_id(1)
    n_exp = n_groups * per_group
    lane = lax.broadcasted_iota(jnp.int32, comb.shape, 1)

    @pl.when(e == 0)
    def _():
        x = x_ref[...]
        k0 = 0
        for m_ref in mix_refs:
            k = m_ref.shape[1]
            x = x + jnp.dot(m_ref[...], wo_ref[k0:k0 + k, :], preferred_element_type=F32)
            k0 += k
        h = _rms(x, g_ref[...])
        hb[...] = h.astype(BF16)
        h_hi = hb[...]
        h_lo = (h - h_hi.astype(F32)).astype(BF16)
        w = wr_ref[...]
        w_hi = w.astype(BF16)
        w_lo = (w - w_hi.astype(F32)).astype(BF16)
        logits = (jnp.dot(h_hi, w_hi, preferred_element_type=F32)
                  + jnp.dot(h_hi, w_lo, preferred_element_type=F32)
                  + jnp.dot(h_lo, w_hi, preferred_element_type=F32))
        lane_f = lane.astype(F32)
        gl = jnp.where(lane < n_groups, logits, -jnp.inf)
        gmax = jnp.max(gl, axis=-1, keepdims=True)
        gsel = jnp.min(jnp.where(gl == gmax, lane_f, float(LANES)), axis=-1, keepdims=True)
        p_group = 1.0 / jnp.sum(jnp.where(lane < n_groups, jnp.exp(gl - gmax), 0.0),
                                axis=-1, keepdims=True)
        first = n_groups + gsel * per_group
        el = jnp.where((lane_f >= first) & (lane_f < first + per_group), logits, -jnp.inf)
        v1 = jnp.max(el, axis=-1, keepdims=True)
        i1 = jnp.min(jnp.where(el == v1, lane_f, float(LANES)), axis=-1, keepdims=True)
        el2 = jnp.where(lane_f == i1, -jnp.inf, el)
        v2 = jnp.max(el2, axis=-1, keepdims=True)
        i2 = jnp.min(jnp.where(el2 == v2, lane_f, float(LANES)), axis=-1, keepdims=True)
        ratio = jnp.exp(v2 - v1)
        gate1 = p_group / (1.0 + ratio)
        comb[...] = jnp.where(lane_f == i1, gate1, jnp.where(lane_f == i2, gate1 * ratio, 0.0))
        o_ref[...] = x

    c = jnp.sum(jnp.where(lane == e + n_groups, comb[...], 0.0), axis=-1, keepdims=True)
    hv = hb[...]
    gate = jnp.dot(hv, wg_ref[0], preferred_element_type=F32)
    up = jnp.dot(hv, wu_ref[0], preferred_element_type=F32)
    act = gate * (1.0 / (1.0 + jnp.exp(-gate))) * up * c
    o_ref[...] += jnp.dot(act.astype(BF16), wd_ref[0], preferred_element_type=F32)

    if final_norm:
        @pl.when(e == n_exp - 1)
        def _():
            o_ref[...] = _rms(o_ref[...], gf_ref[...])


def moe_block(x, mix, w_out, g, w_group, w_router, w_gate, w_up, w_down, g_final=None):
    n, d = x.shape
    n_groups, per_group = w_router.shape[1], w_router.shape[2]
    n_exp, _, f = w_gate.shape
    w_route = jnp.concatenate(
        [w_group, w_router.reshape(d, n_exp),
         jnp.zeros((d, LANES - n_groups - n_exp), F32)], axis=1)
    final_norm = g_final is not None
    gf = (g_final if final_norm else g).reshape(1, d)
    tm = min(1024, n)
    return pl.pallas_call(
        functools.partial(_moe_kernel, n_mix=len(mix), n_groups=n_groups, per_group=per_group,
                          final_norm=final_norm),
        grid=(n // tm, n_exp),
        in_specs=[pl.BlockSpec((tm, d), lambda i, e: (i, 0))]
        + [pl.BlockSpec((tm, m.shape[1]), lambda i, e: (i, 0)) for m in mix]
        + [pl.BlockSpec(w_out.shape, lambda i, e: (0, 0)),
                  pl.BlockSpec((1, d), lambda i, e: (0, 0)),
                  pl.BlockSpec((d, LANES), lambda i, e: (0, 0)),
                  pl.BlockSpec((1, d, f), lambda i, e: (e, 0, 0)),
                  pl.BlockSpec((1, d, f), lambda i, e: (e, 0, 0)),
                  pl.BlockSpec((1, f, d), lambda i, e: (e, 0, 0)),
                  pl.BlockSpec((1, d), lambda i, e: (0, 0))],
        out_specs=pl.BlockSpec((tm, d), lambda i, e: (i, 0)),
        out_shape=jax.ShapeDtypeStruct((n, d), F32),
        scratch_shapes=[pltpu.VMEM((tm, d), BF16), pltpu.VMEM((tm, LANES), F32)],
        compiler_params=_cparams(("parallel", "arbitrary")),
    )(x, *mix, w_out, g.reshape(1, d), w_route, w_gate.astype(BF16), w_up.astype(BF16),
      w_down.astype(BF16), gf)


def _rope_tables(pos, head_dim, rows, live_lanes=LANES):
    half = head_dim // 2
    inv_freq = ROPE_THETA ** (-np.arange(half, dtype=np.float64) / half)
    ang = np.asarray(pos, np.float64)[:, None] * inv_freq[None, :]
    cos, sin = np.cos(ang), np.sin(ang)
    reps = LANES // head_dim
    cos = np.tile(np.concatenate([cos, cos], axis=1), (1, reps))
    sin = np.tile(np.concatenate([-sin, sin], axis=1), (1, reps))
    live = np.arange(LANES) < live_lanes
    cos = np.where(live, cos, 1.0)
    sin = np.where(live, sin, 0.0)
    reps_rows = max(rows // len(pos), 1)
    return (jnp.asarray(np.tile(cos, (reps_rows, 1)), F32),
            jnp.asarray(np.tile(sin, (reps_rows, 1)), F32))


def _code_to_float(code):
    return pltpu.bitcast(code ^ ((code >> 31) & 0x7FFFFFFF), F32)


def _dsa_reach(q_first, tq, length):
    return jnp.minimum(((q_first + tq - 1) // CHUNK + 1) * CHUNK, length)


def _dsa_select(qi_ref, w_ref, ki_ref, bias, sc, *, off, length, tq, ksel, cw):
    qb = pl.program_id(1)
    q0 = off + qb * tq
    qchunk = (q0 + lax.broadcasted_iota(jnp.int32, (tq, 1), 0)) // CHUNK
    nck = (_dsa_reach(q0, tq, length) + cw - 1) // cw
    w = w_ref[0] * (H_IDX ** -0.5)
    ng = cw // LANES
    kf = float(ksel)

    def fill(c, _):
        k0 = pl.multiple_of(c * cw, cw)
        kk = ki_ref[0, pl.ds(k0, cw), :]
        score = jnp.zeros((tq, cw), F32)
        for h in range(H_IDX):
            lg = lax.dot_general(qi_ref[0, :, h * D_IDX:(h + 1) * D_IDX], kk, NT_DIMS,
                                 preferred_element_type=F32) * (D_IDX ** -0.5)
            score = score + jnp.maximum(lg, 0.0) * w[:, h:h + 1]
        kpos = k0 + lax.broadcasted_iota(jnp.int32, (1, cw), 1)
        admissible = (kpos // CHUNK <= qchunk) & (kpos < length)
        sc[c] = jnp.where(admissible, score + 0.0, -jnp.inf)
        return 0

    lax.fori_loop(0, nck, fill, 0)

    def count(t, strict=False):
        tots = []
        for r0 in range(0, tq, SELECT_ROWS):
            rows = slice(r0, min(r0 + SELECT_ROWS, tq))
            t_rows = jnp.broadcast_to(t[rows], (rows.stop - rows.start, LANES))

            def body(c, tot, rows=rows, t_rows=t_rows):
                for g in range(ng):
                    x = sc[c, rows, g * LANES:(g + 1) * LANES]
                    hit = (x > t_rows) if strict else (x >= t_rows)
                    tot = tot + jnp.where(hit, 1.0, 0.0)
                return tot
            tots.append(lax.fori_loop(0, nck, body, jnp.zeros((t_rows.shape[0], LANES), F32)))
        return jnp.concatenate([jnp.sum(tot, axis=-1, keepdims=True) for tot in tots], axis=0)

    code0 = jnp.where(count(jnp.zeros((tq, 1), F32)) >= kf, 0, INT32_MIN).astype(jnp.int32)

    def bit_step(i, code):
        cand = code + lax.shift_left(jnp.int32(1), 30 - i)
        return jnp.where(count(_code_to_float(cand)) >= kf, cand, code)

    code = lax.fori_loop(0, 31, bit_step, code0)
    thr = _code_to_float(jnp.maximum(code, NEG_INF_CODE + 1))
    n_ge = count(thr)
    n_gt = count(thr, strict=True)
    tied = jnp.max(n_ge) > kf

    @pl.when(jnp.logical_not(tied))
    def _():
        def emit(c, _):
            bias[c] = jnp.where(sc[c] >= thr, 0.0, -jnp.inf).astype(bias.dtype)
            return 0
        lax.fori_loop(0, nck, emit, 0)

    @pl.when(tied)
    def _():
        need = kf - n_gt
        row = lax.broadcasted_iota(jnp.int32, (LANES, LANES), 0)
        col = lax.broadcasted_iota(jnp.int32, (LANES, LANES), 1)
        before = jnp.where(row < col, 1.0, 0.0).astype(BF16)

        def emit(c, run):
            for g in range(ng):
                sl = slice(g * LANES, (g + 1) * LANES)
                x = sc[c, :, sl]
                eq = jnp.where(x == thr, 1.0, 0.0)
                rank = run + jnp.dot(eq.astype(BF16), before, preferred_element_type=F32)
                sel = (x > thr) | ((x == thr) & (rank < need))
                bias[c, :, sl] = jnp.where(sel, 0.0, -jnp.inf).astype(bias.dtype)
                run = run + jnp.sum(eq, axis=-1, keepdims=True)
            return run
        lax.fori_loop(0, nck, emit, jnp.zeros((tq, 1), F32))


def _dsa_attend(q_ref, k_ref, v_ref, bias, o_ref, qs, s_scr, p_scr, acc, m_scr, l_scr, a_scr,
                *, off, length, tq, tk):
    qb = pl.program_id(1)
    rep = H_DSA // KV_DSA
    nkb = (_dsa_reach(off + qb * tq, tq, length) + tk - 1) // tk
    ng = tk // LANES
    nchunk = tq // BF16_ROWS

    for g in range(KV_DSA):
        for r in range(rep):
            hd = g * rep + r
            qs[g, r * tq:(r + 1) * tq, :] = q_ref[0, :, hd * HEAD_DIM:(hd + 1) * HEAD_DIM]
    acc[...] = jnp.zeros_like(acc)
    m_scr[...] = jnp.full_like(m_scr, M_INIT)
    l_scr[...] = jnp.zeros_like(l_scr)

    def scores(kb, dst):
        k0 = pl.multiple_of(kb * tk, tk)
        for g in range(KV_DSA):
            sl = slice(g * HEAD_DIM, (g + 1) * HEAD_DIM)
            dst[g] = lax.dot_general(qs[g], k_ref[0, pl.ds(k0, tk), sl], NT_DIMS,
                                     preferred_element_type=F32)

    def consume(kb, src):
        k0 = pl.multiple_of(kb * tk, tk)
        for c in range(nchunk):
            bias_c = [bias[kb, c * BF16_ROWS:(c + 1) * BF16_ROWS,
                           gl * LANES:(gl + 1) * LANES].astype(F32) for gl in range(ng)]
            for g in range(KV_DSA):
                for r in range(rep):
                    rows = slice(r * tq + c * BF16_ROWS, r * tq + (c + 1) * BF16_ROWS)
                    parts = [src[g, rows, gl * LANES:(gl + 1) * LANES] + bias_c[gl]
                             for gl in range(ng)]
                    ps, m_new, l_new, alpha = _softmax_rows(parts, m_scr[g, rows, :],
                                                            l_scr[g, rows, :])
                    m_scr[g, rows, :] = m_new
                    l_scr[g, rows, :] = l_new
                    a_scr[g, rows, :] = alpha
                    for gl in range(ng):
                        p_scr[g, rows, gl * LANES:(gl + 1) * LANES] = ps[gl].astype(BF16)
        for g in range(KV_DSA):
            sl = slice(g * HEAD_DIM, (g + 1) * HEAD_DIM)
            acc[g] = a_scr[g] * acc[g] + jnp.dot(p_scr[g], v_ref[0, pl.ds(k0, tk), sl],
                                                 preferred_element_type=F32)

    def block(kb, _):
        scores(kb, s_scr)
        consume(kb, s_scr)
        return 0

    lax.fori_loop(0, nkb, block, 0)
    for g in range(KV_DSA):
        o = acc[g] / jnp.sum(l_scr[g], axis=-1, keepdims=True)
        for r in range(rep):
            hd = g * rep + r
            o_ref[0, :, hd * HEAD_DIM:(hd + 1) * HEAD_DIM] = (
                o[r * tq:(r + 1) * tq]).astype(o_ref.dtype)


def _dsa_kernel(qi_ref, w_ref, ki_ref, q_ref, k_ref, v_ref, o_ref, sc, bias, qs, s_scr, p_scr, acc,
                m_scr, l_scr, a_scr, *, off, length, tq, tk, ksel):
    _dsa_select(qi_ref, w_ref, ki_ref, bias, sc, off=off, length=length, tq=tq, ksel=ksel, cw=tk)
    _dsa_attend(q_ref, k_ref, v_ref, bias, o_ref, qs, s_scr, p_scr, acc, m_scr, l_scr, a_scr,
                off=off, length=length, tq=tq, tk=tk)


def dsa_sparse_attention(qi, w_idx, ki, q, k_arr, v_arr, t, length, tq, tk):
    b, lp, _ = ki.shape
    wq = H_DSA * HEAD_DIM
    wk = KV_DSA * HEAD_DIM
    rep = H_DSA // KV_DSA
    nch = lp // tk
    per_q = lambda w: pl.BlockSpec((1, tq, w), lambda bi, qb: (bi, qb, 0))
    per_b = lambda w: pl.BlockSpec((1, lp, w), lambda bi, qb: (bi, 0, 0))
    return pl.pallas_call(
        functools.partial(_dsa_kernel, off=length - t, length=length, tq=tq, tk=tk,
                          ksel=min(TOPK_MAX, length // 4)),
        grid=(b, t // tq),
        in_specs=[per_q(H_IDX * D_IDX), per_q(H_IDX), per_b(D_IDX),
                  per_q(wq), per_b(wk), per_b(wk)],
        out_specs=per_q(wq),
        out_shape=jax.ShapeDtypeStruct((b, t, wq), BF16),
        scratch_shapes=[pltpu.VMEM((nch, tq, tk), F32),
                        pltpu.VMEM((nch, tq, tk), BF16),
                        pltpu.VMEM((KV_DSA, rep * tq, HEAD_DIM), BF16),
                        pltpu.VMEM((KV_DSA, rep * tq, tk), F32),
                        pltpu.VMEM((KV_DSA, rep * tq, tk), BF16),
                        pltpu.VMEM((KV_DSA, rep * tq, HEAD_DIM), F32),
                        pltpu.VMEM((KV_DSA, rep * tq, LANES), F32),
                        pltpu.VMEM((KV_DSA, rep * tq, LANES), F32),
                        pltpu.VMEM((KV_DSA, rep * tq, LANES), F32)],
        compiler_params=_cparams(("parallel", "arbitrary")),
    )(qi, w_idx, ki, q, k_arr, v_arr)


def _with_past(past, new, lp):
    b = new.shape[0]
    parts = [] if past is None else [past.reshape(b, past.shape[1], -1).astype(BF16)]
    parts.append(new.astype(BF16))
    rows = sum(p.shape[1] for p in parts)
    if rows < lp:
        parts.append(jnp.zeros((b, lp - rows, new.shape[2]), BF16))
    return parts[0] if len(parts) == 1 else jnp.concatenate(parts, axis=1)


def _mixer_ab(x2, b, t, past, g, w_in, b_forget):
    wsb = H_SB * HEAD_DIM
    wfx = H_FOX * HEAD_DIM
    qscale = LOG2E * HEAD_DIM ** -0.5
    c_fx = 3 * wsb
    c_gate = c_fx + 3 * wfx
    plan = [Segment(0, wsb, bf16_scale=qscale),
            Segment(wsb, wsb, f32=True, bf16_scale=1.0, heads_as_rows=True),
            Segment(2 * wsb, wsb, f32=True, bf16_scale=1.0, heads_as_rows=True),
            Segment(c_fx, wfx, bf16_scale=qscale),
            Segment(c_fx + wfx, wfx, f32=True, bf16_scale=1.0, heads_as_rows=True),
            Segment(c_fx + 2 * wfx, wfx, f32=True, bf16_scale=1.0, heads_as_rows=True),
            Segment(c_gate, LANES, f32=True)]
    w_pad = jnp.pad(w_in, ((0, 0), (0, c_gate + LANES - w_in.shape[1]))).astype(BF16)
    (q_sb, k_sb, k_sb_b, v_sb, v_sb_b, q_fx, k_fx, k_fx_b, v_fx, v_fx_b, gate) = norm_project(
        x2, g, w_pad, plan)
    rows = tuple(a.reshape(b, t, -1, HEAD_DIM) for a in (k_sb, v_sb, k_fx, v_fx))
    f_logit = gate.reshape(b, t, LANES)[:, :, :H_FOX]

    p = 0 if past is None else past[0].shape[1]
    length = p + t
    lp = _round_up(length, KEY_PAD)
    new_t = jnp.swapaxes(f_logit, 1, 2)
    parts = [new_t] if past is None else [jnp.swapaxes(past[4].astype(F32), 1, 2), new_t]
    if length < lp:
        parts.append(jnp.zeros((b, H_FOX, lp - length), F32))
    pre = (parts[0] if len(parts) == 1 else jnp.concatenate(parts, axis=2)).reshape(b * H_FOX, lp)
    b_col = jnp.tile(b_forget.astype(F32), b).reshape(b * H_FOX, 1)
    logf_t, *cum_parts = logf_cumsum(pre, b_col, p, length)
    logf = jnp.swapaxes(logf_t.reshape(b, H_FOX, lp)[:, :, p:length], 1, 2)

    f_parts = jnp.stack([jnp.swapaxes(c.reshape(b, H_FOX, lp), 1, 2) for c in cum_parts],
                        axis=-1)
    q_bias = _fox_bias_cols(f_parts[:, p:length], 0, 1.0)
    k_bias = _fox_bias_cols(f_parts, f_parts.shape[-1], -1.0)

    as3 = lambda a: a.reshape(b, t, -1)
    ks, vs, kf, vf = (_with_past(None if past is None else past[i], as3(a), lp)
                      for i, a in enumerate((k_sb_b, v_sb_b, k_fx_b, v_fx_b)))
    tq = min(256, t)
    o_sb = sb_attention(as3(q_sb), ks, 0, vs, 0, t, length, tq, 256)
    o_fx = fox_attention(as3(q_fx), q_bias, kf, 0, k_bias, vf, 0, t, length, tq, 512)
    return [o_sb.reshape(b * t, wsb), o_fx.reshape(b * t, wfx)], rows + (logf,)


def _mixer_dsa(x2, b, t, past, g, w_in):
    n = b * t
    wq = H_DSA * HEAD_DIM
    wk = KV_DSA * HEAD_DIM
    wi = H_IDX * D_IDX
    p = 0 if past is None else past[0].shape[1]
    length = p + t
    lp = _round_up(length, KEY_PAD)
    pos = p + np.arange(t)
    tab_rows = max(t, min(256, n))
    tables = [_rope_tables(pos, HEAD_DIM, tab_rows),
              _rope_tables(pos, D_IDX, tab_rows),
              _rope_tables(pos, D_IDX, tab_rows, live_lanes=D_IDX)]
    c_k, c_v, c_qi, c_ki = wq, wq + wk, wq + 2 * wk, wq + 2 * wk + wi
    plan = [Segment(0, wq, rope=0, half=HEAD_DIM // 2, bf16_scale=LOG2E * HEAD_DIM ** -0.5),
            Segment(c_k, wk, rope=0, half=HEAD_DIM // 2, f32=True, bf16_scale=1.0,
                    heads_as_rows=True),
            Segment(c_v, wk, f32=True, bf16_scale=1.0, heads_as_rows=True),
            Segment(c_qi, wi, rope=1, half=D_IDX // 2, bf16_scale=1.0),
            Segment(c_ki, LANES, rope=2, half=D_IDX // 2, f32=True, bf16_scale=1.0)]
    w_pad = jnp.pad(w_in, ((0, 0), (0, c_ki + LANES - w_in.shape[1]))).astype(BF16)
    q_b, k_f, k_b, v_f, v_b, qi_b, kw_f, kw_b = norm_project(x2, g, w_pad, plan, tables)
    as3 = lambda a: a.reshape(b, t, -1)
    k_rows = k_f.reshape(b, t, KV_DSA, HEAD_DIM)
    v_rows = v_f.reshape(b, t, KV_DSA, HEAD_DIM)
    ki_rows = as3(kw_f)[:, :, :D_IDX]
    w_idx = as3(kw_f)[:, :, D_IDX:D_IDX + H_IDX]

    tk = KEY_PAD
    ki_all = _with_past(None if past is None else past[2], as3(kw_b)[:, :, :D_IDX], lp)
    k_all = _with_past(None if past is None else past[0], as3(k_b), lp)
    v_all = _with_past(None if past is None else past[1], as3(v_b), lp)
    o = dsa_sparse_attention(as3(qi_b), w_idx, ki_all, as3(q_b), k_all, v_all, t, length,
                             min(128, t), tk)
    return [o.reshape(n, wq)], (k_rows, v_rows, ki_rows)


def _trunk(x, past_ab, past_dsa, norm_mix, norm_ffn, norm_final, w_in_ab, b_forget, w_out_ab,
           w_in_dsa, w_out_dsa, moe_w_group, moe_w_router, moe_w_gate, moe_w_up, moe_w_down):
    b, t, d = x.shape
    x2 = x.reshape(b * t, d)
    mix, rows_ab = _mixer_ab(x2, b, t, past_ab, norm_mix[0], w_in_ab, b_forget)
    x2 = moe_block(x2, mix, w_out_ab.astype(BF16), norm_ffn[0], moe_w_group[0], moe_w_router[0],
                   moe_w_gate[0], moe_w_up[0], moe_w_down[0])
    mix, rows_dsa = _mixer_dsa(x2, b, t, past_dsa, norm_mix[1], w_in_dsa)
    y = moe_block(x2, mix, w_out_dsa.astype(BF16), norm_ffn[1], moe_w_group[1], moe_w_router[1],
                  moe_w_gate[1], moe_w_up[1], moe_w_down[1], g_final=norm_final)
    return y.reshape(b, t, d), rows_ab, rows_dsa


def kernel(x_prompt, x_sample, cache_sb_k, cache_sb_v, cache_fox_k, cache_fox_v, cache_fox_logf,
           cache_dsa_k, cache_dsa_v, cache_dsa_idx_k, norm_mix, norm_ffn, norm_final,
           w_in_ab, b_forget, w_out_ab, w_in_dsa, w_out_dsa,
           moe_w_group, moe_w_router, moe_w_gate, moe_w_up, moe_w_down):
    weights = (norm_mix, norm_ffn, norm_final, w_in_ab, b_forget, w_out_ab, w_in_dsa, w_out_dsa,
               moe_w_group, moe_w_router, moe_w_gate, moe_w_up, moe_w_down)
    y_p, ab_p, dsa_p = _trunk(x_prompt, None, None, *weights)
    y_s, ab_s, dsa_s = _trunk(
        x_sample, (cache_sb_k, cache_sb_v, cache_fox_k, cache_fox_v, cache_fox_logf),
        (cache_dsa_k, cache_dsa_v, cache_dsa_idx_k), *weights)
    return (y_p, y_s) + ab_p + dsa_p + ab_s + dsa_s
```

```python
import functools
from typing import NamedTuple, Optional

import jax
import jax.numpy as jnp
import numpy as np
from jax import lax
from jax.experimental import pallas as pl
from jax.experimental.pallas import tpu as pltpu

CHUNK = 64
HEAD_DIM = 128
H_SB = 4
H_FOX = 4
H_DSA = 8
KV_DSA = 2
H_IDX = 4
D_IDX = 64
TOPK_MAX = 256
ROPE_THETA = 10000.0
EPS = 1e-6

LANES = 128
BF16_ROWS = 16

KEY_PAD = 512
LOG2E = 1.4426950408889634
M_INIT = -1e38
SB_DEAD_LOG2 = -150.0
INT32_MIN = -(2 ** 31)
SELECT_ROWS = 128
NEG_INF_CODE = int(np.int32(np.uint32(0xFF800000) ^ np.uint32(0x7FFFFFFF)))

F32 = jnp.float32
BF16 = jnp.bfloat16
NT_DIMS = (((1,), (1,)), ((), ()))


def _round_up(a, b):
    return (a + b - 1) // b * b


def _cparams(semantics, vmem_mib=48):
    return pltpu.CompilerParams(dimension_semantics=semantics,
                                vmem_limit_bytes=vmem_mib * 1024 * 1024)


def _rms(x, g):
    return x * lax.rsqrt(jnp.mean(x * x, axis=-1, keepdims=True) + EPS) * g


def _log_sigmoid(z):
    return jnp.minimum(z, 0.0) - jnp.log1p(jnp.exp(-jnp.abs(z)))


def _split3(x):
    hi = x.astype(BF16)
    r1 = x - hi.astype(F32)
    mid = r1.astype(BF16)
    lo = (r1 - mid.astype(F32)).astype(BF16)
    return hi, mid, lo


class Segment(NamedTuple):
    start: int
    width: int
    rope: Optional[int] = None
    half: int = 0
    f32: bool = False
    bf16_scale: Optional[float] = None
    heads_as_rows: bool = False


PROJ_TILE = 256


def _rotate(x, cos, sin, half):
    if 2 * half == LANES:
        partner = pltpu.roll(x, half, 1)
    else:
        lane = lax.broadcasted_iota(jnp.int32, x.shape, 1)
        partner = jnp.where(lane % (2 * half) < half,
                            pltpu.roll(x, LANES - half, 1), pltpu.roll(x, half, 1))
    return x * cos + partner * sin


def _proj_kernel(x_ref, g_ref, w_ref, *refs, plan, n_tab):
    tabs, outs = refs[:n_tab], list(refs[n_tab:])
    h = _rms(x_ref[...], g_ref[...]).astype(BF16)
    for seg in plan:
        o_f32 = outs.pop(0) if seg.f32 else None
        o_b16 = outs.pop(0) if seg.bf16_scale is not None else None
        tile = min(PROJ_TILE, seg.width)
        for j in range(seg.width // tile):
            acc = jnp.dot(h, w_ref[:, seg.start + j * tile:seg.start + (j + 1) * tile],
                          preferred_element_type=F32)
            if seg.rope is not None:
                cos, sin = tabs[2 * seg.rope][...], tabs[2 * seg.rope + 1][...]
                acc = jnp.concatenate(
                    [_rotate(acc[:, gi * LANES:(gi + 1) * LANES], cos, sin, seg.half)
                     for gi in range(tile // LANES)], axis=1)
            cols = slice(j * tile, (j + 1) * tile)
            if o_f32 is not None and seg.heads_as_rows:
                heads = seg.width // LANES
                for gi in range(tile // LANES):
                    head = j * (tile // LANES) + gi
                    o_f32[pl.ds(head, acc.shape[0], stride=heads), :] = (
                        acc[:, gi * LANES:(gi + 1) * LANES])
            elif o_f32 is not None:
                o_f32[:, cols] = acc
            if o_b16 is not None:
                o_b16[:, cols] = (acc * seg.bf16_scale).astype(BF16)


def norm_project(x, g, w, plan, tables=()):
    n, d = x.shape
    e = w.shape[1]
    tm = min(256, n)
    flat_tabs = [t for pair in tables for t in pair]
    tab_specs = []
    for tab in flat_tabs:
        period = tab.shape[0] // tm
        tab_specs.append(pl.BlockSpec((tm, LANES), lambda i, period=period: (i % period, 0)))
    out_specs, out_shape = [], []
    for seg in plan:
        if seg.f32:
            heads = seg.width // LANES if seg.heads_as_rows else 1
            out_specs.append(pl.BlockSpec((tm * heads, seg.width // heads), lambda i: (i, 0)))
            out_shape.append(jax.ShapeDtypeStruct((n * heads, seg.width // heads), F32))
        if seg.bf16_scale is not None:
            out_specs.append(pl.BlockSpec((tm, seg.width), lambda i: (i, 0)))
            out_shape.append(jax.ShapeDtypeStruct((n, seg.width), BF16))
    return pl.pallas_call(
        functools.partial(_proj_kernel, plan=tuple(plan), n_tab=len(flat_tabs)),
        grid=(n // tm,),
        in_specs=[pl.BlockSpec((tm, d), lambda i: (i, 0)),
                  pl.BlockSpec((1, d), lambda i: (0, 0)),
                  pl.BlockSpec((d, e), lambda i: (0, 0))] + tab_specs,
        out_specs=out_specs,
        out_shape=out_shape,
        compiler_params=_cparams(("parallel",)),
    )(x, g.reshape(1, d), w, *flat_tabs)


def _logf_cumsum_kernel(pre_ref, b_ref, logf_ref, hi_ref, mid_ref, lo_ref, carry,
                        *, past, length, cw):
    j = pl.program_id(0)

    @pl.when(j == 0)
    def _():
        carry[...] = jnp.zeros_like(carry)

    pre = pre_ref[...]
    pos = j * cw + lax.broadcasted_iota(jnp.int32, pre.shape, 1)
    x = jnp.where(pos >= past, _log_sigmoid(pre + b_ref[...]), pre)
    x = jnp.where(pos < length, x, 0.0)
    logf_ref[...] = x
    row = lax.broadcasted_iota(jnp.int32, (cw, cw), 0)
    col = lax.broadcasted_iota(jnp.int32, (cw, cw), 1)
    upper = jnp.where(row <= col, 1.0, 0.0).astype(BF16)
    hi, mid, lo = _split3(x)
    cs = (jnp.dot(hi, upper, preferred_element_type=F32)
          + jnp.dot(mid, upper, preferred_element_type=F32)
          + jnp.dot(lo, upper, preferred_element_type=F32))
    cum = cs + carry[:, 0:1]
    carry[...] = jnp.broadcast_to(cum[:, cw - 1:cw], carry.shape)
    hi_ref[...], mid_ref[...], lo_ref[...] = _split3(cum * LOG2E)


def logf_cumsum(pre, b_col, past, length):
    r, lp = pre.shape
    cw = KEY_PAD
    blk = pl.BlockSpec((r, cw), lambda j: (0, j))
    return pl.pallas_call(
        functools.partial(_logf_cumsum_kernel, past=past, length=length, cw=cw),
        grid=(lp // cw,),
        in_specs=[blk, pl.BlockSpec((r, 1), lambda j: (0, 0))],
        out_specs=[blk, blk, blk, blk],
        out_shape=[jax.ShapeDtypeStruct((r, lp), F32)] + [jax.ShapeDtypeStruct((r, lp), BF16)] * 3,
        scratch_shapes=[pltpu.VMEM((r, LANES), F32)],
        compiler_params=_cparams(("arbitrary",)),
    )(pre, b_col)


def _softmax_rows(s_parts, m_old, l_old):
    mx = s_parts[0]
    for sp in s_parts[1:]:
        mx = jnp.maximum(mx, sp)
    m_new = jnp.maximum(m_old, jnp.max(mx, axis=-1, keepdims=True))
    alpha = jnp.exp2(m_old - m_new)
    ps = [jnp.exp2(sp - m_new) for sp in s_parts]
    tot = ps[0]
    for p in ps[1:]:
        tot = tot + p
    return ps, m_new, alpha * l_old + tot, alpha


def _sb_kernel(q_ref, k_ref, v_ref, o_ref, s_scr, b_scr, hi_scr, lo_scr, p_scr, tri, acc, carry,
               *, off, tq, tk, heads):
    qi = pl.program_id(1)
    q0 = off + qi * tq
    kb_last = (q0 + tq - 2) // tk
    ng = tk // LANES
    nchunk = tq // BF16_ROWS

    acc[...] = jnp.zeros_like(acc)
    carry[...] = jnp.zeros_like(carry)
    row = lax.broadcasted_iota(jnp.int32, (tk, tk), 0)
    col = lax.broadcasted_iota(jnp.int32, (tk, tk), 1)
    tri[...] = jnp.where(row > col, 1.0, 0.0).astype(BF16)

    def block(state):
        kb, _ = state
        k0 = pl.multiple_of(kb * tk, tk)
        kpos = k0 + lax.broadcasted_iota(jnp.int32, (1, LANES), 1)
        for h in range(heads):
            sl = slice(h * HEAD_DIM, (h + 1) * HEAD_DIM)
            s_scr[h] = lax.dot_general(q_ref[0, :, sl], k_ref[0, pl.ds(k0, tk), sl], NT_DIMS,
                                       preferred_element_type=F32)
        for h in range(heads):
            for c in range(nchunk):
                rows = slice(c * BF16_ROWS, (c + 1) * BF16_ROWS)
                qpos = q0 + c * BF16_ROWS + lax.broadcasted_iota(jnp.int32, (BF16_ROWS, 1), 0)
                for g in range(ng):
                    ls = slice(g * LANES, (g + 1) * LANES)
                    z = s_scr[h, rows, ls]
                    lsig = jnp.minimum(z, 0.0) - jnp.log2(1.0 + jnp.exp2(-jnp.abs(z)))
                    l1m = jnp.where(kpos + g * LANES < qpos, lsig - z, 0.0)
                    hi = l1m.astype(BF16)
                    hi_scr[h, rows, ls] = hi
                    lo_scr[h, rows, ls] = (l1m - hi.astype(F32)).astype(BF16)
                    s_scr[h, rows, ls] = lsig
        for h in range(heads):
            b_scr[h] = (jnp.dot(hi_scr[h], tri[...], preferred_element_type=F32)
                        + jnp.dot(lo_scr[h], tri[...], preferred_element_type=F32))
        for h in range(heads):
            for c in range(nchunk):
                rows = slice(c * BF16_ROWS, (c + 1) * BF16_ROWS)
                qpos = q0 + c * BF16_ROWS + lax.broadcasted_iota(jnp.int32, (BF16_ROWS, 1), 0)
                run = carry[h, rows, :]
                for g in range(ng):
                    ls = slice(g * LANES, (g + 1) * LANES)
                    a = jnp.exp2(s_scr[h, rows, ls] + b_scr[h, rows, ls] + run)
                    a = jnp.where(kpos + g * LANES < qpos, a, 0.0)
                    p_scr[h, rows, ls] = a.astype(BF16)
                first = (b_scr[h, rows, 0:1] + hi_scr[h, rows, 0:1].astype(F32)
                         + lo_scr[h, rows, 0:1].astype(F32))
                carry[h, rows, :] = run + first
        for h in range(heads):
            sl = slice(h * HEAD_DIM, (h + 1) * HEAD_DIM)
            acc[h] += jnp.dot(p_scr[h], v_ref[0, pl.ds(k0, tk), sl], preferred_element_type=F32)
        return kb - 1, jnp.max(carry[...]) > SB_DEAD_LOG2

    lax.while_loop(lambda st: jnp.logical_and(st[0] >= 0, st[1]), block,
                   (kb_last, jnp.bool_(True)))
    for h in range(heads):
        o_ref[0, :, h * HEAD_DIM:(h + 1) * HEAD_DIM] = acc[h].astype(o_ref.dtype)


def sb_attention(q, k_arr, kc, v_arr, vc, t, length, tq, tk):
    b = q.shape[0]
    lp = k_arr.shape[1]
    w = H_SB * HEAD_DIM
    return pl.pallas_call(
        functools.partial(_sb_kernel, off=length - t, tq=tq, tk=tk, heads=H_SB),
        grid=(b, t // tq),
        in_specs=[pl.BlockSpec((1, tq, w), lambda bi, qi: (bi, qi, 0)),
                  pl.BlockSpec((1, lp, w), lambda bi, qi: (bi, 0, kc)),
                  pl.BlockSpec((1, lp, w), lambda bi, qi: (bi, 0, vc))],
        out_specs=pl.BlockSpec((1, tq, w), lambda bi, qi: (bi, qi, 0)),
        out_shape=jax.ShapeDtypeStruct((b, t, w), BF16),
        scratch_shapes=[pltpu.VMEM((H_SB, tq, tk), F32),
                        pltpu.VMEM((H_SB, tq, tk), F32),
                        pltpu.VMEM((H_SB, tq, tk), BF16),
                        pltpu.VMEM((H_SB, tq, tk), BF16),
                        pltpu.VMEM((H_SB, tq, tk), BF16),
                        pltpu.VMEM((tk, tk), BF16),
                        pltpu.VMEM((H_SB, tq, HEAD_DIM), F32),
                        pltpu.VMEM((H_SB, tq, LANES), F32)],
        compiler_params=_cparams(("parallel", "arbitrary")),
    )(q, k_arr, v_arr)


def _fox_kernel(q_ref, qb_ref, k_ref, kb_ref, v_ref, o_ref, s_scr, p_scr, acc, m_scr, l_scr, a_scr,
                *, off, tq, tk, heads):
    qi = pl.program_id(1)
    q0 = off + qi * tq
    kb_last = (q0 + tq - 1) // tk
    ng = tk // LANES
    nchunk = tq // BF16_ROWS

    acc[...] = jnp.zeros_like(acc)
    m_scr[...] = jnp.full_like(m_scr, M_INIT)
    l_scr[...] = jnp.zeros_like(l_scr)

    def block(kb, _, masked):
        k0 = pl.multiple_of(kb * tk, tk)
        kpos = k0 + lax.broadcasted_iota(jnp.int32, (1, LANES), 1)
        for h in range(heads):
            sl = slice(h * HEAD_DIM, (h + 1) * HEAD_DIM)
            q_aug = jnp.concatenate([q_ref[0, :, sl], qb_ref[0, :, sl]], axis=1)
            k_aug = jnp.concatenate([k_ref[0, pl.ds(k0, tk), sl], kb_ref[0, pl.ds(k0, tk), sl]],
                                    axis=1)
            s_scr[h] = lax.dot_general(q_aug, k_aug, NT_DIMS, preferred_element_type=F32)
        for h in range(heads):
            for c in range(nchunk):
                rows = slice(c * BF16_ROWS, (c + 1) * BF16_ROWS)
                parts = [s_scr[h, rows, g * LANES:(g + 1) * LANES] for g in range(ng)]
                if masked:
                    qpos = q0 + c * BF16_ROWS + lax.broadcasted_iota(jnp.int32, (BF16_ROWS, 1), 0)
                    parts = [jnp.where(kpos + g * LANES <= qpos, parts[g], -jnp.inf)
                             for g in range(ng)]
                ps, m_new, l_new, alpha = _softmax_rows(parts, m_scr[h, rows, :], l_scr[h, rows, :])
                m_scr[h, rows, :] = m_new
                l_scr[h, rows, :] = l_new
                a_scr[h, rows, :] = alpha
                for g in range(ng):
                    p_scr[h, rows, g * LANES:(g + 1) * LANES] = ps[g].astype(BF16)
        for h in range(heads):
            sl = slice(h * HEAD_DIM, (h + 1) * HEAD_DIM)
            acc[h] = a_scr[h] * acc[h] + jnp.dot(p_scr[h], v_ref[0, pl.ds(k0, tk), sl],
                                                 preferred_element_type=F32)
        return 0

    n_plain = jnp.minimum((q0 + 1) // tk, kb_last + 1)
    lax.fori_loop(0, n_plain, functools.partial(block, masked=False), 0)
    lax.fori_loop(n_plain, kb_last + 1, functools.partial(block, masked=True), 0)
    for h in range(heads):
        l = jnp.sum(l_scr[h], axis=-1, keepdims=True)
        o_ref[0, :, h * HEAD_DIM:(h + 1) * HEAD_DIM] = (acc[h] / l).astype(o_ref.dtype)


def _fox_bias_cols(parts, first, sign):
    b, r, h, n = parts.shape
    place = np.zeros((h, n + 1, h, HEAD_DIM), np.float32)
    for hd in range(h):
        for j in range(n):
            place[hd, j, hd, first + j] = sign
            place[hd, n, hd, (first + n + j) % (2 * n)] = 1.0
    src = jnp.concatenate([parts, jnp.ones((b, r, h, 1), BF16)], axis=-1)
    out = jnp.einsum('brk,kc->brc', src.reshape(b, r, h * (n + 1)),
                     jnp.asarray(place.reshape(h * (n + 1), h * HEAD_DIM), BF16),
                     preferred_element_type=F32)
    return out.astype(BF16)


def fox_attention(q, q_bias, k_arr, kc, k_bias, v_arr, vc, t, length, tq, tk):
    b = q.shape[0]
    lp = k_arr.shape[1]
    w = H_FOX * HEAD_DIM
    return pl.pallas_call(
        functools.partial(_fox_kernel, off=length - t, tq=tq, tk=tk, heads=H_FOX),
        grid=(b, t // tq),
        in_specs=[pl.BlockSpec((1, tq, w), lambda bi, qi: (bi, qi, 0)),
                  pl.BlockSpec((1, tq, w), lambda bi, qi: (bi, qi, 0)),
                  pl.BlockSpec((1, lp, w), lambda bi, qi: (bi, 0, kc)),
                  pl.BlockSpec((1, lp, w), lambda bi, qi: (bi, 0, 0)),
                  pl.BlockSpec((1, lp, w), lambda bi, qi: (bi, 0, vc))],
        out_specs=pl.BlockSpec((1, tq, w), lambda bi, qi: (bi, qi, 0)),
        out_shape=jax.ShapeDtypeStruct((b, t, w), BF16),
        scratch_shapes=[pltpu.VMEM((H_FOX, tq, tk), F32),
                        pltpu.VMEM((H_FOX, tq, tk), BF16),
                        pltpu.VMEM((H_FOX, tq, HEAD_DIM), F32),
                        pltpu.VMEM((H_FOX, tq, LANES), F32),
                        pltpu.VMEM((H_FOX, tq, LANES), F32),
                        pltpu.VMEM((H_FOX, tq, LANES), F32)],
        compiler_params=_cparams(("parallel", "arbitrary")),
    )(q, q_bias, k_arr, k_bias, v_arr)


def _moe_kernel(*refs, n_mix, n_groups, per_group, final_norm):
    x_ref, mix_refs = refs[0], refs[1:1 + n_mix]
    wo_ref, g_ref, wr_ref, wg_ref, wu_ref, wd_ref, gf_ref, o_ref, hb, comb = refs[1 + n_mix:]
    e = pl.program_id(1)
    n_exp = n_groups * per_group
    lane = lax.broadcasted_iota(jnp.int32, comb.shape, 1)

    @pl.when(e == 0)
    def _():
        x = x_ref[...]
        k0 = 0
        for m_ref in mix_refs:
            k = m_ref.shape[1]
            x = x + jnp.dot(m_ref[...], wo_ref[k0:k0 + k, :], preferred_element_type=F32)
            k0 += k
        h = _rms(x, g_ref[...])
        hb[...] = h.astype(BF16)
        h_hi = hb[...]
        h_lo = (h - h_hi.astype(F32)).astype(BF16)
        w = wr_ref[...]
        w_hi = w.astype(BF16)
        w_lo = (w - w_hi.astype(F32)).astype(BF16)
        logits = (jnp.dot(h_hi, w_hi, preferred_element_type=F32)
                  + jnp.dot(h_hi, w_lo, preferred_element_type=F32)
                  + jnp.dot(h_lo, w_hi, preferred_element_type=F32))
        lane_f = lane.astype(F32)
        gl = jnp.where(lane < n_groups, logits, -jnp.inf)
        gmax = jnp.max(gl, axis=-1, keepdims=True)
        gsel = jnp.min(jnp.where(gl == gmax, lane_f, float(LANES)), axis=-1, keepdims=True)
        p_group = 1.0 / jnp.sum(jnp.where(lane < n_groups, jnp.exp(gl - gmax), 0.0),
                                axis=-1, keepdims=True)
        first = n_groups + gsel * per_group
        el = jnp.where((lane_f >= first) & (lane_f < first + per_group), logits, -jnp.inf)
        v1 = jnp.max(el, axis=-1, keepdims=True)
        i1 = jnp.min(jnp.where(el == v1, lane_f, float(LANES)), axis=-1, keepdims=True)
        el2 = jnp.where(lane_f == i1, -jnp.inf, el)
        v2 = jnp.max(el2, axis=-1, keepdims=True)
        i2 = jnp.min(jnp.where(el2 == v2, lane_f, float(LANES)), axis=-1, keepdims=True)
        ratio = jnp.exp(v2 - v1)
        gate1 = p_group / (1.0 + ratio)
        comb[...] = jnp.where(lane_f == i1, gate1, jnp.where(lane_f == i2, gate1 * ratio, 0.0))
        o_ref[...] = x

    c = jnp.sum(jnp.where(lane == e + n_groups, comb[...], 0.0), axis=-1, keepdims=True)
    hv = hb[...]
    gate = jnp.dot(hv, wg_ref[0], preferred_element_type=F32)
    up = jnp.dot(hv, wu_ref[0], preferred_element_type=F32)
    act = gate * (1.0 / (1.0 + jnp.exp(-gate))) * up * c
    o_ref[...] += jnp.dot(act.astype(BF16), wd_ref[0], preferred_element_type=F32)

    if final_norm:
        @pl.when(e == n_exp - 1)
        def _():
            o_ref[...] = _rms(o_ref[...], gf_ref[...])


def moe_block(x, mix, w_out, g, w_group, w_router, w_gate, w_up, w_down, g_final=None):
    n, d = x.shape
    n_groups, per_group = w_router.shape[1], w_router.shape[2]
    n_exp, _, f = w_gate.shape
    w_route = jnp.concatenate(
        [w_group, w_router.reshape(d, n_exp),
         jnp.zeros((d, LANES - n_groups - n_exp), F32)], axis=1)
    final_norm = g_final is not None
    gf = (g_final if final_norm else g).reshape(1, d)
    tm = min(1024, n)
    return pl.pallas_call(
        functools.partial(_moe_kernel, n_mix=len(mix), n_groups=n_groups, per_group=per_group,
                          final_norm=final_norm),
        grid=(n // tm, n_exp),
        in_specs=[pl.BlockSpec((tm, d), lambda i, e: (i, 0))]
        + [pl.BlockSpec((tm, m.shape[1]), lambda i, e: (i, 0)) for m in mix]
        + [pl.BlockSpec(w_out.shape, lambda i, e: (0, 0)),
                  pl.BlockSpec((1, d), lambda i, e: (0, 0)),
                  pl.BlockSpec((d, LANES), lambda i, e: (0, 0)),
                  pl.BlockSpec((1, d, f), lambda i, e: (e, 0, 0)),
                  pl.BlockSpec((1, d, f), lambda i, e: (e, 0, 0)),
                  pl.BlockSpec((1, f, d), lambda i, e: (e, 0, 0)),
                  pl.BlockSpec((1, d), lambda i, e: (0, 0))],
        out_specs=pl.BlockSpec((tm, d), lambda i, e: (i, 0)),
        out_shape=jax.ShapeDtypeStruct((n, d), F32),
        scratch_shapes=[pltpu.VMEM((tm, d), BF16), pltpu.VMEM((tm, LANES), F32)],
        compiler_params=_cparams(("parallel", "arbitrary")),
    )(x, *mix, w_out, g.reshape(1, d), w_route, w_gate.astype(BF16), w_up.astype(BF16),
      w_down.astype(BF16), gf)


def _rope_tables(pos, head_dim, rows, live_lanes=LANES):
    half = head_dim // 2
    inv_freq = ROPE_THETA ** (-jnp.arange(half, dtype=F32) / half)
    ang = pos.astype(F32)[:, None] * inv_freq[None, :]
    cos, sin = jnp.cos(ang), jnp.sin(ang)
    reps = LANES // head_dim
    cos = jnp.tile(jnp.concatenate([cos, cos], axis=1), (1, reps))
    sin = jnp.tile(jnp.concatenate([-sin, sin], axis=1), (1, reps))
    live = jnp.arange(LANES) < live_lanes
    cos = jnp.where(live, cos, 1.0)
    sin = jnp.where(live, sin, 0.0)
    reps_rows = max(rows // pos.shape[0], 1)
    return jnp.tile(cos, (reps_rows, 1)), jnp.tile(sin, (reps_rows, 1))


def _code_to_float(code):
    return pltpu.bitcast(code ^ ((code >> 31) & 0x7FFFFFFF), F32)


def _dsa_reach(q_first, tq, length):
    return jnp.minimum(((q_first + tq - 1) // CHUNK + 1) * CHUNK, length)


def _dsa_select(qi_ref, w_ref, ki_ref, bias, sc, *, off, length, tq, ksel, cw):
    qb = pl.program_id(1)
    q0 = off + qb * tq
    qchunk = (q0 + lax.broadcasted_iota(jnp.int32, (tq, 1), 0)) // CHUNK
    nck = (_dsa_reach(q0, tq, length) + cw - 1) // cw
    w = w_ref[0] * (H_IDX ** -0.5 * D_IDX ** -0.5)
    ng = cw // LANES
    kf = float(ksel)

    def fill(c, _):
        k0 = pl.multiple_of(c * cw, cw)
        kk = ki_ref[0, pl.ds(k0, cw), :]
        score = jnp.zeros((tq, cw), F32)
        for h in range(H_IDX):
            lg = lax.dot_general(qi_ref[0, :, h * D_IDX:(h + 1) * D_IDX], kk, NT_DIMS,
                                 preferred_element_type=F32)
            score = score + jnp.maximum(lg, 0.0) * w[:, h:h + 1]
        kpos = k0 + lax.broadcasted_iota(jnp.int32, (1, cw), 1)
        admissible = (kpos // CHUNK <= qchunk) & (kpos < length)
        sc[c] = jnp.where(admissible, score + 0.0, -jnp.inf)
        return 0

    lax.fori_loop(0, nck, fill, 0)

    def count(t, strict=False):
        tots = []
        for r0 in range(0, tq, SELECT_ROWS):
            rows = slice(r0, min(r0 + SELECT_ROWS, tq))
            t_rows = jnp.broadcast_to(t[rows], (rows.stop - rows.start, LANES))

            def body(c, tot, rows=rows, t_rows=t_rows):
                for g in range(ng):
                    x = sc[c, rows, g * LANES:(g + 1) * LANES]
                    hit = (x > t_rows) if strict else (x >= t_rows)
                    tot = tot + jnp.where(hit, 1.0, 0.0)
                return tot
            tots.append(lax.fori_loop(0, nck, body, jnp.zeros((t_rows.shape[0], LANES), F32)))
        return jnp.concatenate([jnp.sum(tot, axis=-1, keepdims=True) for tot in tots], axis=0)

    code0 = jnp.where(count(jnp.zeros((tq, 1), F32)) >= kf, 0, INT32_MIN).astype(jnp.int32)

    def bit_step(i, code):
        cand = code + lax.shift_left(jnp.int32(1), 30 - i)
        return jnp.where(count(_code_to_float(cand)) >= kf, cand, code)

    code = lax.fori_loop(0, 31, bit_step, code0)
    thr = _code_to_float(jnp.maximum(code, NEG_INF_CODE + 1))
    n_ge = count(thr)
    n_gt = count(thr, strict=True)
    tied = jnp.max(n_ge) > kf

    @pl.when(jnp.logical_not(tied))
    def _():
        def emit(c, _):
            bias[c] = jnp.where(sc[c] >= thr, 0.0, -jnp.inf).astype(bias.dtype)
            return 0
        lax.fori_loop(0, nck, emit, 0)

    @pl.when(tied)
    def _():
        need = kf - n_gt
        row = lax.broadcasted_iota(jnp.int32, (LANES, LANES), 0)
        col = lax.broadcasted_iota(jnp.int32, (LANES, LANES), 1)
        before = jnp.where(row < col, 1.0, 0.0).astype(BF16)

        def emit(c, run):
            for g in range(ng):
                sl = slice(g * LANES, (g + 1) * LANES)
                x = sc[c, :, sl]
                eq = jnp.where(x == thr, 1.0, 0.0)
                rank = run + jnp.dot(eq.astype(BF16), before, preferred_element_type=F32)
                sel = (x > thr) | ((x == thr) & (rank < need))
                bias[c, :, sl] = jnp.where(sel, 0.0, -jnp.inf).astype(bias.dtype)
                run = run + jnp.sum(eq, axis=-1, keepdims=True)
            return run
        lax.fori_loop(0, nck, emit, jnp.zeros((tq, 1), F32))


def _dsa_attend(q_ref, k_ref, v_ref, bias, o_ref, qs, s_scr, p_scr, acc, m_scr, l_scr, a_scr,
                *, off, length, tq, tk):
    qb = pl.program_id(1)
    rep = H_DSA // KV_DSA
    nkb = (_dsa_reach(off + qb * tq, tq, length) + tk - 1) // tk
    ng = tk // LANES
    nchunk = tq // BF16_ROWS

    for g in range(KV_DSA):
        for r in range(rep):
            hd = g * rep + r
            qs[g, r * tq:(r + 1) * tq, :] = q_ref[0, :, hd * HEAD_DIM:(hd + 1) * HEAD_DIM]
    acc[...] = jnp.zeros_like(acc)
    m_scr[...] = jnp.full_like(m_scr, M_INIT)
    l_scr[...] = jnp.zeros_like(l_scr)

    def scores(kb, dst):
        k0 = pl.multiple_of(kb * tk, tk)
        for g in range(KV_DSA):
            sl = slice(g * HEAD_DIM, (g + 1) * HEAD_DIM)
            dst[g] = lax.dot_general(qs[g], k_ref[0, pl.ds(k0, tk), sl], NT_DIMS,
                                     preferred_element_type=F32)

    def consume(kb, src):
        k0 = pl.multiple_of(kb * tk, tk)
        for c in range(nchunk):
            bias_c = [bias[kb, c * BF16_ROWS:(c + 1) * BF16_ROWS,
                           gl * LANES:(gl + 1) * LANES].astype(F32) for gl in range(ng)]
            for g in range(KV_DSA):
                for r in range(rep):
                    rows = slice(r * tq + c * BF16_ROWS, r * tq + (c + 1) * BF16_ROWS)
                    parts = [src[g, rows, gl * LANES:(gl + 1) * LANES] + bias_c[gl]
                             for gl in range(ng)]
                    ps, m_new, l_new, alpha = _softmax_rows(parts, m_scr[g, rows, :],
                                                            l_scr[g, rows, :])
                    m_scr[g, rows, :] = m_new
                    l_scr[g, rows, :] = l_new
                    a_scr[g, rows, :] = alpha
                    for gl in range(ng):
                        p_scr[g, rows, gl * LANES:(gl + 1) * LANES] = ps[gl].astype(BF16)
        for g in range(KV_DSA):
            sl = slice(g * HEAD_DIM, (g + 1) * HEAD_DIM)
            acc[g] = a_scr[g] * acc[g] + jnp.dot(p_scr[g], v_ref[0, pl.ds(k0, tk), sl],
                                                 preferred_element_type=F32)

    def block(kb, _):
        scores(kb, s_scr)
        consume(kb, s_scr)
        return 0

    lax.fori_loop(0, nkb, block, 0)
    for g in range(KV_DSA):
        o = acc[g] / jnp.sum(l_scr[g], axis=-1, keepdims=True)
        for r in range(rep):
            hd = g * rep + r
            o_ref[0, :, hd * HEAD_DIM:(hd + 1) * HEAD_DIM] = (
                o[r * tq:(r + 1) * tq]).astype(o_ref.dtype)


def _dsa_kernel(qi_ref, w_ref, ki_ref, q_ref, k_ref, v_ref, o_ref, sc, bias, qs, s_scr, p_scr, acc,
                m_scr, l_scr, a_scr, *, off, length, tq, tk, ksel):
    _dsa_select(qi_ref, w_ref, ki_ref, bias, sc, off=off, length=length, tq=tq, ksel=ksel, cw=tk)
    _dsa_attend(q_ref, k_ref, v_ref, bias, o_ref, qs, s_scr, p_scr, acc, m_scr, l_scr, a_scr,
                off=off, length=length, tq=tq, tk=tk)


def dsa_sparse_attention(qi, w_idx, ki, q, k_arr, v_arr, t, length, tq, tk):
    b, lp, _ = ki.shape
    wq = H_DSA * HEAD_DIM
    wk = KV_DSA * HEAD_DIM
    rep = H_DSA // KV_DSA
    nch = lp // tk
    per_q = lambda w: pl.BlockSpec((1, tq, w), lambda bi, qb: (bi, qb, 0))
    per_b = lambda w: pl.BlockSpec((1, lp, w), lambda bi, qb: (bi, 0, 0))
    return pl.pallas_call(
        functools.partial(_dsa_kernel, off=length - t, length=length, tq=tq, tk=tk,
                          ksel=min(TOPK_MAX, length // 4)),
        grid=(b, t // tq),
        in_specs=[per_q(H_IDX * D_IDX), per_q(H_IDX), per_b(D_IDX),
                  per_q(wq), per_b(wk), per_b(wk)],
        out_specs=per_q(wq),
        out_shape=jax.ShapeDtypeStruct((b, t, wq), BF16),
        scratch_shapes=[pltpu.VMEM((nch, tq, tk), F32),
                        pltpu.VMEM((nch, tq, tk), BF16),
                        pltpu.VMEM((KV_DSA, rep * tq, HEAD_DIM), BF16),
                        pltpu.VMEM((KV_DSA, rep * tq, tk), F32),
                        pltpu.VMEM((KV_DSA, rep * tq, tk), BF16),
                        pltpu.VMEM((KV_DSA, rep * tq, HEAD_DIM), F32),
                        pltpu.VMEM((KV_DSA, rep * tq, LANES), F32),
                        pltpu.VMEM((KV_DSA, rep * tq, LANES), F32),
                        pltpu.VMEM((KV_DSA, rep * tq, LANES), F32)],
        compiler_params=_cparams(("parallel", "arbitrary")),
    )(qi, w_idx, ki, q, k_arr, v_arr)


def _with_past(past, new, lp):
    b = new.shape[0]
    parts = [] if past is None else [past.reshape(b, past.shape[1], -1).astype(BF16)]
    parts.append(new.astype(BF16))
    rows = sum(p.shape[1] for p in parts)
    if rows < lp:
        parts.append(jnp.zeros((b, lp - rows, new.shape[2]), BF16))
    return parts[0] if len(parts) == 1 else jnp.concatenate(parts, axis=1)


def _mixer_ab(x2, b, t, past, g, w_in, b_forget):
    wsb = H_SB * HEAD_DIM
    wfx = H_FOX * HEAD_DIM
    qscale = LOG2E * HEAD_DIM ** -0.5
    c_fx = 3 * wsb
    c_gate = c_fx + 3 * wfx
    plan = [Segment(0, wsb, bf16_scale=qscale),
            Segment(wsb, wsb, f32=True, bf16_scale=1.0, heads_as_rows=True),
            Segment(2 * wsb, wsb, f32=True, bf16_scale=1.0, heads_as_rows=True),
            Segment(c_fx, wfx, bf16_scale=qscale),
            Segment(c_fx + wfx, wfx, f32=True, bf16_scale=1.0, heads_as_rows=True),
            Segment(c_fx + 2 * wfx, wfx, f32=True, bf16_scale=1.0, heads_as_rows=True),
            Segment(c_gate, LANES, f32=True)]
    w_pad = jnp.pad(w_in, ((0, 0), (0, c_gate + LANES - w_in.shape[1]))).astype(BF16)
    (q_sb, k_sb, k_sb_b, v_sb, v_sb_b, q_fx, k_fx, k_fx_b, v_fx, v_fx_b, gate) = norm_project(
        x2, g, w_pad, plan)
    rows = tuple(a.reshape(b, t, -1, HEAD_DIM) for a in (k_sb, v_sb, k_fx, v_fx))
    f_logit = gate.reshape(b, t, LANES)[:, :, :H_FOX]

    p = 0 if past is None else past[0].shape[1]
    length = p + t
    lp = _round_up(length, KEY_PAD)
    new_t = jnp.swapaxes(f_logit, 1, 2)
    parts = [new_t] if past is None else [jnp.swapaxes(past[4].astype(F32), 1, 2), new_t]
    if length < lp:
        parts.append(jnp.zeros((b, H_FOX, lp - length), F32))
    pre = (parts[0] if len(parts) == 1 else jnp.concatenate(parts, axis=2)).reshape(b * H_FOX, lp)
    b_col = jnp.tile(b_forget.astype(F32), b).reshape(b * H_FOX, 1)
    logf_t, *cum_parts = logf_cumsum(pre, b_col, p, length)
    logf = jnp.swapaxes(logf_t.reshape(b, H_FOX, lp)[:, :, p:length], 1, 2)

    f_parts = jnp.stack([jnp.swapaxes(c.reshape(b, H_FOX, lp), 1, 2) for c in cum_parts],
                        axis=-1)
    q_bias = _fox_bias_cols(f_parts[:, p:length], 0, 1.0)
    k_bias = _fox_bias_cols(f_parts, f_parts.shape[-1], -1.0)

    as3 = lambda a: a.reshape(b, t, -1)
    ks, vs, kf, vf = (_with_past(None if past is None else past[i], as3(a), lp)
                      for i, a in enumerate((k_sb_b, v_sb_b, k_fx_b, v_fx_b)))
    tq = min(256, t)
    o_sb = sb_attention(as3(q_sb), ks, 0, vs, 0, t, length, tq, 256)
    o_fx = fox_attention(as3(q_fx), q_bias, kf, 0, k_bias, vf, 0, t, length, tq, 512)
    return [o_sb.reshape(b * t, wsb), o_fx.reshape(b * t, wfx)], rows + (logf,)


def _mixer_dsa(x2, b, t, past, g, w_in):
    n = b * t
    wq = H_DSA * HEAD_DIM
    wk = KV_DSA * HEAD_DIM
    wi = H_IDX * D_IDX
    p = 0 if past is None else past[0].shape[1]
    length = p + t
    lp = _round_up(length, KEY_PAD)
    pos = p + jnp.arange(t)
    tab_rows = max(t, min(256, n))
    tables = [_rope_tables(pos, HEAD_DIM, tab_rows),
              _rope_tables(pos, D_IDX, tab_rows),
              _rope_tables(pos, D_IDX, tab_rows, live_lanes=D_IDX)]
    c_k, c_v, c_qi, c_ki = wq, wq + wk, wq + 2 * wk, wq + 2 * wk + wi
    plan = [Segment(0, wq, rope=0, half=HEAD_DIM // 2, bf16_scale=LOG2E * HEAD_DIM ** -0.5),
            Segment(c_k, wk, rope=0, half=HEAD_DIM // 2, f32=True, bf16_scale=1.0,
                    heads_as_rows=True),
            Segment(c_v, wk, f32=True, bf16_scale=1.0, heads_as_rows=True),
            Segment(c_qi, wi, rope=1, half=D_IDX // 2, bf16_scale=1.0),
            Segment(c_ki, LANES, rope=2, half=D_IDX // 2, f32=True, bf16_scale=1.0)]
    w_pad = jnp.pad(w_in, ((0, 0), (0, c_ki + LANES - w_in.shape[1]))).astype(BF16)
    q_b, k_f, k_b, v_f, v_b, qi_b, kw_f, kw_b = norm_project(x2, g, w_pad, plan, tables)
    as3 = lambda a: a.reshape(b, t, -1)
    k_rows = k_f.reshape(b, t, KV_DSA, HEAD_DIM)
    v_rows = v_f.reshape(b, t, KV_DSA, HEAD_DIM)
    ki_rows = as3(kw_f)[:, :, :D_IDX]
    w_idx = as3(kw_f)[:, :, D_IDX:D_IDX + H_IDX]

    tk = KEY_PAD
    ki_all = _with_past(None if past is None else past[2], as3(kw_b)[:, :, :D_IDX], lp)
    k_all = _with_past(None if past is None else past[0], as3(k_b), lp)
    v_all = _with_past(None if past is None else past[1], as3(v_b), lp)
    o = dsa_sparse_attention(as3(qi_b), w_idx, ki_all, as3(q_b), k_all, v_all, t, length,
                             min(128, t), tk)
    return [o.reshape(n, wq)], (k_rows, v_rows, ki_rows)


def _trunk(x, past_ab, past_dsa, norm_mix, norm_ffn, norm_final, w_in_ab, b_forget, w_out_ab,
           w_in_dsa, w_out_dsa, moe_w_group, moe_w_router, moe_w_gate, moe_w_up, moe_w_down):
    b, t, d = x.shape
    x2 = x.reshape(b * t, d)
    mix, rows_ab = _mixer_ab(x2, b, t, past_ab, norm_mix[0], w_in_ab, b_forget)
    x2 = moe_block(x2, mix, w_out_ab.astype(BF16), norm_ffn[0], moe_w_group[0], moe_w_router[0],
                   moe_w_gate[0], moe_w_up[0], moe_w_down[0])
    mix, rows_dsa = _mixer_dsa(x2, b, t, past_dsa, norm_mix[1], w_in_dsa)
    y = moe_block(x2, mix, w_out_dsa.astype(BF16), norm_ffn[1], moe_w_group[1], moe_w_router[1],
                  moe_w_gate[1], moe_w_up[1], moe_w_down[1], g_final=norm_final)
    return y.reshape(b, t, d), rows_ab, rows_dsa


def kernel(x_prompt, x_sample, cache_sb_k, cache_sb_v, cache_fox_k, cache_fox_v, cache_fox_logf,
           cache_dsa_k, cache_dsa_v, cache_dsa_idx_k, norm_mix, norm_ffn, norm_final,
           w_in_ab, b_forget, w_out_ab, w_in_dsa, w_out_dsa,
           moe_w_group, moe_w_router, moe_w_gate, moe_w_up, moe_w_down):
    weights = (norm_mix, norm_ffn, norm_final, w_in_ab, b_forget, w_out_ab, w_in_dsa, w_out_dsa,
               moe_w_group, moe_w_router, moe_w_gate, moe_w_up, moe_w_down)
    y_p, ab_p, dsa_p = _trunk(x_prompt, None, None, *weights)
    y_s, ab_s, dsa_s = _trunk(
        x_sample, (cache_sb_k, cache_sb_v, cache_fox_k, cache_fox_v, cache_fox_logf),
        (cache_dsa_k, cache_dsa_v, cache_dsa_idx_k), *weights)
    return (y_p, y_s) + ab_p + dsa_p + ab_s + dsa_s
```

```python
import functools
from typing import NamedTuple, Optional

import jax
import jax.numpy as jnp
import numpy as np
from jax import lax
from jax.experimental import pallas as pl
from jax.experimental.pallas import tpu as pltpu

CHUNK = 64
HEAD_DIM = 128
H_SB = 4
H_FOX = 4
H_DSA = 8
KV_DSA = 2
H_IDX = 4
D_IDX = 64
TOPK_MAX = 256
ROPE_THETA = 10000.0
EPS = 1e-6

LANES = 128
SUBLANES = 8
BF16_ROWS = 16

KEY_PAD = 512
LOG2E = 1.4426950408889634
M_INIT = -1e38
SB_DEAD_LOG2 = -150.0
INT32_MIN = -(2 ** 31)
SELECT_Q = 128
COUNT_ROWS = 64
NEG_INF_CODE = int(np.int32(np.uint32(0xFF800000) ^ np.uint32(0x7FFFFFFF)))

F32 = jnp.float32
BF16 = jnp.bfloat16
NT_DIMS = (((1,), (1,)), ((), ()))


def _round_up(a, b):
    return (a + b - 1) // b * b


def _cparams(semantics, vmem_mib=48):
    return pltpu.CompilerParams(dimension_semantics=semantics,
                                vmem_limit_bytes=vmem_mib * 1024 * 1024)


def _rms(x, g):
    return x * lax.rsqrt(jnp.mean(x * x, axis=-1, keepdims=True) + EPS) * g


def _log_sigmoid(z):
    return jnp.minimum(z, 0.0) - jnp.log1p(jnp.exp(-jnp.abs(z)))


def _split3(x):
    hi = x.astype(BF16)
    r1 = x - hi.astype(F32)
    mid = r1.astype(BF16)
    lo = (r1 - mid.astype(F32)).astype(BF16)
    return hi, mid, lo


class Segment(NamedTuple):
    start: int
    width: int
    rope: Optional[int] = None
    half: int = 0
    f32: bool = False
    bf16_scale: Optional[float] = None
    heads_as_rows: bool = False


PROJ_TILE = 256


def _rotate(x, cos, sin, half):
    if 2 * half == LANES:
        partner = pltpu.roll(x, half, 1)
    else:
        lane = lax.broadcasted_iota(jnp.int32, x.shape, 1)
        partner = jnp.where(lane % (2 * half) < half,
                            pltpu.roll(x, LANES - half, 1), pltpu.roll(x, half, 1))
    return x * cos + partner * sin


def _proj_kernel(x_ref, g_ref, w_ref, *refs, plan, n_tab):
    tabs, outs = refs[:n_tab], list(refs[n_tab:])
    h = _rms(x_ref[...], g_ref[...]).astype(BF16)
    for seg in plan:
        o_f32 = outs.pop(0) if seg.f32 else None
        o_b16 = outs.pop(0) if seg.bf16_scale is not None else None
        tile = min(PROJ_TILE, seg.width)
        for j in range(seg.width // tile):
            acc = jnp.dot(h, w_ref[:, seg.start + j * tile:seg.start + (j + 1) * tile],
                          preferred_element_type=F32)
            if seg.rope is not None:
                cos, sin = tabs[2 * seg.rope][...], tabs[2 * seg.rope + 1][...]
                acc = jnp.concatenate(
                    [_rotate(acc[:, gi * LANES:(gi + 1) * LANES], cos, sin, seg.half)
                     for gi in range(tile // LANES)], axis=1)
            cols = slice(j * tile, (j + 1) * tile)
            if o_f32 is not None and seg.heads_as_rows:
                heads = seg.width // LANES
                for gi in range(tile // LANES):
                    head = j * (tile // LANES) + gi
                    o_f32[pl.ds(head, acc.shape[0], stride=heads), :] = (
                        acc[:, gi * LANES:(gi + 1) * LANES])
            elif o_f32 is not None:
                o_f32[:, cols] = acc
            if o_b16 is not None:
                o_b16[:, cols] = (acc * seg.bf16_scale).astype(BF16)


def norm_project(x, g, w, plan, tables=()):
    n, d = x.shape
    e = w.shape[1]
    tm = min(256, n)
    flat_tabs = [t for pair in tables for t in pair]
    tab_specs = []
    for tab in flat_tabs:
        period = tab.shape[0] // tm
        tab_specs.append(pl.BlockSpec((tm, LANES), lambda i, period=period: (i % period, 0)))
    out_specs, out_shape = [], []
    for seg in plan:
        if seg.f32:
            heads = seg.width // LANES if seg.heads_as_rows else 1
            out_specs.append(pl.BlockSpec((tm * heads, seg.width // heads), lambda i: (i, 0)))
            out_shape.append(jax.ShapeDtypeStruct((n * heads, seg.width // heads), F32))
        if seg.bf16_scale is not None:
            out_specs.append(pl.BlockSpec((tm, seg.width), lambda i: (i, 0)))
            out_shape.append(jax.ShapeDtypeStruct((n, seg.width), BF16))
    return pl.pallas_call(
        functools.partial(_proj_kernel, plan=tuple(plan), n_tab=len(flat_tabs)),
        grid=(n // tm,),
        in_specs=[pl.BlockSpec((tm, d), lambda i: (i, 0)),
                  pl.BlockSpec((1, d), lambda i: (0, 0)),
                  pl.BlockSpec((d, e), lambda i: (0, 0))] + tab_specs,
        out_specs=out_specs,
        out_shape=out_shape,
        compiler_params=_cparams(("parallel",)),
    )(x, g.reshape(1, d), w, *flat_tabs)


def _logf_cumsum_kernel(pre_ref, b_ref, logf_ref, hi_ref, mid_ref, lo_ref, carry,
                        *, past, length, cw):
    j = pl.program_id(0)

    @pl.when(j == 0)
    def _():
        carry[...] = jnp.zeros_like(carry)

    pre = pre_ref[...]
    pos = j * cw + lax.broadcasted_iota(jnp.int32, pre.shape, 1)
    x = jnp.where(pos >= past, _log_sigmoid(pre + b_ref[...]), pre)
    x = jnp.where(pos < length, x, 0.0)
    logf_ref[...] = x
    row = lax.broadcasted_iota(jnp.int32, (cw, cw), 0)
    col = lax.broadcasted_iota(jnp.int32, (cw, cw), 1)
    upper = jnp.where(row <= col, 1.0, 0.0).astype(BF16)
    hi, mid, lo = _split3(x)
    cs = (jnp.dot(hi, upper, preferred_element_type=F32)
          + jnp.dot(mid, upper, preferred_element_type=F32)
          + jnp.dot(lo, upper, preferred_element_type=F32))
    cum = cs + carry[:, 0:1]
    carry[...] = jnp.broadcast_to(cum[:, cw - 1:cw], carry.shape)
    hi_ref[...], mid_ref[...], lo_ref[...] = _split3(cum * LOG2E)


def logf_cumsum(pre, b_col, past, length):
    r, lp = pre.shape
    cw = KEY_PAD
    blk = pl.BlockSpec((r, cw), lambda j: (0, j))
    return pl.pallas_call(
        functools.partial(_logf_cumsum_kernel, past=past, length=length, cw=cw),
        grid=(lp // cw,),
        in_specs=[blk, pl.BlockSpec((r, 1), lambda j: (0, 0))],
        out_specs=[blk, blk, blk, blk],
        out_shape=[jax.ShapeDtypeStruct((r, lp), F32)] + [jax.ShapeDtypeStruct((r, lp), BF16)] * 3,
        scratch_shapes=[pltpu.VMEM((r, LANES), F32)],
        compiler_params=_cparams(("arbitrary",)),
    )(pre, b_col)


def _softmax_rows(s_parts, m_old, l_old):
    mx = s_parts[0]
    for sp in s_parts[1:]:
        mx = jnp.maximum(mx, sp)
    m_new = jnp.maximum(m_old, jnp.max(mx, axis=-1, keepdims=True))
    alpha = jnp.exp2(m_old - m_new)
    ps = [jnp.exp2(sp - m_new) for sp in s_parts]
    tot = ps[0]
    for p in ps[1:]:
        tot = tot + p
    return ps, m_new, alpha * l_old + tot, alpha


def _sb_kernel(q_ref, k_ref, v_ref, o_ref, s_scr, b_scr, hi_scr, lo_scr, p_scr, tri, acc, carry,
               *, off, tq, tk, heads):
    qi = pl.program_id(1)
    q0 = off + qi * tq
    kb_last = (q0 + tq - 2) // tk
    ng = tk // LANES
    nchunk = tq // BF16_ROWS

    acc[...] = jnp.zeros_like(acc)
    carry[...] = jnp.zeros_like(carry)
    row = lax.broadcasted_iota(jnp.int32, (tk, tk), 0)
    col = lax.broadcasted_iota(jnp.int32, (tk, tk), 1)
    tri[...] = jnp.where(row > col, 1.0, 0.0).astype(BF16)

    def block(state):
        kb, _ = state
        k0 = pl.multiple_of(kb * tk, tk)
        kpos = k0 + lax.broadcasted_iota(jnp.int32, (1, LANES), 1)
        for h in range(heads):
            sl = slice(h * HEAD_DIM, (h + 1) * HEAD_DIM)
            s_scr[h] = lax.dot_general(q_ref[0, :, sl], k_ref[0, pl.ds(k0, tk), sl], NT_DIMS,
                                       preferred_element_type=F32)
        for h in range(heads):
            for c in range(nchunk):
                rows = slice(c * BF16_ROWS, (c + 1) * BF16_ROWS)
                qpos = q0 + c * BF16_ROWS + lax.broadcasted_iota(jnp.int32, (BF16_ROWS, 1), 0)
                for g in range(ng):
                    ls = slice(g * LANES, (g + 1) * LANES)
                    z = s_scr[h, rows, ls]
                    lsig = jnp.minimum(z, 0.0) - jnp.log2(1.0 + jnp.exp2(-jnp.abs(z)))
                    l1m = jnp.where(kpos + g * LANES < qpos, lsig - z, 0.0)
                    hi = l1m.astype(BF16)
                    hi_scr[h, rows, ls] = hi
                    lo_scr[h, rows, ls] = (l1m - hi.astype(F32)).astype(BF16)
                    s_scr[h, rows, ls] = lsig
        for h in range(heads):
            b_scr[h] = (jnp.dot(hi_scr[h], tri[...], preferred_element_type=F32)
                        + jnp.dot(lo_scr[h], tri[...], preferred_element_type=F32))
        for h in range(heads):
            for c in range(nchunk):
                rows = slice(c * BF16_ROWS, (c + 1) * BF16_ROWS)
                qpos = q0 + c * BF16_ROWS + lax.broadcasted_iota(jnp.int32, (BF16_ROWS, 1), 0)
                run = carry[h, rows, :]
                for g in range(ng):
                    ls = slice(g * LANES, (g + 1) * LANES)
                    a = jnp.exp2(s_scr[h, rows, ls] + b_scr[h, rows, ls] + run)
                    a = jnp.where(kpos + g * LANES < qpos, a, 0.0)
                    p_scr[h, rows, ls] = a.astype(BF16)
                first = (b_scr[h, rows, 0:1] + hi_scr[h, rows, 0:1].astype(F32)
                         + lo_scr[h, rows, 0:1].astype(F32))
                carry[h, rows, :] = run + first
        for h in range(heads):
            sl = slice(h * HEAD_DIM, (h + 1) * HEAD_DIM)
            acc[h] += jnp.dot(p_scr[h], v_ref[0, pl.ds(k0, tk), sl], preferred_element_type=F32)
        return kb - 1, jnp.max(carry[...]) > SB_DEAD_LOG2

    lax.while_loop(lambda st: jnp.logical_and(st[0] >= 0, st[1]), block,
                   (kb_last, jnp.bool_(True)))
    for h in range(heads):
        o_ref[0, :, h * HEAD_DIM:(h + 1) * HEAD_DIM] = acc[h].astype(o_ref.dtype)


def sb_attention(q, k_arr, kc, v_arr, vc, t, length, tq, tk):
    b = q.shape[0]
    lp = k_arr.shape[1]
    w = H_SB * HEAD_DIM
    return pl.pallas_call(
        functools.partial(_sb_kernel, off=length - t, tq=tq, tk=tk, heads=H_SB),
        grid=(b, t // tq),
        in_specs=[pl.BlockSpec((1, tq, w), lambda bi, qi: (bi, qi, 0)),
                  pl.BlockSpec((1, lp, w), lambda bi, qi: (bi, 0, kc)),
                  pl.BlockSpec((1, lp, w), lambda bi, qi: (bi, 0, vc))],
        out_specs=pl.BlockSpec((1, tq, w), lambda bi, qi: (bi, qi, 0)),
        out_shape=jax.ShapeDtypeStruct((b, t, w), BF16),
        scratch_shapes=[pltpu.VMEM((H_SB, tq, tk), F32),
                        pltpu.VMEM((H_SB, tq, tk), F32),
                        pltpu.VMEM((H_SB, tq, tk), BF16),
                        pltpu.VMEM((H_SB, tq, tk), BF16),
                        pltpu.VMEM((H_SB, tq, tk), BF16),
                        pltpu.VMEM((tk, tk), BF16),
                        pltpu.VMEM((H_SB, tq, HEAD_DIM), F32),
                        pltpu.VMEM((H_SB, tq, LANES), F32)],
        compiler_params=_cparams(("parallel", "arbitrary")),
    )(q, k_arr, v_arr)


def _fox_kernel(q_ref, qb_ref, k_ref, kb_ref, v_ref, o_ref, s_scr, p_scr, acc, m_scr, l_scr, a_scr,
                *, off, tq, tk, heads):
    qi = pl.program_id(1)
    q0 = off + qi * tq
    kb_last = (q0 + tq - 1) // tk
    ng = tk // LANES
    nchunk = tq // BF16_ROWS

    acc[...] = jnp.zeros_like(acc)
    m_scr[...] = jnp.full_like(m_scr, M_INIT)
    l_scr[...] = jnp.zeros_like(l_scr)

    def block(kb, _, masked):
        k0 = pl.multiple_of(kb * tk, tk)
        kpos = k0 + lax.broadcasted_iota(jnp.int32, (1, LANES), 1)
        for h in range(heads):
            sl = slice(h * HEAD_DIM, (h + 1) * HEAD_DIM)
            q_aug = jnp.concatenate([q_ref[0, :, sl], qb_ref[0, :, sl]], axis=1)
            k_aug = jnp.concatenate([k_ref[0, pl.ds(k0, tk), sl], kb_ref[0, pl.ds(k0, tk), sl]],
                                    axis=1)
            s_scr[h] = lax.dot_general(q_aug, k_aug, NT_DIMS, preferred_element_type=F32)
        for h in range(heads):
            for c in range(nchunk):
                rows = slice(c * BF16_ROWS, (c + 1) * BF16_ROWS)
                parts = [s_scr[h, rows, g * LANES:(g + 1) * LANES] for g in range(ng)]
                if masked:
                    qpos = q0 + c * BF16_ROWS + lax.broadcasted_iota(jnp.int32, (BF16_ROWS, 1), 0)
                    parts = [jnp.where(kpos + g * LANES <= qpos, parts[g], -jnp.inf)
                             for g in range(ng)]
                ps, m_new, l_new, alpha = _softmax_rows(parts, m_scr[h, rows, :], l_scr[h, rows, :])
                m_scr[h, rows, :] = m_new
                l_scr[h, rows, :] = l_new
                a_scr[h, rows, :] = alpha
                for g in range(ng):
                    p_scr[h, rows, g * LANES:(g + 1) * LANES] = ps[g].astype(BF16)
        for h in range(heads):
            sl = slice(h * HEAD_DIM, (h + 1) * HEAD_DIM)
            acc[h] = a_scr[h] * acc[h] + jnp.dot(p_scr[h], v_ref[0, pl.ds(k0, tk), sl],
                                                 preferred_element_type=F32)
        return 0

    n_plain = jnp.minimum((q0 + 1) // tk, kb_last + 1)
    lax.fori_loop(0, n_plain, functools.partial(block, masked=False), 0)
    lax.fori_loop(n_plain, kb_last + 1, functools.partial(block, masked=True), 0)
    for h in range(heads):
        l = jnp.sum(l_scr[h], axis=-1, keepdims=True)
        o_ref[0, :, h * HEAD_DIM:(h + 1) * HEAD_DIM] = (acc[h] / l).astype(o_ref.dtype)


def _fox_bias_cols(parts, first, sign):
    b, r, h, n = parts.shape
    place = np.zeros((h, n + 1, h, HEAD_DIM), np.float32)
    for hd in range(h):
        for j in range(n):
            place[hd, j, hd, first + j] = sign
            place[hd, n, hd, (first + n + j) % (2 * n)] = 1.0
    src = jnp.concatenate([parts, jnp.ones((b, r, h, 1), BF16)], axis=-1)
    out = jnp.einsum('brk,kc->brc', src.reshape(b, r, h * (n + 1)),
                     jnp.asarray(place.reshape(h * (n + 1), h * HEAD_DIM), BF16),
                     preferred_element_type=F32)
    return out.astype(BF16)


def fox_attention(q, q_bias, k_arr, kc, k_bias, v_arr, vc, t, length, tq, tk):
    b = q.shape[0]
    lp = k_arr.shape[1]
    w = H_FOX * HEAD_DIM
    return pl.pallas_call(
        functools.partial(_fox_kernel, off=length - t, tq=tq, tk=tk, heads=H_FOX),
        grid=(b, t // tq),
        in_specs=[pl.BlockSpec((1, tq, w), lambda bi, qi: (bi, qi, 0)),
                  pl.BlockSpec((1, tq, w), lambda bi, qi: (bi, qi, 0)),
                  pl.BlockSpec((1, lp, w), lambda bi, qi: (bi, 0, kc)),
                  pl.BlockSpec((1, lp, w), lambda bi, qi: (bi, 0, 0)),
                  pl.BlockSpec((1, lp, w), lambda bi, qi: (bi, 0, vc))],
        out_specs=pl.BlockSpec((1, tq, w), lambda bi, qi: (bi, qi, 0)),
        out_shape=jax.ShapeDtypeStruct((b, t, w), BF16),
        scratch_shapes=[pltpu.VMEM((H_FOX, tq, tk), F32),
                        pltpu.VMEM((H_FOX, tq, tk), BF16),
                        pltpu.VMEM((H_FOX, tq, HEAD_DIM), F32),
                        pltpu.VMEM((H_FOX, tq, LANES), F32),
                        pltpu.VMEM((H_FOX, tq, LANES), F32),
                        pltpu.VMEM((H_FOX, tq, LANES), F32)],
        compiler_params=_cparams(("parallel", "arbitrary")),
    )(q, q_bias, k_arr, k_bias, v_arr)


def _moe_kernel(*refs, n_mix, n_groups, per_group, final_norm):
    x_ref, mix_refs = refs[0], refs[1:1 + n_mix]
    wo_ref, g_ref, wr_ref, wg_ref, wu_ref, wd_ref, gf_ref, o_ref, hb, comb = refs[1 + n_mix:]
    e = pl.program_id(1)
    n_exp = n_groups * per_group
    lane = lax.broadcasted_iota(jnp.int32, comb.shape, 1)

    @pl.when(e == 0)
    def _():
        x = x_ref[...]
        k0 = 0
        for m_ref in mix_refs:
            k = m_ref.shape[1]
            x = x + jnp.dot(m_ref[...], wo_ref[k0:k0 + k, :], preferred_element_type=F32)
            k0 += k
        h = _rms(x, g_ref[...])
        hb[...] = h.astype(BF16)
        h_hi = hb[...]
        h_lo = (h - h_hi.astype(F32)).astype(BF16)
        w = wr_ref[...]
        w_hi = w.astype(BF16)
        w_lo = (w - w_hi.astype(F32)).astype(BF16)
        logits = (jnp.dot(h_hi, w_hi, preferred_element_type=F32)
                  + jnp.dot(h_hi, w_lo, preferred_element_type=F32)
                  + jnp.dot(h_lo, w_hi, preferred_element_type=F32))
        lane_f = lane.astype(F32)
        gl = jnp.where(lane < n_groups, logits, -jnp.inf)
        gmax = jnp.max(gl, axis=-1, keepdims=True)
        gsel = jnp.min(jnp.where(gl == gmax, lane_f, float(LANES)), axis=-1, keepdims=True)
        p_group = 1.0 / jnp.sum(jnp.where(lane < n_groups, jnp.exp(gl - gmax), 0.0),
                                axis=-1, keepdims=True)
        first = n_groups + gsel * per_group
        el = jnp.where((lane_f >= first) & (lane_f < first + per_group), logits, -jnp.inf)
        v1 = jnp.max(el, axis=-1, keepdims=True)
        i1 = jnp.min(jnp.where(el == v1, lane_f, float(LANES)), axis=-1, keepdims=True)
        el2 = jnp.where(lane_f == i1, -jnp.inf, el)
        v2 = jnp.max(el2, axis=-1, keepdims=True)
        i2 = jnp.min(jnp.where(el2 == v2, lane_f, float(LANES)), axis=-1, keepdims=True)
        ratio = jnp.exp(v2 - v1)
        gate1 = p_group / (1.0 + ratio)
        comb[...] = jnp.where(lane_f == i1, gate1, jnp.where(lane_f == i2, gate1 * ratio, 0.0))
        o_ref[...] = x

    c = jnp.sum(jnp.where(lane == e + n_groups, comb[...], 0.0), axis=-1, keepdims=True)
    hv = hb[...]
    gate = jnp.dot(hv, wg_ref[0], preferred_element_type=F32)
    up = jnp.dot(hv, wu_ref[0], preferred_element_type=F32)
    act = gate * (1.0 / (1.0 + jnp.exp(-gate))) * up * c
    o_ref[...] += jnp.dot(act.astype(BF16), wd_ref[0], preferred_element_type=F32)

    if final_norm:
        @pl.when(e == n_exp - 1)
        def _():
            o_ref[...] = _rms(o_ref[...], gf_ref[...])


def moe_block(x, mix, w_out, g, w_group, w_router, w_gate, w_up, w_down, g_final=None):
    n, d = x.shape
    n_groups, per_group = w_router.shape[1], w_router.shape[2]
    n_exp, _, f = w_gate.shape
    w_route = jnp.concatenate(
        [w_group, w_router.reshape(d, n_exp),
         jnp.zeros((d, LANES - n_groups - n_exp), F32)], axis=1)
    final_norm = g_final is not None
    gf = (g_final if final_norm else g).reshape(1, d)
    tm = min(1024, n)
    return pl.pallas_call(
        functools.partial(_moe_kernel, n_mix=len(mix), n_groups=n_groups, per_group=per_group,
                          final_norm=final_norm),
        grid=(n // tm, n_exp),
        in_specs=[pl.BlockSpec((tm, d), lambda i, e: (i, 0))]
        + [pl.BlockSpec((tm, m.shape[1]), lambda i, e: (i, 0)) for m in mix]
        + [pl.BlockSpec(w_out.shape, lambda i, e: (0, 0)),
                  pl.BlockSpec((1, d), lambda i, e: (0, 0)),
                  pl.BlockSpec((d, LANES), lambda i, e: (0, 0)),
                  pl.BlockSpec((1, d, f), lambda i, e: (e, 0, 0)),
                  pl.BlockSpec((1, d, f), lambda i, e: (e, 0, 0)),
                  pl.BlockSpec((1, f, d), lambda i, e: (e, 0, 0)),
                  pl.BlockSpec((1, d), lambda i, e: (0, 0))],
        out_specs=pl.BlockSpec((tm, d), lambda i, e: (i, 0)),
        out_shape=jax.ShapeDtypeStruct((n, d), F32),
        scratch_shapes=[pltpu.VMEM((tm, d), BF16), pltpu.VMEM((tm, LANES), F32)],
        compiler_params=_cparams(("parallel", "arbitrary")),
    )(x, *mix, w_out, g.reshape(1, d), w_route, w_gate.astype(BF16), w_up.astype(BF16),
      w_down.astype(BF16), gf)


def _rope_tables(pos, head_dim, rows, live_lanes=LANES):
    half = head_dim // 2
    inv_freq = ROPE_THETA ** (-jnp.arange(half, dtype=F32) / half)
    ang = pos.astype(F32)[:, None] * inv_freq[None, :]
    cos, sin = jnp.cos(ang), jnp.sin(ang)
    reps = LANES // head_dim
    cos = jnp.tile(jnp.concatenate([cos, cos], axis=1), (1, reps))
    sin = jnp.tile(jnp.concatenate([-sin, sin], axis=1), (1, reps))
    live = jnp.arange(LANES) < live_lanes
    cos = jnp.where(live, cos, 1.0)
    sin = jnp.where(live, sin, 0.0)
    reps_rows = max(rows // pos.shape[0], 1)
    return jnp.tile(cos, (reps_rows, 1)), jnp.tile(sin, (reps_rows, 1))


def _code_to_float(code):
    return pltpu.bitcast(code ^ ((code >> 31) & 0x7FFFFFFF), F32)


def _dsa_reach(q_first, tq, length):
    return jnp.minimum(((q_first + tq - 1) // CHUNK + 1) * CHUNK, length)


def _dsa_select(qi_ref, wt_ref, ki_ref, bias, sc, *, off, length, tq, ksel, cw):
    qb = pl.program_id(1)
    q0 = off + qb * tq
    qchunk = (q0 + lax.broadcasted_iota(jnp.int32, (1, SELECT_Q), 1)) // CHUNK
    nck = (_dsa_reach(q0, tq, length) + cw - 1) // cw
    w = wt_ref[0] * (H_IDX ** -0.5 * D_IDX ** -0.5)
    kf = float(ksel)

    def fill(c, _):
        for s in range(cw // LANES):
            k0 = pl.multiple_of(c * cw, cw) + s * LANES
            kk = ki_ref[0, pl.ds(k0, LANES), :]
            score = jnp.zeros((LANES, SELECT_Q), F32)
            for h in range(H_IDX):
                lg = lax.dot_general(kk, qi_ref[0, :, h * D_IDX:(h + 1) * D_IDX], NT_DIMS,
                                     preferred_element_type=F32)
                score = score + jnp.maximum(lg, 0.0) * w[h:h + 1, :]
            kpos = k0 + lax.broadcasted_iota(jnp.int32, (LANES, 1), 0)
            admissible = (kpos // CHUNK <= qchunk) & (kpos < length)
            sc[c, s * LANES:(s + 1) * LANES, :] = jnp.where(admissible, score + 0.0, -jnp.inf)
        return 0

    lax.fori_loop(0, nck, fill, 0)

    def count(t, strict=False):
        def body(c, tot):
            x = sc[c].reshape(cw // COUNT_ROWS, COUNT_ROWS, SELECT_Q)
            hit = (x > t) if strict else (x >= t)
            return tot + jnp.sum(jnp.where(hit, 1.0, 0.0), axis=0)
        tot = lax.fori_loop(0, nck, body, jnp.zeros((COUNT_ROWS, SELECT_Q), F32))
        return jnp.sum(tot, axis=0, keepdims=True)

    code0 = jnp.where(count(jnp.zeros((1, SELECT_Q), F32)) >= kf, 0, INT32_MIN).astype(jnp.int32)

    def bit_step(i, code):
        cand = code + lax.shift_left(jnp.int32(1), 30 - i)
        return jnp.where(count(_code_to_float(cand)) >= kf, cand, code)

    code = lax.fori_loop(0, 31, bit_step, code0)
    thr = _code_to_float(jnp.maximum(code, NEG_INF_CODE + 1))
    real = lax.broadcasted_iota(jnp.int32, (1, SELECT_Q), 1) < tq
    n_ge = jnp.where(real, count(thr), 0.0)
    n_gt = count(thr, strict=True)
    tied = jnp.max(n_ge) > kf

    @pl.when(jnp.logical_not(tied))
    def _():
        def emit(c, _):
            bias[c] = jnp.where(sc[c] >= thr, 0.0, -jnp.inf).T.astype(bias.dtype)
            return 0
        lax.fori_loop(0, nck, emit, 0)

    @pl.when(tied)
    def _():
        need = kf - n_gt
        row = lax.broadcasted_iota(jnp.int32, (LANES, LANES), 0)
        col = lax.broadcasted_iota(jnp.int32, (LANES, LANES), 1)
        earlier = jnp.where(col < row, 1.0, 0.0).astype(BF16)

        def emit(c, run):
            for s in range(cw // LANES):
                x = sc[c, s * LANES:(s + 1) * LANES, :]
                eq = jnp.where(x == thr, 1.0, 0.0)
                rank = run + jnp.dot(earlier, eq.astype(BF16), preferred_element_type=F32)
                sel = (x > thr) | ((x == thr) & (rank < need))
                bias[c, :, s * LANES:(s + 1) * LANES] = (
                    jnp.where(sel, 0.0, -jnp.inf).T.astype(bias.dtype))
                run = run + jnp.sum(eq, axis=0, keepdims=True)
            return run
        lax.fori_loop(0, nck, emit, jnp.zeros((1, SELECT_Q), F32))


def _dsa_attend(q_ref, k_ref, v_ref, bias, o_ref, qs, s_scr, p_scr, acc, m_scr, l_scr, a_scr,
                *, off, length, tq, tk):
    qb = pl.program_id(1)
    rep = H_DSA // KV_DSA
    nkb = (_dsa_reach(off + qb * tq, tq, length) + tk - 1) // tk
    ng = tk // LANES
    nchunk = tq // BF16_ROWS

    for g in range(KV_DSA):
        for r in range(rep):
            hd = g * rep + r
            qs[g, r * tq:(r + 1) * tq, :] = q_ref[0, :, hd * HEAD_DIM:(hd + 1) * HEAD_DIM]
    acc[...] = jnp.zeros_like(acc)
    m_scr[...] = jnp.full_like(m_scr, M_INIT)
    l_scr[...] = jnp.zeros_like(l_scr)

    def scores(kb, dst):
        k0 = pl.multiple_of(kb * tk, tk)
        for g in range(KV_DSA):
            sl = slice(g * HEAD_DIM, (g + 1) * HEAD_DIM)
            dst[g] = lax.dot_general(qs[g], k_ref[0, pl.ds(k0, tk), sl], NT_DIMS,
                                     preferred_element_type=F32)

    def consume(kb, src):
        k0 = pl.multiple_of(kb * tk, tk)
        for c in range(nchunk):
            bias_c = [bias[kb, c * BF16_ROWS:(c + 1) * BF16_ROWS,
                           gl * LANES:(gl + 1) * LANES].astype(F32) for gl in range(ng)]
            for g in range(KV_DSA):
                for r in range(rep):
                    rows = slice(r * tq + c * BF16_ROWS, r * tq + (c + 1) * BF16_ROWS)
                    parts = [src[g, rows, gl * LANES:(gl + 1) * LANES] + bias_c[gl]
                             for gl in range(ng)]
                    ps, m_new, l_new, alpha = _softmax_rows(parts, m_scr[g, rows, :],
                                                            l_scr[g, rows, :])
                    m_scr[g, rows, :] = m_new
                    l_scr[g, rows, :] = l_new
                    a_scr[g, rows, :] = alpha
                    for gl in range(ng):
                        p_scr[g, rows, gl * LANES:(gl + 1) * LANES] = ps[gl].astype(BF16)
        for g in range(KV_DSA):
            sl = slice(g * HEAD_DIM, (g + 1) * HEAD_DIM)
            acc[g] = a_scr[g] * acc[g] + jnp.dot(p_scr[g], v_ref[0, pl.ds(k0, tk), sl],
                                                 preferred_element_type=F32)

    def block(kb, _):
        scores(kb, s_scr)
        consume(kb, s_scr)
        return 0

    lax.fori_loop(0, nkb, block, 0)
    for g in range(KV_DSA):
        o = acc[g] / jnp.sum(l_scr[g], axis=-1, keepdims=True)
        for r in range(rep):
            hd = g * rep + r
            o_ref[0, :, hd * HEAD_DIM:(hd + 1) * HEAD_DIM] = (
                o[r * tq:(r + 1) * tq]).astype(o_ref.dtype)


def _dsa_kernel(qi_ref, wt_ref, ki_ref, q_ref, k_ref, v_ref, o_ref, sc, bias, qs, s_scr, p_scr, acc,
                m_scr, l_scr, a_scr, *, off, length, tq, tk, ksel):
    _dsa_select(qi_ref, wt_ref, ki_ref, bias, sc, off=off, length=length, tq=tq, ksel=ksel, cw=tk)
    _dsa_attend(q_ref, k_ref, v_ref, bias, o_ref, qs, s_scr, p_scr, acc, m_scr, l_scr, a_scr,
                off=off, length=length, tq=tq, tk=tk)


def dsa_sparse_attention(qi, w_idx, ki, q, k_arr, v_arr, t, length, tq, tk):
    assert tq == SELECT_Q or (t == tq and tq < SELECT_Q)
    b, lp, _ = ki.shape
    t_pad = _round_up(t, SELECT_Q)
    qi = jnp.pad(qi, ((0, 0), (0, t_pad - t), (0, 0)))
    w_t = jnp.pad(jnp.swapaxes(w_idx, 1, 2), ((0, 0), (0, 0), (0, t_pad - t)))
    wq = H_DSA * HEAD_DIM
    wk = KV_DSA * HEAD_DIM
    rep = H_DSA // KV_DSA
    nch = lp // tk
    per_q = lambda w: pl.BlockSpec((1, tq, w), lambda bi, qb: (bi, qb, 0))
    per_b = lambda w: pl.BlockSpec((1, lp, w), lambda bi, qb: (bi, 0, 0))
    return pl.pallas_call(
        functools.partial(_dsa_kernel, off=length - t, length=length, tq=tq, tk=tk,
                          ksel=min(TOPK_MAX, length // 4)),
        grid=(b, t // tq),
        in_specs=[pl.BlockSpec((1, SELECT_Q, H_IDX * D_IDX), lambda bi, qb: (bi, qb, 0)),
                  pl.BlockSpec((1, H_IDX, SELECT_Q), lambda bi, qb: (bi, 0, qb)),
                  per_b(D_IDX), per_q(wq), per_b(wk), per_b(wk)],
        out_specs=per_q(wq),
        out_shape=jax.ShapeDtypeStruct((b, t, wq), BF16),
        scratch_shapes=[pltpu.VMEM((nch, tk, SELECT_Q), F32),
                        pltpu.VMEM((nch, SELECT_Q, tk), BF16),
                        pltpu.VMEM((KV_DSA, rep * tq, HEAD_DIM), BF16),
                        pltpu.VMEM((KV_DSA, rep * tq, tk), F32),
                        pltpu.VMEM((KV_DSA, rep * tq, tk), BF16),
                        pltpu.VMEM((KV_DSA, rep * tq, HEAD_DIM), F32),
                        pltpu.VMEM((KV_DSA, rep * tq, LANES), F32),
                        pltpu.VMEM((KV_DSA, rep * tq, LANES), F32),
                        pltpu.VMEM((KV_DSA, rep * tq, LANES), F32)],
        compiler_params=_cparams(("parallel", "arbitrary")),
    )(qi, w_t, ki, q, k_arr, v_arr)


def _with_past(past, new, lp):
    b = new.shape[0]
    parts = [] if past is None else [past.reshape(b, past.shape[1], -1).astype(BF16)]
    parts.append(new.astype(BF16))
    rows = sum(p.shape[1] for p in parts)
    if rows < lp:
        parts.append(jnp.zeros((b, lp - rows, new.shape[2]), BF16))
    return parts[0] if len(parts) == 1 else jnp.concatenate(parts, axis=1)


def _mixer_ab(x2, b, t, past, g, w_in, b_forget):
    wsb = H_SB * HEAD_DIM
    wfx = H_FOX * HEAD_DIM
    qscale = LOG2E * HEAD_DIM ** -0.5
    c_fx = 3 * wsb
    c_gate = c_fx + 3 * wfx
    plan = [Segment(0, wsb, bf16_scale=qscale),
            Segment(wsb, wsb, f32=True, bf16_scale=1.0, heads_as_rows=True),
            Segment(2 * wsb, wsb, f32=True, bf16_scale=1.0, heads_as_rows=True),
            Segment(c_fx, wfx, bf16_scale=qscale),
            Segment(c_fx + wfx, wfx, f32=True, bf16_scale=1.0, heads_as_rows=True),
            Segment(c_fx + 2 * wfx, wfx, f32=True, bf16_scale=1.0, heads_as_rows=True),
            Segment(c_gate, LANES, f32=True)]
    w_pad = jnp.pad(w_in, ((0, 0), (0, c_gate + LANES - w_in.shape[1]))).astype(BF16)
    (q_sb, k_sb, k_sb_b, v_sb, v_sb_b, q_fx, k_fx, k_fx_b, v_fx, v_fx_b, gate) = norm_project(
        x2, g, w_pad, plan)
    rows = tuple(a.reshape(b, t, -1, HEAD_DIM) for a in (k_sb, v_sb, k_fx, v_fx))
    f_logit = gate.reshape(b, t, LANES)[:, :, :H_FOX]

    p = 0 if past is None else past[0].shape[1]
    length = p + t
    lp = _round_up(length, KEY_PAD)
    new_t = jnp.swapaxes(f_logit, 1, 2)
    parts = [new_t] if past is None else [jnp.swapaxes(past[4].astype(F32), 1, 2), new_t]
    if length < lp:
        parts.append(jnp.zeros((b, H_FOX, lp - length), F32))
    pre = (parts[0] if len(parts) == 1 else jnp.concatenate(parts, axis=2)).reshape(b * H_FOX, lp)
    b_col = jnp.tile(b_forget.astype(F32), b).reshape(b * H_FOX, 1)
    logf_t, *cum_parts = logf_cumsum(pre, b_col, p, length)
    logf = jnp.swapaxes(logf_t.reshape(b, H_FOX, lp)[:, :, p:length], 1, 2)

    f_parts = jnp.stack([jnp.swapaxes(c.reshape(b, H_FOX, lp), 1, 2) for c in cum_parts],
                        axis=-1)
    q_bias = _fox_bias_cols(f_parts[:, p:length], 0, 1.0)
    k_bias = _fox_bias_cols(f_parts, f_parts.shape[-1], -1.0)

    as3 = lambda a: a.reshape(b, t, -1)
    ks, vs, kf, vf = (_with_past(None if past is None else past[i], as3(a), lp)
                      for i, a in enumerate((k_sb_b, v_sb_b, k_fx_b, v_fx_b)))
    tq = min(256, t)
    o_sb = sb_attention(as3(q_sb), ks, 0, vs, 0, t, length, tq, 256)
    o_fx = fox_attention(as3(q_fx), q_bias, kf, 0, k_bias, vf, 0, t, length, tq, 512)
    return [o_sb.reshape(b * t, wsb), o_fx.reshape(b * t, wfx)], rows + (logf,)


def _mixer_dsa(x2, b, t, past, g, w_in):
    n = b * t
    wq = H_DSA * HEAD_DIM
    wk = KV_DSA * HEAD_DIM
    wi = H_IDX * D_IDX
    p = 0 if past is None else past[0].shape[1]
    length = p + t
    lp = _round_up(length, KEY_PAD)
    pos = p + jnp.arange(t)
    tab_rows = max(t, min(256, n))
    tables = [_rope_tables(pos, HEAD_DIM, tab_rows),
              _rope_tables(pos, D_IDX, tab_rows),
              _rope_tables(pos, D_IDX, tab_rows, live_lanes=D_IDX)]
    c_k, c_v, c_qi, c_ki = wq, wq + wk, wq + 2 * wk, wq + 2 * wk + wi
    plan = [Segment(0, wq, rope=0, half=HEAD_DIM // 2, bf16_scale=LOG2E * HEAD_DIM ** -0.5),
            Segment(c_k, wk, rope=0, half=HEAD_DIM // 2, f32=True, bf16_scale=1.0,
                    heads_as_rows=True),
            Segment(c_v, wk, f32=True, bf16_scale=1.0, heads_as_rows=True),
            Segment(c_qi, wi, rope=1, half=D_IDX // 2, bf16_scale=1.0),
            Segment(c_ki, LANES, rope=2, half=D_IDX // 2, f32=True, bf16_scale=1.0)]
    w_pad = jnp.pad(w_in, ((0, 0), (0, c_ki + LANES - w_in.shape[1]))).astype(BF16)
    q_b, k_f, k_b, v_f, v_b, qi_b, kw_f, kw_b = norm_project(x2, g, w_pad, plan, tables)
    as3 = lambda a: a.reshape(b, t, -1)
    k_rows = k_f.reshape(b, t, KV_DSA, HEAD_DIM)
    v_rows = v_f.reshape(b, t, KV_DSA, HEAD_DIM)
    ki_rows = as3(kw_f)[:, :, :D_IDX]
    w_idx = as3(kw_f)[:, :, D_IDX:D_IDX + H_IDX]

    tk = KEY_PAD
    ki_all = _with_past(None if past is None else past[2], as3(kw_b)[:, :, :D_IDX], lp)
    k_all = _with_past(None if past is None else past[0], as3(k_b), lp)
    v_all = _with_past(None if past is None else past[1], as3(v_b), lp)
    o = dsa_sparse_attention(as3(qi_b), w_idx, ki_all, as3(q_b), k_all, v_all, t, length,
                             min(128, t), tk)
    return [o.reshape(n, wq)], (k_rows, v_rows, ki_rows)


def _trunk(x, past_ab, past_dsa, norm_mix, norm_ffn, norm_final, w_in_ab, b_forget, w_out_ab,
           w_in_dsa, w_out_dsa, moe_w_group, moe_w_router, moe_w_gate, moe_w_up, moe_w_down):
    b, t, d = x.shape
    x2 = x.reshape(b * t, d)
    mix, rows_ab = _mixer_ab(x2, b, t, past_ab, norm_mix[0], w_in_ab, b_forget)
    x2 = moe_block(x2, mix, w_out_ab.astype(BF16), norm_ffn[0], moe_w_group[0], moe_w_router[0],
                   moe_w_gate[0], moe_w_up[0], moe_w_down[0])
    mix, rows_dsa = _mixer_dsa(x2, b, t, past_dsa, norm_mix[1], w_in_dsa)
    y = moe_block(x2, mix, w_out_dsa.astype(BF16), norm_ffn[1], moe_w_group[1], moe_w_router[1],
                  moe_w_gate[1], moe_w_up[1], moe_w_down[1], g_final=norm_final)
    return y.reshape(b, t, d), rows_ab, rows_dsa


def kernel(x_prompt, x_sample, cache_sb_k, cache_sb_v, cache_fox_k, cache_fox_v, cache_fox_logf,
           cache_dsa_k, cache_dsa_v, cache_dsa_idx_k, norm_mix, norm_ffn, norm_final,
           w_in_ab, b_forget, w_out_ab, w_in_dsa, w_out_dsa,
           moe_w_group, moe_w_router, moe_w_gate, moe_w_up, moe_w_down):
    weights = (norm_mix, norm_ffn, norm_final, w_in_ab, b_forget, w_out_ab, w_in_dsa, w_out_dsa,
               moe_w_group, moe_w_router, moe_w_gate, moe_w_up, moe_w_down)
    y_p, ab_p, dsa_p = _trunk(x_prompt, None, None, *weights)
    y_s, ab_s, dsa_s = _trunk(
        x_sample, (cache_sb_k, cache_sb_v, cache_fox_k, cache_fox_v, cache_fox_logf),
        (cache_dsa_k, cache_dsa_v, cache_dsa_idx_k), *weights)
    return (y_p, y_s) + ab_p + dsa_p + ab_s + dsa_s
```

```python
import functools
from typing import NamedTuple, Optional

import jax
import jax.numpy as jnp
import numpy as np
from jax import lax
from jax.experimental import pallas as pl
from jax.experimental.pallas import tpu as pltpu

CHUNK = 64
HEAD_DIM = 128
H_SB = 4
H_FOX = 4
H_DSA = 8
KV_DSA = 2
H_IDX = 4
D_IDX = 64
TOPK_MAX = 256
ROPE_THETA = 10000.0
EPS = 1e-6

LANES = 128
SUBLANES = 8
BF16_ROWS = 16

KEY_PAD = 512
LOG2E = 1.4426950408889634
M_INIT = -1e38
SB_DEAD_LOG2 = -150.0
INT32_MIN = -(2 ** 31)
SELECT_Q = 128
COUNT_ROWS = 64
FINE_BITS = 17
NEG_INF_CODE = int(np.int32(np.uint32(0xFF800000) ^ np.uint32(0x7FFFFFFF)))

F32 = jnp.float32
BF16 = jnp.bfloat16
NT_DIMS = (((1,), (1,)), ((), ()))


def _round_up(a, b):
    return (a + b - 1) // b * b


def _cparams(semantics, vmem_mib=48):
    return pltpu.CompilerParams(dimension_semantics=semantics,
                                vmem_limit_bytes=vmem_mib * 1024 * 1024)


def _rms(x, g):
    return x * lax.rsqrt(jnp.mean(x * x, axis=-1, keepdims=True) + EPS) * g


def _log_sigmoid(z):
    return jnp.minimum(z, 0.0) - jnp.log1p(jnp.exp(-jnp.abs(z)))


def _split3(x):
    hi = x.astype(BF16)
    r1 = x - hi.astype(F32)
    mid = r1.astype(BF16)
    lo = (r1 - mid.astype(F32)).astype(BF16)
    return hi, mid, lo


class Segment(NamedTuple):
    start: int
    width: int
    rope: Optional[int] = None
    half: int = 0
    f32: bool = False
    bf16_scale: Optional[float] = None
    heads_as_rows: bool = False


PROJ_TILE = 256


def _rotate(x, cos, sin, half):
    if 2 * half == LANES:
        partner = pltpu.roll(x, half, 1)
    else:
        lane = lax.broadcasted_iota(jnp.int32, x.shape, 1)
        partner = jnp.where(lane % (2 * half) < half,
                            pltpu.roll(x, LANES - half, 1), pltpu.roll(x, half, 1))
    return x * cos + partner * sin


def _proj_kernel(x_ref, g_ref, w_ref, *refs, plan, n_tab):
    tabs, outs = refs[:n_tab], list(refs[n_tab:])
    h = _rms(x_ref[...], g_ref[...]).astype(BF16)
    for seg in plan:
        o_f32 = outs.pop(0) if seg.f32 else None
        o_b16 = outs.pop(0) if seg.bf16_scale is not None else None
        tile = min(PROJ_TILE, seg.width)
        for j in range(seg.width // tile):
            acc = jnp.dot(h, w_ref[:, seg.start + j * tile:seg.start + (j + 1) * tile],
                          preferred_element_type=F32)
            if seg.rope is not None:
                cos, sin = tabs[2 * seg.rope][...], tabs[2 * seg.rope + 1][...]
                acc = jnp.concatenate(
                    [_rotate(acc[:, gi * LANES:(gi + 1) * LANES], cos, sin, seg.half)
                     for gi in range(tile // LANES)], axis=1)
            cols = slice(j * tile, (j + 1) * tile)
            if o_f32 is not None and seg.heads_as_rows:
                heads = seg.width // LANES
                for gi in range(tile // LANES):
                    head = j * (tile // LANES) + gi
                    o_f32[pl.ds(head, acc.shape[0], stride=heads), :] = (
                        acc[:, gi * LANES:(gi + 1) * LANES])
            elif o_f32 is not None:
                o_f32[:, cols] = acc
            if o_b16 is not None:
                o_b16[:, cols] = (acc * seg.bf16_scale).astype(BF16)


def norm_project(x, g, w, plan, tables=()):
    n, d = x.shape
    e = w.shape[1]
    tm = min(256, n)
    flat_tabs = [t for pair in tables for t in pair]
    tab_specs = []
    for tab in flat_tabs:
        period = tab.shape[0] // tm
        tab_specs.append(pl.BlockSpec((tm, LANES), lambda i, period=period: (i % period, 0)))
    out_specs, out_shape = [], []
    for seg in plan:
        if seg.f32:
            heads = seg.width // LANES if seg.heads_as_rows else 1
            out_specs.append(pl.BlockSpec((tm * heads, seg.width // heads), lambda i: (i, 0)))
            out_shape.append(jax.ShapeDtypeStruct((n * heads, seg.width // heads), F32))
        if seg.bf16_scale is not None:
            out_specs.append(pl.BlockSpec((tm, seg.width), lambda i: (i, 0)))
            out_shape.append(jax.ShapeDtypeStruct((n, seg.width), BF16))
    return pl.pallas_call(
        functools.partial(_proj_kernel, plan=tuple(plan), n_tab=len(flat_tabs)),
        grid=(n // tm,),
        in_specs=[pl.BlockSpec((tm, d), lambda i: (i, 0)),
                  pl.BlockSpec((1, d), lambda i: (0, 0)),
                  pl.BlockSpec((d, e), lambda i: (0, 0))] + tab_specs,
        out_specs=out_specs,
        out_shape=out_shape,
        compiler_params=_cparams(("parallel",)),
    )(x, g.reshape(1, d), w, *flat_tabs)


def _logf_cumsum_kernel(pre_ref, b_ref, logf_ref, hi_ref, mid_ref, lo_ref, carry,
                        *, past, length, cw):
    j = pl.program_id(0)

    @pl.when(j == 0)
    def _():
        carry[...] = jnp.zeros_like(carry)

    pre = pre_ref[...]
    pos = j * cw + lax.broadcasted_iota(jnp.int32, pre.shape, 1)
    x = jnp.where(pos >= past, _log_sigmoid(pre + b_ref[...]), pre)
    x = jnp.where(pos < length, x, 0.0)
    logf_ref[...] = x
    row = lax.broadcasted_iota(jnp.int32, (cw, cw), 0)
    col = lax.broadcasted_iota(jnp.int32, (cw, cw), 1)
    upper = jnp.where(row <= col, 1.0, 0.0).astype(BF16)
    hi, mid, lo = _split3(x)
    cs = (jnp.dot(hi, upper, preferred_element_type=F32)
          + jnp.dot(mid, upper, preferred_element_type=F32)
          + jnp.dot(lo, upper, preferred_element_type=F32))
    cum = cs + carry[:, 0:1]
    carry[...] = jnp.broadcast_to(cum[:, cw - 1:cw], carry.shape)
    hi_ref[...], mid_ref[...], lo_ref[...] = _split3(cum * LOG2E)


def logf_cumsum(pre, b_col, past, length):
    r, lp = pre.shape
    cw = KEY_PAD
    blk = pl.BlockSpec((r, cw), lambda j: (0, j))
    return pl.pallas_call(
        functools.partial(_logf_cumsum_kernel, past=past, length=length, cw=cw),
        grid=(lp // cw,),
        in_specs=[blk, pl.BlockSpec((r, 1), lambda j: (0, 0))],
        out_specs=[blk, blk, blk, blk],
        out_shape=[jax.ShapeDtypeStruct((r, lp), F32)] + [jax.ShapeDtypeStruct((r, lp), BF16)] * 3,
        scratch_shapes=[pltpu.VMEM((r, LANES), F32)],
        compiler_params=_cparams(("arbitrary",)),
    )(pre, b_col)


def _softmax_rows(s_parts, m_old, l_old):
    mx = s_parts[0]
    for sp in s_parts[1:]:
        mx = jnp.maximum(mx, sp)
    m_new = jnp.maximum(m_old, jnp.max(mx, axis=-1, keepdims=True))
    alpha = jnp.exp2(m_old - m_new)
    ps = [jnp.exp2(sp - m_new) for sp in s_parts]
    tot = ps[0]
    for p in ps[1:]:
        tot = tot + p
    return ps, m_new, alpha * l_old + tot, alpha


def _sb_kernel(q_ref, k_ref, v_ref, o_ref, s_scr, b_scr, hi_scr, lo_scr, p_scr, tri, acc, carry,
               *, off, tq, tk, heads):
    qi = pl.program_id(1)
    q0 = off + qi * tq
    kb_last = (q0 + tq - 2) // tk
    ng = tk // LANES
    nchunk = tq // BF16_ROWS

    acc[...] = jnp.zeros_like(acc)
    carry[...] = jnp.zeros_like(carry)
    row = lax.broadcasted_iota(jnp.int32, (tk, tk), 0)
    col = lax.broadcasted_iota(jnp.int32, (tk, tk), 1)
    tri[...] = jnp.where(row > col, 1.0, 0.0).astype(BF16)

    def block(state):
        kb, _ = state
        k0 = pl.multiple_of(kb * tk, tk)
        kpos = k0 + lax.broadcasted_iota(jnp.int32, (1, LANES), 1)
        for h in range(heads):
            sl = slice(h * HEAD_DIM, (h + 1) * HEAD_DIM)
            s_scr[h] = lax.dot_general(q_ref[0, :, sl], k_ref[0, pl.ds(k0, tk), sl], NT_DIMS,
                                       preferred_element_type=F32)
        for h in range(heads):
            for c in range(nchunk):
                rows = slice(c * BF16_ROWS, (c + 1) * BF16_ROWS)
                qpos = q0 + c * BF16_ROWS + lax.broadcasted_iota(jnp.int32, (BF16_ROWS, 1), 0)
                for g in range(ng):
                    ls = slice(g * LANES, (g + 1) * LANES)
                    z = s_scr[h, rows, ls]
                    lsig = jnp.minimum(z, 0.0) - jnp.log2(1.0 + jnp.exp2(-jnp.abs(z)))
                    l1m = jnp.where(kpos + g * LANES < qpos, lsig - z, 0.0)
                    hi = l1m.astype(BF16)
                    hi_scr[h, rows, ls] = hi
                    lo_scr[h, rows, ls] = (l1m - hi.astype(F32)).astype(BF16)
                    s_scr[h, rows, ls] = lsig
        for h in range(heads):
            b_scr[h] = (jnp.dot(hi_scr[h], tri[...], preferred_element_type=F32)
                        + jnp.dot(lo_scr[h], tri[...], preferred_element_type=F32))
        for h in range(heads):
            for c in range(nchunk):
                rows = slice(c * BF16_ROWS, (c + 1) * BF16_ROWS)
                qpos = q0 + c * BF16_ROWS + lax.broadcasted_iota(jnp.int32, (BF16_ROWS, 1), 0)
                run = carry[h, rows, :]
                for g in range(ng):
                    ls = slice(g * LANES, (g + 1) * LANES)
                    a = jnp.exp2(s_scr[h, rows, ls] + b_scr[h, rows, ls] + run)
                    a = jnp.where(kpos + g * LANES < qpos, a, 0.0)
                    p_scr[h, rows, ls] = a.astype(BF16)
                first = (b_scr[h, rows, 0:1] + hi_scr[h, rows, 0:1].astype(F32)
                         + lo_scr[h, rows, 0:1].astype(F32))
                carry[h, rows, :] = run + first
        for h in range(heads):
            sl = slice(h * HEAD_DIM, (h + 1) * HEAD_DIM)
            acc[h] += jnp.dot(p_scr[h], v_ref[0, pl.ds(k0, tk), sl], preferred_element_type=F32)
        return kb - 1, jnp.max(carry[...]) > SB_DEAD_LOG2

    lax.while_loop(lambda st: jnp.logical_and(st[0] >= 0, st[1]), block,
                   (kb_last, jnp.bool_(True)))
    for h in range(heads):
        o_ref[0, :, h * HEAD_DIM:(h + 1) * HEAD_DIM] = acc[h].astype(o_ref.dtype)


def sb_attention(q, k_arr, kc, v_arr, vc, t, length, tq, tk):
    b = q.shape[0]
    lp = k_arr.shape[1]
    w = H_SB * HEAD_DIM
    return pl.pallas_call(
        functools.partial(_sb_kernel, off=length - t, tq=tq, tk=tk, heads=H_SB),
        grid=(b, t // tq),
        in_specs=[pl.BlockSpec((1, tq, w), lambda bi, qi: (bi, qi, 0)),
                  pl.BlockSpec((1, lp, w), lambda bi, qi: (bi, 0, kc)),
                  pl.BlockSpec((1, lp, w), lambda bi, qi: (bi, 0, vc))],
        out_specs=pl.BlockSpec((1, tq, w), lambda bi, qi: (bi, qi, 0)),
        out_shape=jax.ShapeDtypeStruct((b, t, w), BF16),
        scratch_shapes=[pltpu.VMEM((H_SB, tq, tk), F32),
                        pltpu.VMEM((H_SB, tq, tk), F32),
                        pltpu.VMEM((H_SB, tq, tk), BF16),
                        pltpu.VMEM((H_SB, tq, tk), BF16),
                        pltpu.VMEM((H_SB, tq, tk), BF16),
                        pltpu.VMEM((tk, tk), BF16),
                        pltpu.VMEM((H_SB, tq, HEAD_DIM), F32),
                        pltpu.VMEM((H_SB, tq, LANES), F32)],
        compiler_params=_cparams(("parallel", "arbitrary")),
    )(q, k_arr, v_arr)


def _fox_kernel(q_ref, qb_ref, k_ref, kb_ref, v_ref, o_ref, s_scr, p_scr, acc, m_scr, l_scr, a_scr,
                *, off, tq, tk, heads):
    qi = pl.program_id(1)
    q0 = off + qi * tq
    kb_last = (q0 + tq - 1) // tk
    ng = tk // LANES
    nchunk = tq // BF16_ROWS

    acc[...] = jnp.zeros_like(acc)
    m_scr[...] = jnp.full_like(m_scr, M_INIT)
    l_scr[...] = jnp.zeros_like(l_scr)

    def block(kb, _, masked):
        k0 = pl.multiple_of(kb * tk, tk)
        kpos = k0 + lax.broadcasted_iota(jnp.int32, (1, LANES), 1)
        for h in range(heads):
            sl = slice(h * HEAD_DIM, (h + 1) * HEAD_DIM)
            q_aug = jnp.concatenate([q_ref[0, :, sl], qb_ref[0, :, sl]], axis=1)
            k_aug = jnp.concatenate([k_ref[0, pl.ds(k0, tk), sl], kb_ref[0, pl.ds(k0, tk), sl]],
                                    axis=1)
            s_scr[h] = lax.dot_general(q_aug, k_aug, NT_DIMS, preferred_element_type=F32)
        for h in range(heads):
            for c in range(nchunk):
                rows = slice(c * BF16_ROWS, (c + 1) * BF16_ROWS)
                parts = [s_scr[h, rows, g * LANES:(g + 1) * LANES] for g in range(ng)]
                if masked:
                    qpos = q0 + c * BF16_ROWS + lax.broadcasted_iota(jnp.int32, (BF16_ROWS, 1), 0)
                    parts = [jnp.where(kpos + g * LANES <= qpos, parts[g], -jnp.inf)
                             for g in range(ng)]
                ps, m_new, l_new, alpha = _softmax_rows(parts, m_scr[h, rows, :], l_scr[h, rows, :])
                m_scr[h, rows, :] = m_new
                l_scr[h, rows, :] = l_new
                a_scr[h, rows, :] = alpha
                for g in range(ng):
                    p_scr[h, rows, g * LANES:(g + 1) * LANES] = ps[g].astype(BF16)
        for h in range(heads):
            sl = slice(h * HEAD_DIM, (h + 1) * HEAD_DIM)
            acc[h] = a_scr[h] * acc[h] + jnp.dot(p_scr[h], v_ref[0, pl.ds(k0, tk), sl],
                                                 preferred_element_type=F32)
        return 0

    n_plain = jnp.minimum((q0 + 1) // tk, kb_last + 1)
    lax.fori_loop(0, n_plain, functools.partial(block, masked=False), 0)
    lax.fori_loop(n_plain, kb_last + 1, functools.partial(block, masked=True), 0)
    for h in range(heads):
        l = jnp.sum(l_scr[h], axis=-1, keepdims=True)
        o_ref[0, :, h * HEAD_DIM:(h + 1) * HEAD_DIM] = (acc[h] / l).astype(o_ref.dtype)


def _fox_bias_cols(parts, first, sign):
    b, r, h, n = parts.shape
    place = np.zeros((h, n + 1, h, HEAD_DIM), np.float32)
    for hd in range(h):
        for j in range(n):
            place[hd, j, hd, first + j] = sign
            place[hd, n, hd, (first + n + j) % (2 * n)] = 1.0
    src = jnp.concatenate([parts, jnp.ones((b, r, h, 1), BF16)], axis=-1)
    out = jnp.einsum('brk,kc->brc', src.reshape(b, r, h * (n + 1)),
                     jnp.asarray(place.reshape(h * (n + 1), h * HEAD_DIM), BF16),
                     preferred_element_type=F32)
    return out.astype(BF16)


def fox_attention(q, q_bias, k_arr, kc, k_bias, v_arr, vc, t, length, tq, tk):
    b = q.shape[0]
    lp = k_arr.shape[1]
    w = H_FOX * HEAD_DIM
    return pl.pallas_call(
        functools.partial(_fox_kernel, off=length - t, tq=tq, tk=tk, heads=H_FOX),
        grid=(b, t // tq),
        in_specs=[pl.BlockSpec((1, tq, w), lambda bi, qi: (bi, qi, 0)),
                  pl.BlockSpec((1, tq, w), lambda bi, qi: (bi, qi, 0)),
                  pl.BlockSpec((1, lp, w), lambda bi, qi: (bi, 0, kc)),
                  pl.BlockSpec((1, lp, w), lambda bi, qi: (bi, 0, 0)),
                  pl.BlockSpec((1, lp, w), lambda bi, qi: (bi, 0, vc))],
        out_specs=pl.BlockSpec((1, tq, w), lambda bi, qi: (bi, qi, 0)),
        out_shape=jax.ShapeDtypeStruct((b, t, w), BF16),
        scratch_shapes=[pltpu.VMEM((H_FOX, tq, tk), F32),
                        pltpu.VMEM((H_FOX, tq, tk), BF16),
                        pltpu.VMEM((H_FOX, tq, HEAD_DIM), F32),
                        pltpu.VMEM((H_FOX, tq, LANES), F32),
                        pltpu.VMEM((H_FOX, tq, LANES), F32),
                        pltpu.VMEM((H_FOX, tq, LANES), F32)],
        compiler_params=_cparams(("parallel", "arbitrary")),
    )(q, q_bias, k_arr, k_bias, v_arr)


def _moe_kernel(*refs, n_mix, n_groups, per_group, final_norm):
    x_ref, mix_refs = refs[0], refs[1:1 + n_mix]
    wo_ref, g_ref, wr_ref, wg_ref, wu_ref, wd_ref, gf_ref, o_ref, hb, comb = refs[1 + n_mix:]
    e = pl.program_id(1)
    n_exp = n_groups * per_group
    lane = lax.broadcasted_iota(jnp.int32, comb.shape, 1)

    @pl.when(e == 0)
    def _():
        x = x_ref[...]
        k0 = 0
        for m_ref in mix_refs:
            k = m_ref.shape[1]
            x = x + jnp.dot(m_ref[...], wo_ref[k0:k0 + k, :], preferred_element_type=F32)
            k0 += k
        h = _rms(x, g_ref[...])
        hb[...] = h.astype(BF16)
        h_hi = hb[...]
        h_lo = (h - h_hi.astype(F32)).astype(BF16)
        w = wr_ref[...]
        w_hi = w.astype(BF16)
        w_lo = (w - w_hi.astype(F32)).astype(BF16)
        logits = (jnp.dot(h_hi, w_hi, preferred_element_type=F32)
                  + jnp.dot(h_hi, w_lo, preferred_element_type=F32)
                  + jnp.dot(h_lo, w_hi, preferred_element_type=F32))
        lane_f = lane.astype(F32)
        gl = jnp.where(lane < n_groups, logits, -jnp.inf)
        gmax = jnp.max(gl, axis=-1, keepdims=True)
        gsel = jnp.min(jnp.where(gl == gmax, lane_f, float(LANES)), axis=-1, keepdims=True)
        p_group = 1.0 / jnp.sum(jnp.where(lane < n_groups, jnp.exp(gl - gmax), 0.0),
                                axis=-1, keepdims=True)
        first = n_groups + gsel * per_group
        el = jnp.where((lane_f >= first) & (lane_f < first + per_group), logits, -jnp.inf)
        v1 = jnp.max(el, axis=-1, keepdims=True)
        i1 = jnp.min(jnp.where(el == v1, lane_f, float(LANES)), axis=-1, keepdims=True)
        el2 = jnp.where(lane_f == i1, -jnp.inf, el)
        v2 = jnp.max(el2, axis=-1, keepdims=True)
        i2 = jnp.min(jnp.where(el2 == v2, lane_f, float(LANES)), axis=-1, keepdims=True)
        ratio = jnp.exp(v2 - v1)
        gate1 = p_group / (1.0 + ratio)
        comb[...] = jnp.where(lane_f == i1, gate1, jnp.where(lane_f == i2, gate1 * ratio, 0.0))
        o_ref[...] = x

    c = jnp.sum(jnp.where(lane == e + n_groups, comb[...], 0.0), axis=-1, keepdims=True)
    hv = hb[...]
    gate = jnp.dot(hv, wg_ref[0], preferred_element_type=F32)
    up = jnp.dot(hv, wu_ref[0], preferred_element_type=F32)
    act = gate * (1.0 / (1.0 + jnp.exp(-gate))) * up * c
    o_ref[...] += jnp.dot(act.astype(BF16), wd_ref[0], preferred_element_type=F32)

    if final_norm:
        @pl.when(e == n_exp - 1)
        def _():
            o_ref[...] = _rms(o_ref[...], gf_ref[...])


def moe_block(x, mix, w_out, g, w_group, w_router, w_gate, w_up, w_down, g_final=None):
    n, d = x.shape
    n_groups, per_group = w_router.shape[1], w_router.shape[2]
    n_exp, _, f = w_gate.shape
    w_route = jnp.concatenate(
        [w_group, w_router.reshape(d, n_exp),
         jnp.zeros((d, LANES - n_groups - n_exp), F32)], axis=1)
    final_norm = g_final is not None
    gf = (g_final if final_norm else g).reshape(1, d)
    tm = min(1024, n)
    return pl.pallas_call(
        functools.partial(_moe_kernel, n_mix=len(mix), n_groups=n_groups, per_group=per_group,
                          final_norm=final_norm),
        grid=(n // tm, n_exp),
        in_specs=[pl.BlockSpec((tm, d), lambda i, e: (i, 0))]
        + [pl.BlockSpec((tm, m.shape[1]), lambda i, e: (i, 0)) for m in mix]
        + [pl.BlockSpec(w_out.shape, lambda i, e: (0, 0)),
                  pl.BlockSpec((1, d), lambda i, e: (0, 0)),
                  pl.BlockSpec((d, LANES), lambda i, e: (0, 0)),
                  pl.BlockSpec((1, d, f), lambda i, e: (e, 0, 0)),
                  pl.BlockSpec((1, d, f), lambda i, e: (e, 0, 0)),
                  pl.BlockSpec((1, f, d), lambda i, e: (e, 0, 0)),
                  pl.BlockSpec((1, d), lambda i, e: (0, 0))],
        out_specs=pl.BlockSpec((tm, d), lambda i, e: (i, 0)),
        out_shape=jax.ShapeDtypeStruct((n, d), F32),
        scratch_shapes=[pltpu.VMEM((tm, d), BF16), pltpu.VMEM((tm, LANES), F32)],
        compiler_params=_cparams(("parallel", "arbitrary")),
    )(x, *mix, w_out, g.reshape(1, d), w_route, w_gate.astype(BF16), w_up.astype(BF16),
      w_down.astype(BF16), gf)


def _rope_tables(pos, head_dim, rows, live_lanes=LANES):
    half = head_dim // 2
    inv_freq = ROPE_THETA ** (-jnp.arange(half, dtype=F32) / half)
    ang = pos.astype(F32)[:, None] * inv_freq[None, :]
    cos, sin = jnp.cos(ang), jnp.sin(ang)
    reps = LANES // head_dim
    cos = jnp.tile(jnp.concatenate([cos, cos], axis=1), (1, reps))
    sin = jnp.tile(jnp.concatenate([-sin, sin], axis=1), (1, reps))
    live = jnp.arange(LANES) < live_lanes
    cos = jnp.where(live, cos, 1.0)
    sin = jnp.where(live, sin, 0.0)
    reps_rows = max(rows // pos.shape[0], 1)
    return jnp.tile(cos, (reps_rows, 1)), jnp.tile(sin, (reps_rows, 1))


def _code_to_float(code):
    return pltpu.bitcast(code ^ ((code >> 31) & 0x7FFFFFFF), F32)


def _dsa_reach(q_first, tq, length):
    return jnp.minimum(((q_first + tq - 1) // CHUNK + 1) * CHUNK, length)


def _dsa_select(qi_ref, wt_ref, ki_ref, bias, sc, sr, *, off, length, tq, ksel, cw):
    qb = pl.program_id(1)
    q0 = off + qb * tq
    qchunk = (q0 + lax.broadcasted_iota(jnp.int32, (1, SELECT_Q), 1)) // CHUNK
    nck = (_dsa_reach(q0, tq, length) + cw - 1) // cw
    w = wt_ref[0] * (H_IDX ** -0.5 * D_IDX ** -0.5)
    kf = float(ksel)

    def fill(c, _):
        for s in range(cw // LANES):
            k0 = pl.multiple_of(c * cw, cw) + s * LANES
            kk = ki_ref[0, pl.ds(k0, LANES), :]
            score = jnp.zeros((LANES, SELECT_Q), F32)
            for h in range(H_IDX):
                lg = lax.dot_general(kk, qi_ref[0, :, h * D_IDX:(h + 1) * D_IDX], NT_DIMS,
                                     preferred_element_type=F32)
                score = score + jnp.maximum(lg, 0.0) * w[h:h + 1, :]
            kpos = k0 + lax.broadcasted_iota(jnp.int32, (LANES, 1), 0)
            admissible = (kpos // CHUNK <= qchunk) & (kpos < length)
            score = jnp.where(admissible, score + 0.0, -jnp.inf)
            sc[c, s * LANES:(s + 1) * LANES, :] = score
            sr[c, s * LANES:(s + 1) * LANES, :] = score.astype(BF16)
        return 0

    lax.fori_loop(0, nck, fill, 0)

    def count(t, strict=False):
        def body(c, tot):
            x = sc[c].reshape(cw // COUNT_ROWS, COUNT_ROWS, SELECT_Q)
            hit = (x > t) if strict else (x >= t)
            return tot + jnp.sum(jnp.where(hit, 1.0, 0.0), axis=0)
        tot = lax.fori_loop(0, nck, body, jnp.zeros((COUNT_ROWS, SELECT_Q), F32))
        return jnp.sum(tot, axis=0, keepdims=True)

    def count_rounded(t16):
        def body(c, tot):
            x = sr[c].reshape(cw // COUNT_ROWS, COUNT_ROWS, SELECT_Q)
            hit = jnp.where(x >= t16, jnp.ones((), BF16), jnp.zeros((), BF16))
            for i in range(cw // COUNT_ROWS):
                tot = tot + hit[i]
            return tot
        tot = lax.fori_loop(0, nck, body, jnp.zeros((COUNT_ROWS, SELECT_Q), BF16))
        return jnp.sum(tot.astype(F32), axis=0, keepdims=True)

    code_hi = jnp.where(count_rounded(jnp.zeros((1, SELECT_Q), BF16)) >= kf,
                        0, INT32_MIN).astype(jnp.int32)

    def coarse_step(i, code):
        cand = code + lax.shift_left(jnp.int32(1), 30 - i)
        ok = count_rounded(_code_to_float(cand).astype(BF16)) >= kf
        return jnp.where(ok, cand, code)

    code_hi = lax.fori_loop(0, 31 - FINE_BITS + 1, coarse_step, code_hi)
    slack = 2 ** (FINE_BITS - 2) + 1
    base = jnp.maximum(code_hi, INT32_MIN + slack) - slack

    def fine_step(i, code):
        cand = code + lax.shift_left(jnp.int32(1), FINE_BITS - 1 - i)
        return jnp.where(count(_code_to_float(cand)) >= kf, cand, code)

    code = lax.fori_loop(0, FINE_BITS, fine_step, base)
    thr = _code_to_float(jnp.maximum(code, NEG_INF_CODE + 1))
    real = lax.broadcasted_iota(jnp.int32, (1, SELECT_Q), 1) < tq
    n_ge = jnp.where(real, count(thr), 0.0)
    n_gt = count(thr, strict=True)
    tied = jnp.max(n_ge) > kf

    @pl.when(jnp.logical_not(tied))
    def _():
        def emit(c, _):
            bias[c] = jnp.where(sc[c] >= thr, 0.0, -jnp.inf).T.astype(bias.dtype)
            return 0
        lax.fori_loop(0, nck, emit, 0)

    @pl.when(tied)
    def _():
        need = kf - n_gt
        row = lax.broadcasted_iota(jnp.int32, (LANES, LANES), 0)
        col = lax.broadcasted_iota(jnp.int32, (LANES, LANES), 1)
        earlier = jnp.where(col < row, 1.0, 0.0).astype(BF16)

        def emit(c, run):
            for s in range(cw // LANES):
                x = sc[c, s * LANES:(s + 1) * LANES, :]
                eq = jnp.where(x == thr, 1.0, 0.0)
                rank = run + jnp.dot(earlier, eq.astype(BF16), preferred_element_type=F32)
                sel = (x > thr) | ((x == thr) & (rank < need))
                bias[c, :, s * LANES:(s + 1) * LANES] = (
                    jnp.where(sel, 0.0, -jnp.inf).T.astype(bias.dtype))
                run = run + jnp.sum(eq, axis=0, keepdims=True)
            return run
        lax.fori_loop(0, nck, emit, jnp.zeros((1, SELECT_Q), F32))


def _dsa_attend(q_ref, k_ref, v_ref, bias, o_ref, qs, s_scr, p_scr, acc, m_scr, l_scr, a_scr,
                *, off, length, tq, tk):
    qb = pl.program_id(1)
    rep = H_DSA // KV_DSA
    nkb = (_dsa_reach(off + qb * tq, tq, length) + tk - 1) // tk
    ng = tk // LANES
    nchunk = tq // BF16_ROWS

    for g in range(KV_DSA):
        for r in range(rep):
            hd = g * rep + r
            qs[g, r * tq:(r + 1) * tq, :] = q_ref[0, :, hd * HEAD_DIM:(hd + 1) * HEAD_DIM]
    acc[...] = jnp.zeros_like(acc)
    m_scr[...] = jnp.full_like(m_scr, M_INIT)
    l_scr[...] = jnp.zeros_like(l_scr)

    def scores(kb, dst):
        k0 = pl.multiple_of(kb * tk, tk)
        for g in range(KV_DSA):
            sl = slice(g * HEAD_DIM, (g + 1) * HEAD_DIM)
            dst[g] = lax.dot_general(qs[g], k_ref[0, pl.ds(k0, tk), sl], NT_DIMS,
                                     preferred_element_type=F32)

    def consume(kb, src):
        k0 = pl.multiple_of(kb * tk, tk)
        for c in range(nchunk):
            bias_c = [bias[kb, c * BF16_ROWS:(c + 1) * BF16_ROWS,
                           gl * LANES:(gl + 1) * LANES].astype(F32) for gl in range(ng)]
            for g in range(KV_DSA):
                for r in range(rep):
                    rows = slice(r * tq + c * BF16_ROWS, r * tq + (c + 1) * BF16_ROWS)
                    parts = [src[g, rows, gl * LANES:(gl + 1) * LANES] + bias_c[gl]
                             for gl in range(ng)]
                    ps, m_new, l_new, alpha = _softmax_rows(parts, m_scr[g, rows, :],
                                                            l_scr[g, rows, :])
                    m_scr[g, rows, :] = m_new
                    l_scr[g, rows, :] = l_new
                    a_scr[g, rows, :] = alpha
                    for gl in range(ng):
                        p_scr[g, rows, gl * LANES:(gl + 1) * LANES] = ps[gl].astype(BF16)
        for g in range(KV_DSA):
            sl = slice(g * HEAD_DIM, (g + 1) * HEAD_DIM)
            acc[g] = a_scr[g] * acc[g] + jnp.dot(p_scr[g], v_ref[0, pl.ds(k0, tk), sl],
                                                 preferred_element_type=F32)

    def block(kb, _):
        scores(kb, s_scr)
        consume(kb, s_scr)
        return 0

    lax.fori_loop(0, nkb, block, 0)
    for g in range(KV_DSA):
        o = acc[g] / jnp.sum(l_scr[g], axis=-1, keepdims=True)
        for r in range(rep):
            hd = g * rep + r
            o_ref[0, :, hd * HEAD_DIM:(hd + 1) * HEAD_DIM] = (
                o[r * tq:(r + 1) * tq]).astype(o_ref.dtype)


def _dsa_kernel(qi_ref, wt_ref, ki_ref, q_ref, k_ref, v_ref, o_ref, sc, sr, bias, qs, s_scr, p_scr,
                acc, m_scr, l_scr, a_scr, *, off, length, tq, tk, ksel):
    _dsa_select(qi_ref, wt_ref, ki_ref, bias, sc, sr, off=off, length=length, tq=tq, ksel=ksel,
                cw=tk)
    _dsa_attend(q_ref, k_ref, v_ref, bias, o_ref, qs, s_scr, p_scr, acc, m_scr, l_scr, a_scr,
                off=off, length=length, tq=tq, tk=tk)


def dsa_sparse_attention(qi, w_idx, ki, q, k_arr, v_arr, t, length, tq, tk):
    assert tq == SELECT_Q or (t == tq and tq < SELECT_Q)
    b, lp, _ = ki.shape
    t_pad = _round_up(t, SELECT_Q)
    qi = jnp.pad(qi, ((0, 0), (0, t_pad - t), (0, 0)))
    w_t = jnp.pad(jnp.swapaxes(w_idx, 1, 2), ((0, 0), (0, 0), (0, t_pad - t)))
    wq = H_DSA * HEAD_DIM
    wk = KV_DSA * HEAD_DIM
    rep = H_DSA // KV_DSA
    nch = lp // tk
    per_q = lambda w: pl.BlockSpec((1, tq, w), lambda bi, qb: (bi, qb, 0))
    per_b = lambda w: pl.BlockSpec((1, lp, w), lambda bi, qb: (bi, 0, 0))
    return pl.pallas_call(
        functools.partial(_dsa_kernel, off=length - t, length=length, tq=tq, tk=tk,
                          ksel=min(TOPK_MAX, length // 4)),
        grid=(b, t // tq),
        in_specs=[pl.BlockSpec((1, SELECT_Q, H_IDX * D_IDX), lambda bi, qb: (bi, qb, 0)),
                  pl.BlockSpec((1, H_IDX, SELECT_Q), lambda bi, qb: (bi, 0, qb)),
                  per_b(D_IDX), per_q(wq), per_b(wk), per_b(wk)],
        out_specs=per_q(wq),
        out_shape=jax.ShapeDtypeStruct((b, t, wq), BF16),
        scratch_shapes=[pltpu.VMEM((nch, tk, SELECT_Q), F32),
                        pltpu.VMEM((nch, tk, SELECT_Q), BF16),
                        pltpu.VMEM((nch, SELECT_Q, tk), BF16),
                        pltpu.VMEM((KV_DSA, rep * tq, HEAD_DIM), BF16),
                        pltpu.VMEM((KV_DSA, rep * tq, tk), F32),
                        pltpu.VMEM((KV_DSA, rep * tq, tk), BF16),
                        pltpu.VMEM((KV_DSA, rep * tq, HEAD_DIM), F32),
                        pltpu.VMEM((KV_DSA, rep * tq, LANES), F32),
                        pltpu.VMEM((KV_DSA, rep * tq, LANES), F32),
                        pltpu.VMEM((KV_DSA, rep * tq, LANES), F32)],
        compiler_params=_cparams(("parallel", "arbitrary")),
    )(qi, w_t, ki, q, k_arr, v_arr)


def _with_past(past, new, lp):
    b = new.shape[0]
    parts = [] if past is None else [past.reshape(b, past.shape[1], -1).astype(BF16)]
    parts.append(new.astype(BF16))
    rows = sum(p.shape[1] for p in parts)
    if rows < lp:
        parts.append(jnp.zeros((b, lp - rows, new.shape[2]), BF16))
    return parts[0] if len(parts) == 1 else jnp.concatenate(parts, axis=1)


def _mixer_ab(x2, b, t, past, g, w_in, b_forget):
    wsb = H_SB * HEAD_DIM
    wfx = H_FOX * HEAD_DIM
    qscale = LOG2E * HEAD_DIM ** -0.5
    c_fx = 3 * wsb
    c_gate = c_fx + 3 * wfx
    plan = [Segment(0, wsb, bf16_scale=qscale),
            Segment(wsb, wsb, f32=True, bf16_scale=1.0, heads_as_rows=True),
            Segment(2 * wsb, wsb, f32=True, bf16_scale=1.0, heads_as_rows=True),
            Segment(c_fx, wfx, bf16_scale=qscale),
            Segment(c_fx + wfx, wfx, f32=True, bf16_scale=1.0, heads_as_rows=True),
            Segment(c_fx + 2 * wfx, wfx, f32=True, bf16_scale=1.0, heads_as_rows=True),
            Segment(c_gate, LANES, f32=True)]
    w_pad = jnp.pad(w_in, ((0, 0), (0, c_gate + LANES - w_in.shape[1]))).astype(BF16)
    (q_sb, k_sb, k_sb_b, v_sb, v_sb_b, q_fx, k_fx, k_fx_b, v_fx, v_fx_b, gate) = norm_project(
        x2, g, w_pad, plan)
    rows = tuple(a.reshape(b, t, -1, HEAD_DIM) for a in (k_sb, v_sb, k_fx, v_fx))
    f_logit = gate.reshape(b, t, LANES)[:, :, :H_FOX]

    p = 0 if past is None else past[0].shape[1]
    length = p + t
    lp = _round_up(length, KEY_PAD)
    new_t = jnp.swapaxes(f_logit, 1, 2)
    parts = [new_t] if past is None else [jnp.swapaxes(past[4].astype(F32), 1, 2), new_t]
    if length < lp:
        parts.append(jnp.zeros((b, H_FOX, lp - length), F32))
    pre = (parts[0] if len(parts) == 1 else jnp.concatenate(parts, axis=2)).reshape(b * H_FOX, lp)
    b_col = jnp.tile(b_forget.astype(F32), b).reshape(b * H_FOX, 1)
    logf_t, *cum_parts = logf_cumsum(pre, b_col, p, length)
    logf = jnp.swapaxes(logf_t.reshape(b, H_FOX, lp)[:, :, p:length], 1, 2)

    f_parts = jnp.stack([jnp.swapaxes(c.reshape(b, H_FOX, lp), 1, 2) for c in cum_parts],
                        axis=-1)
    q_bias = _fox_bias_cols(f_parts[:, p:length], 0, 1.0)
    k_bias = _fox_bias_cols(f_parts, f_parts.shape[-1], -1.0)

    as3 = lambda a: a.reshape(b, t, -1)
    ks, vs, kf, vf = (_with_past(None if past is None else past[i], as3(a), lp)
                      for i, a in enumerate((k_sb_b, v_sb_b, k_fx_b, v_fx_b)))
    tq = min(256, t)
    o_sb = sb_attention(as3(q_sb), ks, 0, vs, 0, t, length, tq, 256)
    o_fx = fox_attention(as3(q_fx), q_bias, kf, 0, k_bias, vf, 0, t, length, tq, 512)
    return [o_sb.reshape(b * t, wsb), o_fx.reshape(b * t, wfx)], rows + (logf,)


def _mixer_dsa(x2, b, t, past, g, w_in):
    n = b * t
    wq = H_DSA * HEAD_DIM
    wk = KV_DSA * HEAD_DIM
    wi = H_IDX * D_IDX
    p = 0 if past is None else past[0].shape[1]
    length = p + t
    lp = _round_up(length, KEY_PAD)
    pos = p + jnp.arange(t)
    tab_rows = max(t, min(256, n))
    tables = [_rope_tables(pos, HEAD_DIM, tab_rows),
              _rope_tables(pos, D_IDX, tab_rows),
              _rope_tables(pos, D_IDX, tab_rows, live_lanes=D_IDX)]
    c_k, c_v, c_qi, c_ki = wq, wq + wk, wq + 2 * wk, wq + 2 * wk + wi
    plan = [Segment(0, wq, rope=0, half=HEAD_DIM // 2, bf16_scale=LOG2E * HEAD_DIM ** -0.5),
            Segment(c_k, wk, rope=0, half=HEAD_DIM // 2, f32=True, bf16_scale=1.0,
                    heads_as_rows=True),
            Segment(c_v, wk, f32=True, bf16_scale=1.0, heads_as_rows=True),
            Segment(c_qi, wi, rope=1, half=D_IDX // 2, bf16_scale=1.0),
            Segment(c_ki, LANES, rope=2, half=D_IDX // 2, f32=True, bf16_scale=1.0)]
    w_pad = jnp.pad(w_in, ((0, 0), (0, c_ki + LANES - w_in.shape[1]))).astype(BF16)
    q_b, k_f, k_b, v_f, v_b, qi_b, kw_f, kw_b = norm_project(x2, g, w_pad, plan, tables)
    as3 = lambda a: a.reshape(b, t, -1)
    k_rows = k_f.reshape(b, t, KV_DSA, HEAD_DIM)
    v_rows = v_f.reshape(b, t, KV_DSA, HEAD_DIM)
    ki_rows = as3(kw_f)[:, :, :D_IDX]
    w_idx = as3(kw_f)[:, :, D_IDX:D_IDX + H_IDX]

    tk = KEY_PAD
    ki_all = _with_past(None if past is None else past[2], as3(kw_b)[:, :, :D_IDX], lp)
    k_all = _with_past(None if past is None else past[0], as3(k_b), lp)
    v_all = _with_past(None if past is None else past[1], as3(v_b), lp)
    o = dsa_sparse_attention(as3(qi_b), w_idx, ki_all, as3(q_b), k_all, v_all, t, length,
                             min(128, t), tk)
    return [o.reshape(n, wq)], (k_rows, v_rows, ki_rows)


def _trunk(x, past_ab, past_dsa, norm_mix, norm_ffn, norm_final, w_in_ab, b_forget, w_out_ab,
           w_in_dsa, w_out_dsa, moe_w_group, moe_w_router, moe_w_gate, moe_w_up, moe_w_down):
    b, t, d = x.shape
    x2 = x.reshape(b * t, d)
    mix, rows_ab = _mixer_ab(x2, b, t, past_ab, norm_mix[0], w_in_ab, b_forget)
    x2 = moe_block(x2, mix, w_out_ab.astype(BF16), norm_ffn[0], moe_w_group[0], moe_w_router[0],
                   moe_w_gate[0], moe_w_up[0], moe_w_down[0])
    mix, rows_dsa = _mixer_dsa(x2, b, t, past_dsa, norm_mix[1], w_in_dsa)
    y = moe_block(x2, mix, w_out_dsa.astype(BF16), norm_ffn[1], moe_w_group[1], moe_w_router[1],
                  moe_w_gate[1], moe_w_up[1], moe_w_down[1], g_final=norm_final)
    return y.reshape(b, t, d), rows_ab, rows_dsa


def kernel(x_prompt, x_sample, cache_sb_k, cache_sb_v, cache_fox_k, cache_fox_v, cache_fox_logf,
           cache_dsa_k, cache_dsa_v, cache_dsa_idx_k, norm_mix, norm_ffn, norm_final,
           w_in_ab, b_forget, w_out_ab, w_in_dsa, w_out_dsa,
           moe_w_group, moe_w_router, moe_w_gate, moe_w_up, moe_w_down):
    weights = (norm_mix, norm_ffn, norm_final, w_in_ab, b_forget, w_out_ab, w_in_dsa, w_out_dsa,
               moe_w_group, moe_w_router, moe_w_gate, moe_w_up, moe_w_down)
    y_p, ab_p, dsa_p = _trunk(x_prompt, None, None, *weights)
    y_s, ab_s, dsa_s = _trunk(
        x_sample, (cache_sb_k, cache_sb_v, cache_fox_k, cache_fox_v, cache_fox_logf),
        (cache_dsa_k, cache_dsa_v, cache_dsa_idx_k), *weights)
    return (y_p, y_s) + ab_p + dsa_p + ab_s + dsa_s
```

```python
import functools
from typing import NamedTuple, Optional

import jax
import jax.numpy as jnp
import numpy as np
from jax import lax
from jax.experimental import pallas as pl
from jax.experimental.pallas import tpu as pltpu

CHUNK = 64
HEAD_DIM = 128
H_SB = 4
H_FOX = 4
H_DSA = 8
KV_DSA = 2
H_IDX = 4
D_IDX = 64
TOPK_MAX = 256
ROPE_THETA = 10000.0
EPS = 1e-6

LANES = 128
SUBLANES = 8
BF16_ROWS = 16

KEY_PAD = 512
LOG2E = 1.4426950408889634
M_INIT = -1e38
SB_DEAD_LOG2 = -150.0
INT32_MIN = -(2 ** 31)
SELECT_Q = 128
COUNT_ROWS = 64
NEG_INF_CODE = int(np.int32(np.uint32(0xFF800000) ^ np.uint32(0x7FFFFFFF)))

F32 = jnp.float32
BF16 = jnp.bfloat16
NT_DIMS = (((1,), (1,)), ((), ()))


def _round_up(a, b):
    return (a + b - 1) // b * b


def _cparams(semantics, vmem_mib=48):
    return pltpu.CompilerParams(dimension_semantics=semantics,
                                vmem_limit_bytes=vmem_mib * 1024 * 1024)


def _rms(x, g):
    return x * lax.rsqrt(jnp.mean(x * x, axis=-1, keepdims=True) + EPS) * g


def _log_sigmoid(z):
    return jnp.minimum(z, 0.0) - jnp.log1p(jnp.exp(-jnp.abs(z)))


def _split3(x):
    hi = x.astype(BF16)
    r1 = x - hi.astype(F32)
    mid = r1.astype(BF16)
    lo = (r1 - mid.astype(F32)).astype(BF16)
    return hi, mid, lo


class Segment(NamedTuple):
    start: int
    width: int
    rope: Optional[int] = None
    half: int = 0
    f32: bool = False
    bf16_scale: Optional[float] = None
    heads_as_rows: bool = False


PROJ_TILE = 256


def _rotate(x, cos, sin, half):
    if 2 * half == LANES:
        partner = pltpu.roll(x, half, 1)
    else:
        lane = lax.broadcasted_iota(jnp.int32, x.shape, 1)
        partner = jnp.where(lane % (2 * half) < half,
                            pltpu.roll(x, LANES - half, 1), pltpu.roll(x, half, 1))
    return x * cos + partner * sin


def _proj_kernel(x_ref, g_ref, w_ref, *refs, plan, n_tab):
    tabs, outs = refs[:n_tab], list(refs[n_tab:])
    h = _rms(x_ref[...], g_ref[...]).astype(BF16)
    for seg in plan:
        o_f32 = outs.pop(0) if seg.f32 else None
        o_b16 = outs.pop(0) if seg.bf16_scale is not None else None
        tile = min(PROJ_TILE, seg.width)
        for j in range(seg.width // tile):
            acc = jnp.dot(h, w_ref[:, seg.start + j * tile:seg.start + (j + 1) * tile],
                          preferred_element_type=F32)
            if seg.rope is not None:
                cos, sin = tabs[2 * seg.rope][...], tabs[2 * seg.rope + 1][...]
                acc = jnp.concatenate(
                    [_rotate(acc[:, gi * LANES:(gi + 1) * LANES], cos, sin, seg.half)
                     for gi in range(tile // LANES)], axis=1)
            cols = slice(j * tile, (j + 1) * tile)
            if o_f32 is not None and seg.heads_as_rows:
                heads = seg.width // LANES
                for gi in range(tile // LANES):
                    head = j * (tile // LANES) + gi
                    o_f32[pl.ds(head, acc.shape[0], stride=heads), :] = (
                        acc[:, gi * LANES:(gi + 1) * LANES])
            elif o_f32 is not None:
                o_f32[:, cols] = acc
            if o_b16 is not None:
                o_b16[:, cols] = (acc * seg.bf16_scale).astype(BF16)


def norm_project(x, g, w, plan, tables=()):
    n, d = x.shape
    e = w.shape[1]
    tm = min(256, n)
    flat_tabs = [t for pair in tables for t in pair]
    tab_specs = []
    for tab in flat_tabs:
        period = tab.shape[0] // tm
        tab_specs.append(pl.BlockSpec((tm, LANES), lambda i, period=period: (i % period, 0)))
    out_specs, out_shape = [], []
    for seg in plan:
        if seg.f32:
            heads = seg.width // LANES if seg.heads_as_rows else 1
            out_specs.append(pl.BlockSpec((tm * heads, seg.width // heads), lambda i: (i, 0)))
            out_shape.append(jax.ShapeDtypeStruct((n * heads, seg.width // heads), F32))
        if seg.bf16_scale is not None:
            out_specs.append(pl.BlockSpec((tm, seg.width), lambda i: (i, 0)))
            out_shape.append(jax.ShapeDtypeStruct((n, seg.width), BF16))
    return pl.pallas_call(
        functools.partial(_proj_kernel, plan=tuple(plan), n_tab=len(flat_tabs)),
        grid=(n // tm,),
        in_specs=[pl.BlockSpec((tm, d), lambda i: (i, 0)),
                  pl.BlockSpec((1, d), lambda i: (0, 0)),
                  pl.BlockSpec((d, e), lambda i: (0, 0))] + tab_specs,
        out_specs=out_specs,
        out_shape=out_shape,
        compiler_params=_cparams(("parallel",)),
    )(x, g.reshape(1, d), w, *flat_tabs)


def _logf_cumsum_kernel(pre_ref, b_ref, logf_ref, hi_ref, mid_ref, lo_ref, carry,
                        *, past, length, cw):
    j = pl.program_id(0)

    @pl.when(j == 0)
    def _():
        carry[...] = jnp.zeros_like(carry)

    pre = pre_ref[...]
    pos = j * cw + lax.broadcasted_iota(jnp.int32, pre.shape, 1)
    x = jnp.where(pos >= past, _log_sigmoid(pre + b_ref[...]), pre)
    x = jnp.where(pos < length, x, 0.0)
    logf_ref[...] = x
    row = lax.broadcasted_iota(jnp.int32, (cw, cw), 0)
    col = lax.broadcasted_iota(jnp.int32, (cw, cw), 1)
    upper = jnp.where(row <= col, 1.0, 0.0).astype(BF16)
    hi, mid, lo = _split3(x)
    cs = (jnp.dot(hi, upper, preferred_element_type=F32)
          + jnp.dot(mid, upper, preferred_element_type=F32)
          + jnp.dot(lo, upper, preferred_element_type=F32))
    cum = cs + carry[:, 0:1]
    carry[...] = jnp.broadcast_to(cum[:, cw - 1:cw], carry.shape)
    hi_ref[...], mid_ref[...], lo_ref[...] = _split3(cum * LOG2E)


def logf_cumsum(pre, b_col, past, length):
    r, lp = pre.shape
    cw = KEY_PAD
    blk = pl.BlockSpec((r, cw), lambda j: (0, j))
    return pl.pallas_call(
        functools.partial(_logf_cumsum_kernel, past=past, length=length, cw=cw),
        grid=(lp // cw,),
        in_specs=[blk, pl.BlockSpec((r, 1), lambda j: (0, 0))],
        out_specs=[blk, blk, blk, blk],
        out_shape=[jax.ShapeDtypeStruct((r, lp), F32)] + [jax.ShapeDtypeStruct((r, lp), BF16)] * 3,
        scratch_shapes=[pltpu.VMEM((r, LANES), F32)],
        compiler_params=_cparams(("arbitrary",)),
    )(pre, b_col)


def _softmax_rows(s_parts, m_old, l_old):
    mx = s_parts[0]
    for sp in s_parts[1:]:
        mx = jnp.maximum(mx, sp)
    m_new = jnp.maximum(m_old, jnp.max(mx, axis=-1, keepdims=True))
    alpha = jnp.exp2(m_old - m_new)
    ps = [jnp.exp2(sp - m_new) for sp in s_parts]
    tot = ps[0]
    for p in ps[1:]:
        tot = tot + p
    return ps, m_new, alpha * l_old + tot, alpha


def _sb_kernel(q_ref, k_ref, v_ref, o_ref, s_scr, b_scr, hi_scr, lo_scr, p_scr, tri, acc, carry,
               *, off, tq, tk, heads):
    qi = pl.program_id(1)
    q0 = off + qi * tq
    kb_last = (q0 + tq - 2) // tk
    ng = tk // LANES
    nchunk = tq // BF16_ROWS

    acc[...] = jnp.zeros_like(acc)
    carry[...] = jnp.zeros_like(carry)
    row = lax.broadcasted_iota(jnp.int32, (tk, tk), 0)
    col = lax.broadcasted_iota(jnp.int32, (tk, tk), 1)
    tri[...] = jnp.where(row > col, 1.0, 0.0).astype(BF16)

    def block(state):
        kb, _ = state
        k0 = pl.multiple_of(kb * tk, tk)
        kpos = k0 + lax.broadcasted_iota(jnp.int32, (1, LANES), 1)
        for h in range(heads):
            sl = slice(h * HEAD_DIM, (h + 1) * HEAD_DIM)
            s_scr[h] = lax.dot_general(q_ref[0, :, sl], k_ref[0, pl.ds(k0, tk), sl], NT_DIMS,
                                       preferred_element_type=F32)
        for h in range(heads):
            for c in range(nchunk):
                rows = slice(c * BF16_ROWS, (c + 1) * BF16_ROWS)
                qpos = q0 + c * BF16_ROWS + lax.broadcasted_iota(jnp.int32, (BF16_ROWS, 1), 0)
                for g in range(ng):
                    ls = slice(g * LANES, (g + 1) * LANES)
                    z = s_scr[h, rows, ls]
                    lsig = jnp.minimum(z, 0.0) - jnp.log2(1.0 + jnp.exp2(-jnp.abs(z)))
                    l1m = jnp.where(kpos + g * LANES < qpos, lsig - z, 0.0)
                    hi = l1m.astype(BF16)
                    hi_scr[h, rows, ls] = hi
                    lo_scr[h, rows, ls] = (l1m - hi.astype(F32)).astype(BF16)
                    s_scr[h, rows, ls] = lsig
        for h in range(heads):
            b_scr[h] = (jnp.dot(hi_scr[h], tri[...], preferred_element_type=F32)
                        + jnp.dot(lo_scr[h], tri[...], preferred_element_type=F32))
        for h in range(heads):
            for c in range(nchunk):
                rows = slice(c * BF16_ROWS, (c + 1) * BF16_ROWS)
                qpos = q0 + c * BF16_ROWS + lax.broadcasted_iota(jnp.int32, (BF16_ROWS, 1), 0)
                run = carry[h, rows, :]
                for g in range(ng):
                    ls = slice(g * LANES, (g + 1) * LANES)
                    a = jnp.exp2(s_scr[h, rows, ls] + b_scr[h, rows, ls] + run)
                    a = jnp.where(kpos + g * LANES < qpos, a, 0.0)
                    p_scr[h, rows, ls] = a.astype(BF16)
                first = (b_scr[h, rows, 0:1] + hi_scr[h, rows, 0:1].astype(F32)
                         + lo_scr[h, rows, 0:1].astype(F32))
                carry[h, rows, :] = run + first
        for h in range(heads):
            sl = slice(h * HEAD_DIM, (h + 1) * HEAD_DIM)
            acc[h] += jnp.dot(p_scr[h], v_ref[0, pl.ds(k0, tk), sl], preferred_element_type=F32)
        return kb - 1, jnp.max(carry[...]) > SB_DEAD_LOG2

    lax.while_loop(lambda st: jnp.logical_and(st[0] >= 0, st[1]), block,
                   (kb_last, jnp.bool_(True)))
    for h in range(heads):
        o_ref[0, :, h * HEAD_DIM:(h + 1) * HEAD_DIM] = acc[h].astype(o_ref.dtype)


def sb_attention(q, k_arr, kc, v_arr, vc, t, length, tq, tk):
    b = q.shape[0]
    lp = k_arr.shape[1]
    w = H_SB * HEAD_DIM
    return pl.pallas_call(
        functools.partial(_sb_kernel, off=length - t, tq=tq, tk=tk, heads=H_SB),
        grid=(b, t // tq),
        in_specs=[pl.BlockSpec((1, tq, w), lambda bi, qi: (bi, qi, 0)),
                  pl.BlockSpec((1, lp, w), lambda bi, qi: (bi, 0, kc)),
                  pl.BlockSpec((1, lp, w), lambda bi, qi: (bi, 0, vc))],
        out_specs=pl.BlockSpec((1, tq, w), lambda bi, qi: (bi, qi, 0)),
        out_shape=jax.ShapeDtypeStruct((b, t, w), BF16),
        scratch_shapes=[pltpu.VMEM((H_SB, tq, tk), F32),
                        pltpu.VMEM((H_SB, tq, tk), F32),
                        pltpu.VMEM((H_SB, tq, tk), BF16),
                        pltpu.VMEM((H_SB, tq, tk), BF16),
                        pltpu.VMEM((H_SB, tq, tk), BF16),
                        pltpu.VMEM((tk, tk), BF16),
                        pltpu.VMEM((H_SB, tq, HEAD_DIM), F32),
                        pltpu.VMEM((H_SB, tq, LANES), F32)],
        compiler_params=_cparams(("parallel", "arbitrary")),
    )(q, k_arr, v_arr)


def _fox_kernel(q_ref, qb_ref, k_ref, kb_ref, v_ref, o_ref, s_scr, p_scr, acc, m_scr, l_scr, a_scr,
                *, off, tq, tk, heads):
    qi = pl.program_id(1)
    q0 = off + qi * tq
    kb_last = (q0 + tq - 1) // tk
    ng = tk // LANES
    nchunk = tq // BF16_ROWS

    acc[...] = jnp.zeros_like(acc)
    m_scr[...] = jnp.full_like(m_scr, M_INIT)
    l_scr[...] = jnp.zeros_like(l_scr)

    def block(kb, _, masked):
        k0 = pl.multiple_of(kb * tk, tk)
        kpos = k0 + lax.broadcasted_iota(jnp.int32, (1, LANES), 1)
        for h in range(heads):
            sl = slice(h * HEAD_DIM, (h + 1) * HEAD_DIM)
            q_aug = jnp.concatenate([q_ref[0, :, sl], qb_ref[0, :, sl]], axis=1)
            k_aug = jnp.concatenate([k_ref[0, pl.ds(k0, tk), sl], kb_ref[0, pl.ds(k0, tk), sl]],
                                    axis=1)
            s_scr[h] = lax.dot_general(q_aug, k_aug, NT_DIMS, preferred_element_type=F32)
        for h in range(heads):
            for c in range(nchunk):
                rows = slice(c * BF16_ROWS, (c + 1) * BF16_ROWS)
                parts = [s_scr[h, rows, g * LANES:(g + 1) * LANES] for g in range(ng)]
                if masked:
                    qpos = q0 + c * BF16_ROWS + lax.broadcasted_iota(jnp.int32, (BF16_ROWS, 1), 0)
                    parts = [jnp.where(kpos + g * LANES <= qpos, parts[g], -jnp.inf)
                             for g in range(ng)]
                ps, m_new, l_new, alpha = _softmax_rows(parts, m_scr[h, rows, :], l_scr[h, rows, :])
                m_scr[h, rows, :] = m_new
                l_scr[h, rows, :] = l_new
                a_scr[h, rows, :] = alpha
                for g in range(ng):
                    p_scr[h, rows, g * LANES:(g + 1) * LANES] = ps[g].astype(BF16)
        for h in range(heads):
            sl = slice(h * HEAD_DIM, (h + 1) * HEAD_DIM)
            acc[h] = a_scr[h] * acc[h] + jnp.dot(p_scr[h], v_ref[0, pl.ds(k0, tk), sl],
                                                 preferred_element_type=F32)
        return 0

    n_plain = jnp.minimum((q0 + 1) // tk, kb_last + 1)
    lax.fori_loop(0, n_plain, functools.partial(block, masked=False), 0)
    lax.fori_loop(n_plain, kb_last + 1, functools.partial(block, masked=True), 0)
    for h in range(heads):
        l = jnp.sum(l_scr[h], axis=-1, keepdims=True)
        o_ref[0, :, h * HEAD_DIM:(h + 1) * HEAD_DIM] = (acc[h] / l).astype(o_ref.dtype)


def _fox_bias_cols(parts, first, sign):
    b, r, h, n = parts.shape
    place = np.zeros((h, n + 1, h, HEAD_DIM), np.float32)
    for hd in range(h):
        for j in range(n):
            place[hd, j, hd, first + j] = sign
            place[hd, n, hd, (first + n + j) % (2 * n)] = 1.0
    src = jnp.concatenate([parts, jnp.ones((b, r, h, 1), BF16)], axis=-1)
    out = jnp.einsum('brk,kc->brc', src.reshape(b, r, h * (n + 1)),
                     jnp.asarray(place.reshape(h * (n + 1), h * HEAD_DIM), BF16),
                     preferred_element_type=F32)
    return out.astype(BF16)


def fox_attention(q, q_bias, k_arr, kc, k_bias, v_arr, vc, t, length, tq, tk):
    b = q.shape[0]
    lp = k_arr.shape[1]
    w = H_FOX * HEAD_DIM
    return pl.pallas_call(
        functools.partial(_fox_kernel, off=length - t, tq=tq, tk=tk, heads=H_FOX),
        grid=(b, t // tq),
        in_specs=[pl.BlockSpec((1, tq, w), lambda bi, qi: (bi, qi, 0)),
                  pl.BlockSpec((1, tq, w), lambda bi, qi: (bi, qi, 0)),
                  pl.BlockSpec((1, lp, w), lambda bi, qi: (bi, 0, kc)),
                  pl.BlockSpec((1, lp, w), lambda bi, qi: (bi, 0, 0)),
                  pl.BlockSpec((1, lp, w), lambda bi, qi: (bi, 0, vc))],
        out_specs=pl.BlockSpec((1, tq, w), lambda bi, qi: (bi, qi, 0)),
        out_shape=jax.ShapeDtypeStruct((b, t, w), BF16),
        scratch_shapes=[pltpu.VMEM((H_FOX, tq, tk), F32),
                        pltpu.VMEM((H_FOX, tq, tk), BF16),
                        pltpu.VMEM((H_FOX, tq, HEAD_DIM), F32),
                        pltpu.VMEM((H_FOX, tq, LANES), F32),
                        pltpu.VMEM((H_FOX, tq, LANES), F32),
                        pltpu.VMEM((H_FOX, tq, LANES), F32)],
        compiler_params=_cparams(("parallel", "arbitrary")),
    )(q, q_bias, k_arr, k_bias, v_arr)


def _moe_kernel(*refs, n_mix, n_groups, per_group, final_norm):
    x_ref, mix_refs = refs[0], refs[1:1 + n_mix]
    wo_ref, g_ref, wr_ref, wg_ref, wu_ref, wd_ref, gf_ref, o_ref, hb, comb = refs[1 + n_mix:]
    e = pl.program_id(1)
    n_exp = n_groups * per_group
    lane = lax.broadcasted_iota(jnp.int32, comb.shape, 1)

    @pl.when(e == 0)
    def _():
        x = x_ref[...]
        k0 = 0
        for m_ref in mix_refs:
            k = m_ref.shape[1]
            x = x + jnp.dot(m_ref[...], wo_ref[k0:k0 + k, :], preferred_element_type=F32)
            k0 += k
        h = _rms(x, g_ref[...])
        hb[...] = h.astype(BF16)
        h_hi = hb[...]
        h_lo = (h - h_hi.astype(F32)).astype(BF16)
        w = wr_ref[...]
        w_hi = w.astype(BF16)
        w_lo = (w - w_hi.astype(F32)).astype(BF16)
        logits = (jnp.dot(h_hi, w_hi, preferred_element_type=F32)
                  + jnp.dot(h_hi, w_lo, preferred_element_type=F32)
                  + jnp.dot(h_lo, w_hi, preferred_element_type=F32))
        lane_f = lane.astype(F32)
        gl = jnp.where(lane < n_groups, logits, -jnp.inf)
        gmax = jnp.max(gl, axis=-1, keepdims=True)
        gsel = jnp.min(jnp.where(gl == gmax, lane_f, float(LANES)), axis=-1, keepdims=True)
        p_group = 1.0 / jnp.sum(jnp.where(lane < n_groups, jnp.exp(gl - gmax), 0.0),
                                axis=-1, keepdims=True)
        first = n_groups + gsel * per_group
        el = jnp.where((lane_f >= first) & (lane_f < first + per_group), logits, -jnp.inf)
        v1 = jnp.max(el, axis=-1, keepdims=True)
        i1 = jnp.min(jnp.where(el == v1, lane_f, float(LANES)), axis=-1, keepdims=True)
        el2 = jnp.where(lane_f == i1, -jnp.inf, el)
        v2 = jnp.max(el2, axis=-1, keepdims=True)
        i2 = jnp.min(jnp.where(el2 == v2, lane_f, float(LANES)), axis=-1, keepdims=True)
        ratio = jnp.exp(v2 - v1)
        gate1 = p_group / (1.0 + ratio)
        comb[...] = jnp.where(lane_f == i1, gate1, jnp.where(lane_f == i2, gate1 * ratio, 0.0))
        o_ref[...] = x

    c = jnp.sum(jnp.where(lane == e + n_groups, comb[...], 0.0), axis=-1, keepdims=True)
    hv = hb[...]
    gate = jnp.dot(hv, wg_ref[0], preferred_element_type=F32)
    up = jnp.dot(hv, wu_ref[0], preferred_element_type=F32)
    act = gate * (1.0 / (1.0 + jnp.exp(-gate))) * up * c
    o_ref[...] += jnp.dot(act.astype(BF16), wd_ref[0], preferred_element_type=F32)

    if final_norm:
        @pl.when(e == n_exp - 1)
        def _():
            o_ref[...] = _rms(o_ref[...], gf_ref[...])


def moe_block(x, mix, w_out, g, w_group, w_router, w_gate, w_up, w_down, g_final=None):
    n, d = x.shape
    n_groups, per_group = w_router.shape[1], w_router.shape[2]
    n_exp, _, f = w_gate.shape
    w_route = jnp.concatenate(
        [w_group, w_router.reshape(d, n_exp),
         jnp.zeros((d, LANES - n_groups - n_exp), F32)], axis=1)
    final_norm = g_final is not None
    gf = (g_final if final_norm else g).reshape(1, d)
    tm = min(1024, n)
    return pl.pallas_call(
        functools.partial(_moe_kernel, n_mix=len(mix), n_groups=n_groups, per_group=per_group,
                          final_norm=final_norm),
        grid=(n // tm, n_exp),
        in_specs=[pl.BlockSpec((tm, d), lambda i, e: (i, 0))]
        + [pl.BlockSpec((tm, m.shape[1]), lambda i, e: (i, 0)) for m in mix]
        + [pl.BlockSpec(w_out.shape, lambda i, e: (0, 0)),
                  pl.BlockSpec((1, d), lambda i, e: (0, 0)),
                  pl.BlockSpec((d, LANES), lambda i, e: (0, 0)),
                  pl.BlockSpec((1, d, f), lambda i, e: (e, 0, 0)),
                  pl.BlockSpec((1, d, f), lambda i, e: (e, 0, 0)),
                  pl.BlockSpec((1, f, d), lambda i, e: (e, 0, 0)),
                  pl.BlockSpec((1, d), lambda i, e: (0, 0))],
        out_specs=pl.BlockSpec((tm, d), lambda i, e: (i, 0)),
        out_shape=jax.ShapeDtypeStruct((n, d), F32),
        scratch_shapes=[pltpu.VMEM((tm, d), BF16), pltpu.VMEM((tm, LANES), F32)],
        compiler_params=_cparams(("parallel", "arbitrary")),
    )(x, *mix, w_out, g.reshape(1, d), w_route, w_gate.astype(BF16), w_up.astype(BF16),
      w_down.astype(BF16), gf)


def _rope_tables(pos, head_dim, rows, live_lanes=LANES):
    half = head_dim // 2
    inv_freq = ROPE_THETA ** (-jnp.arange(half, dtype=F32) / half)
    ang = pos.astype(F32)[:, None] * inv_freq[None, :]
    cos, sin = jnp.cos(ang), jnp.sin(ang)
    reps = LANES // head_dim
    cos = jnp.tile(jnp.concatenate([cos, cos], axis=1), (1, reps))
    sin = jnp.tile(jnp.concatenate([-sin, sin], axis=1), (1, reps))
    live = jnp.arange(LANES) < live_lanes
    cos = jnp.where(live, cos, 1.0)
    sin = jnp.where(live, sin, 0.0)
    reps_rows = max(rows // pos.shape[0], 1)
    return jnp.tile(cos, (reps_rows, 1)), jnp.tile(sin, (reps_rows, 1))


def _code_to_float(code):
    return pltpu.bitcast(code ^ ((code >> 31) & 0x7FFFFFFF), F32)


def _dsa_reach(q_first, tq, length):
    return jnp.minimum(((q_first + tq - 1) // CHUNK + 1) * CHUNK, length)


def _dsa_select(qi_ref, wt_ref, ki_ref, bias, sc, *, off, length, tq, ksel, cw):
    qb = pl.program_id(1)
    q0 = off + qb * tq
    qchunk = (q0 + lax.broadcasted_iota(jnp.int32, (1, SELECT_Q), 1)) // CHUNK
    nck = (_dsa_reach(q0, tq, length) + cw - 1) // cw
    w = wt_ref[0] * (H_IDX ** -0.5 * D_IDX ** -0.5)
    kf = float(ksel)

    def fill(c, _):
        for s in range(cw // LANES):
            k0 = pl.multiple_of(c * cw, cw) + s * LANES
            kk = ki_ref[0, pl.ds(k0, LANES), :]
            score = jnp.zeros((LANES, SELECT_Q), F32)
            for h in range(H_IDX):
                lg = lax.dot_general(kk, qi_ref[0, :, h * D_IDX:(h + 1) * D_IDX], NT_DIMS,
                                     preferred_element_type=F32)
                score = score + jnp.maximum(lg, 0.0) * w[h:h + 1, :]
            kpos = k0 + lax.broadcasted_iota(jnp.int32, (LANES, 1), 0)
            admissible = (kpos // CHUNK <= qchunk) & (kpos < length)
            sc[c, s * LANES:(s + 1) * LANES, :] = jnp.where(admissible, score + 0.0, -jnp.inf)
        return 0

    lax.fori_loop(0, nck, fill, 0)

    def count(t, strict=False):
        def body(c, tot):
            x = sc[c].reshape(cw // COUNT_ROWS, COUNT_ROWS, SELECT_Q)
            hit = (x > t) if strict else (x >= t)
            return tot + jnp.sum(jnp.where(hit, 1.0, 0.0), axis=0)
        tot = lax.fori_loop(0, nck, body, jnp.zeros((COUNT_ROWS, SELECT_Q), F32))
        return jnp.sum(tot, axis=0, keepdims=True)

    code0 = jnp.where(count(jnp.zeros((1, SELECT_Q), F32)) >= kf, 0, INT32_MIN).astype(jnp.int32)

    def bit_step(i, code):
        cand = code + lax.shift_left(jnp.int32(1), 30 - i)
        return jnp.where(count(_code_to_float(cand)) >= kf, cand, code)

    code = lax.fori_loop(0, 31, bit_step, code0)
    thr = _code_to_float(jnp.maximum(code, NEG_INF_CODE + 1))
    real = lax.broadcasted_iota(jnp.int32, (1, SELECT_Q), 1) < tq
    n_ge = jnp.where(real, count(thr), 0.0)
    n_gt = count(thr, strict=True)
    tied = jnp.max(n_ge) > kf

    @pl.when(jnp.logical_not(tied))
    def _():
        def emit(c, _):
            bias[c] = jnp.where(sc[c] >= thr, 0.0, -jnp.inf).astype(bias.dtype)
            return 0
        lax.fori_loop(0, nck, emit, 0)

    @pl.when(tied)
    def _():
        need = kf - n_gt
        row = lax.broadcasted_iota(jnp.int32, (LANES, LANES), 0)
        col = lax.broadcasted_iota(jnp.int32, (LANES, LANES), 1)
        earlier = jnp.where(col < row, 1.0, 0.0).astype(BF16)

        def emit(c, run):
            for s in range(cw // LANES):
                x = sc[c, s * LANES:(s + 1) * LANES, :]
                eq = jnp.where(x == thr, 1.0, 0.0)
                rank = run + jnp.dot(earlier, eq.astype(BF16), preferred_element_type=F32)
                sel = (x > thr) | ((x == thr) & (rank < need))
                bias[c, s * LANES:(s + 1) * LANES, :] = (
                    jnp.where(sel, 0.0, -jnp.inf).astype(bias.dtype))
                run = run + jnp.sum(eq, axis=0, keepdims=True)
            return run
        lax.fori_loop(0, nck, emit, jnp.zeros((1, SELECT_Q), F32))


def _dsa_attend(q_ref, k_ref, vt_ref, bias, o_ref, qs, s_scr, p_scr, bf, acc, m_scr, l_scr,
                *, off, length, tq, tk):
    qb = pl.program_id(1)
    rep = H_DSA // KV_DSA
    nkb = (_dsa_reach(off + qb * tq, tq, length) + tk - 1) // tk
    nslab = tk // COUNT_ROWS

    if tq < SELECT_Q:
        qs[...] = jnp.zeros_like(qs)
    for g in range(KV_DSA):
        for r in range(rep):
            hd = g * rep + r
            qs[g, r * SELECT_Q:r * SELECT_Q + tq, :] = q_ref[0, :, hd * HEAD_DIM:(hd + 1) * HEAD_DIM]
    acc[...] = jnp.zeros_like(acc)
    m_scr[...] = jnp.full_like(m_scr, M_INIT)
    l_scr[...] = jnp.zeros_like(l_scr)

    def block(kb, _):
        k0 = pl.multiple_of(kb * tk, tk)
        bf[...] = bias[kb].astype(F32)
        for g in range(KV_DSA):
            sl = slice(g * HEAD_DIM, (g + 1) * HEAD_DIM)
            s_scr[g] = lax.dot_general(k_ref[0, pl.ds(k0, tk), sl], qs[g], NT_DIMS,
                                       preferred_element_type=F32)
        for g in range(KV_DSA):
            for r in range(rep):
                cols = slice(r * SELECT_Q, (r + 1) * SELECT_Q)
                slab = lambda i: slice(i * COUNT_ROWS, (i + 1) * COUNT_ROWS)
                mx = s_scr[g, slab(0), cols] + bf[slab(0), :]
                for i in range(1, nslab):
                    mx = jnp.maximum(mx, s_scr[g, slab(i), cols] + bf[slab(i), :])
                m_old = m_scr[g, r]
                m_new = jnp.maximum(m_old, jnp.max(mx, axis=0, keepdims=True))
                tot = jnp.zeros((COUNT_ROWS, SELECT_Q), F32)
                for i in range(nslab):
                    p = jnp.exp2(s_scr[g, slab(i), cols] + bf[slab(i), :] - m_new)
                    tot = tot + p
                    p_scr[g, slab(i), cols] = p.astype(BF16)
                alpha = jnp.exp2(m_old - m_new)
                m_scr[g, r] = m_new
                l_scr[g, r] = alpha * l_scr[g, r] + jnp.sum(tot, axis=0, keepdims=True)
                acc[g, :, cols] = alpha * acc[g, :, cols]
        for g in range(KV_DSA):
            sl = slice(g * HEAD_DIM, (g + 1) * HEAD_DIM)
            acc[g] += jnp.dot(vt_ref[0, kb, sl, :], p_scr[g], preferred_element_type=F32)
        return 0

    lax.fori_loop(0, nkb, block, 0)
    for g in range(KV_DSA):
        for r in range(rep):
            hd = g * rep + r
            cols = slice(r * SELECT_Q, (r + 1) * SELECT_Q)
            o_t = acc[g, :, cols] / l_scr[g, r]
            o_ref[0, :, hd * HEAD_DIM:(hd + 1) * HEAD_DIM] = o_t.T[:tq].astype(o_ref.dtype)


def _dsa_kernel(qi_ref, wt_ref, ki_ref, q_ref, k_ref, vt_ref, o_ref, sc, bias, qs, s_scr, p_scr, bf,
                acc, m_scr, l_scr, *, off, length, tq, tk, ksel):
    _dsa_select(qi_ref, wt_ref, ki_ref, bias, sc, off=off, length=length, tq=tq, ksel=ksel, cw=tk)
    _dsa_attend(q_ref, k_ref, vt_ref, bias, o_ref, qs, s_scr, p_scr, bf, acc, m_scr, l_scr,
                off=off, length=length, tq=tq, tk=tk)


def dsa_sparse_attention(qi, w_idx, ki, q, k_arr, v_arr, t, length, tq, tk):
    assert tq == SELECT_Q or (t == tq and tq < SELECT_Q)
    b, lp, _ = ki.shape
    t_pad = _round_up(t, SELECT_Q)
    qi = jnp.pad(qi, ((0, 0), (0, t_pad - t), (0, 0)))
    w_t = jnp.pad(jnp.swapaxes(w_idx, 1, 2), ((0, 0), (0, 0), (0, t_pad - t)))
    v_t = jnp.swapaxes(v_arr.reshape(b, lp // tk, tk, -1), 2, 3)
    wq = H_DSA * HEAD_DIM
    wk = KV_DSA * HEAD_DIM
    rep = H_DSA // KV_DSA
    nch = lp // tk
    per_q = lambda w: pl.BlockSpec((1, tq, w), lambda bi, qb: (bi, qb, 0))
    per_b = lambda w: pl.BlockSpec((1, lp, w), lambda bi, qb: (bi, 0, 0))
    return pl.pallas_call(
        functools.partial(_dsa_kernel, off=length - t, length=length, tq=tq, tk=tk,
                          ksel=min(TOPK_MAX, length // 4)),
        grid=(b, t // tq),
        in_specs=[pl.BlockSpec((1, SELECT_Q, H_IDX * D_IDX), lambda bi, qb: (bi, qb, 0)),
                  pl.BlockSpec((1, H_IDX, SELECT_Q), lambda bi, qb: (bi, 0, qb)),
                  per_b(D_IDX), per_q(wq), per_b(wk),
                  pl.BlockSpec((1, nch, wk, tk), lambda bi, qb: (bi, 0, 0, 0))],
        out_specs=per_q(wq),
        out_shape=jax.ShapeDtypeStruct((b, t, wq), BF16),
        scratch_shapes=[pltpu.VMEM((nch, tk, SELECT_Q), F32),
                        pltpu.VMEM((nch, tk, SELECT_Q), BF16),
                        pltpu.VMEM((KV_DSA, rep * SELECT_Q, HEAD_DIM), BF16),
                        pltpu.VMEM((KV_DSA, tk, rep * SELECT_Q), F32),
                        pltpu.VMEM((KV_DSA, tk, rep * SELECT_Q), BF16),
                        pltpu.VMEM((tk, SELECT_Q), F32),
                        pltpu.VMEM((KV_DSA, HEAD_DIM, rep * SELECT_Q), F32),
                        pltpu.VMEM((KV_DSA, rep, 1, SELECT_Q), F32),
                        pltpu.VMEM((KV_DSA, rep, 1, SELECT_Q), F32)],
        compiler_params=_cparams(("parallel", "arbitrary")),
    )(qi, w_t, ki, q, k_arr, v_t)


def _with_past(past, new, lp):
    b = new.shape[0]
    parts = [] if past is None else [past.reshape(b, past.shape[1], -1).astype(BF16)]
    parts.append(new.astype(BF16))
    rows = sum(p.shape[1] for p in parts)
    if rows < lp:
        parts.append(jnp.zeros((b, lp - rows, new.shape[2]), BF16))
    return parts[0] if len(parts) == 1 else jnp.concatenate(parts, axis=1)


def _mixer_ab(x2, b, t, past, g, w_in, b_forget):
    wsb = H_SB * HEAD_DIM
    wfx = H_FOX * HEAD_DIM
    qscale = LOG2E * HEAD_DIM ** -0.5
    c_fx = 3 * wsb
    c_gate = c_fx + 3 * wfx
    plan = [Segment(0, wsb, bf16_scale=qscale),
            Segment(wsb, wsb, f32=True, bf16_scale=1.0, heads_as_rows=True),
            Segment(2 * wsb, wsb, f32=True, bf16_scale=1.0, heads_as_rows=True),
            Segment(c_fx, wfx, bf16_scale=qscale),
            Segment(c_fx + wfx, wfx, f32=True, bf16_scale=1.0, heads_as_rows=True),
            Segment(c_fx + 2 * wfx, wfx, f32=True, bf16_scale=1.0, heads_as_rows=True),
            Segment(c_gate, LANES, f32=True)]
    w_pad = jnp.pad(w_in, ((0, 0), (0, c_gate + LANES - w_in.shape[1]))).astype(BF16)
    (q_sb, k_sb, k_sb_b, v_sb, v_sb_b, q_fx, k_fx, k_fx_b, v_fx, v_fx_b, gate) = norm_project(
        x2, g, w_pad, plan)
    rows = tuple(a.reshape(b, t, -1, HEAD_DIM) for a in (k_sb, v_sb, k_fx, v_fx))
    f_logit = gate.reshape(b, t, LANES)[:, :, :H_FOX]

    p = 0 if past is None else past[0].shape[1]
    length = p + t
    lp = _round_up(length, KEY_PAD)
    new_t = jnp.swapaxes(f_logit, 1, 2)
    parts = [new_t] if past is None else [jnp.swapaxes(past[4].astype(F32), 1, 2), new_t]
    if length < lp:
        parts.append(jnp.zeros((b, H_FOX, lp - length), F32))
    pre = (parts[0] if len(parts) == 1 else jnp.concatenate(parts, axis=2)).reshape(b * H_FOX, lp)
    b_col = jnp.tile(b_forget.astype(F32), b).reshape(b * H_FOX, 1)
    logf_t, *cum_parts = logf_cumsum(pre, b_col, p, length)
    logf = jnp.swapaxes(logf_t.reshape(b, H_FOX, lp)[:, :, p:length], 1, 2)

    f_parts = jnp.stack([jnp.swapaxes(c.reshape(b, H_FOX, lp), 1, 2) for c in cum_parts],
                        axis=-1)
    q_bias = _fox_bias_cols(f_parts[:, p:length], 0, 1.0)
    k_bias = _fox_bias_cols(f_parts, f_parts.shape[-1], -1.0)

    as3 = lambda a: a.reshape(b, t, -1)
    ks, vs, kf, vf = (_with_past(None if past is None else past[i], as3(a), lp)
                      for i, a in enumerate((k_sb_b, v_sb_b, k_fx_b, v_fx_b)))
    tq = min(256, t)
    o_sb = sb_attention(as3(q_sb), ks, 0, vs, 0, t, length, tq, 256)
    o_fx = fox_attention(as3(q_fx), q_bias, kf, 0, k_bias, vf, 0, t, length, tq, 512)
    return [o_sb.reshape(b * t, wsb), o_fx.reshape(b * t, wfx)], rows + (logf,)


def _mixer_dsa(x2, b, t, past, g, w_in):
    n = b * t
    wq = H_DSA * HEAD_DIM
    wk = KV_DSA * HEAD_DIM
    wi = H_IDX * D_IDX
    p = 0 if past is None else past[0].shape[1]
    length = p + t
    lp = _round_up(length, KEY_PAD)
    pos = p + jnp.arange(t)
    tab_rows = max(t, min(256, n))
    tables = [_rope_tables(pos, HEAD_DIM, tab_rows),
              _rope_tables(pos, D_IDX, tab_rows),
              _rope_tables(pos, D_IDX, tab_rows, live_lanes=D_IDX)]
    c_k, c_v, c_qi, c_ki = wq, wq + wk, wq + 2 * wk, wq + 2 * wk + wi
    plan = [Segment(0, wq, rope=0, half=HEAD_DIM // 2, bf16_scale=LOG2E * HEAD_DIM ** -0.5),
            Segment(c_k, wk, rope=0, half=HEAD_DIM // 2, f32=True, bf16_scale=1.0,
                    heads_as_rows=True),
            Segment(c_v, wk, f32=True, bf16_scale=1.0, heads_as_rows=True),
            Segment(c_qi, wi, rope=1, half=D_IDX // 2, bf16_scale=1.0),
            Segment(c_ki, LANES, rope=2, half=D_IDX // 2, f32=True, bf16_scale=1.0)]
    w_pad = jnp.pad(w_in, ((0, 0), (0, c_ki + LANES - w_in.shape[1]))).astype(BF16)
    q_b, k_f, k_b, v_f, v_b, qi_b, kw_f, kw_b = norm_project(x2, g, w_pad, plan, tables)
    as3 = lambda a: a.reshape(b, t, -1)
    k_rows = k_f.reshape(b, t, KV_DSA, HEAD_DIM)
    v_rows = v_f.reshape(b, t, KV_DSA, HEAD_DIM)
    ki_rows = as3(kw_f)[:, :, :D_IDX]
    w_idx = as3(kw_f)[:, :, D_IDX:D_IDX + H_IDX]

    tk = KEY_PAD
    ki_all = _with_past(None if past is None else past[2], as3(kw_b)[:, :, :D_IDX], lp)
    k_all = _with_past(None if past is None else past[0], as3(k_b), lp)
    v_all = _with_past(None if past is None else past[1], as3(v_b), lp)
    o = dsa_sparse_attention(as3(qi_b), w_idx, ki_all, as3(q_b), k_all, v_all, t, length,
                             min(128, t), tk)
    return [o.reshape(n, wq)], (k_rows, v_rows, ki_rows)


def _trunk(x, past_ab, past_dsa, norm_mix, norm_ffn, norm_final, w_in_ab, b_forget, w_out_ab,
           w_in_dsa, w_out_dsa, moe_w_group, moe_w_router, moe_w_gate, moe_w_up, moe_w_down):
    b, t, d = x.shape
    x2 = x.reshape(b * t, d)
    mix, rows_ab = _mixer_ab(x2, b, t, past_ab, norm_mix[0], w_in_ab, b_forget)
    x2 = moe_block(x2, mix, w_out_ab.astype(BF16), norm_ffn[0], moe_w_group[0], moe_w_router[0],
                   moe_w_gate[0], moe_w_up[0], moe_w_down[0])
    mix, rows_dsa = _mixer_dsa(x2, b, t, past_dsa, norm_mix[1], w_in_dsa)
    y = moe_block(x2, mix, w_out_dsa.astype(BF16), norm_ffn[1], moe_w_group[1], moe_w_router[1],
                  moe_w_gate[1], moe_w_up[1], moe_w_down[1], g_final=norm_final)
    return y.reshape(b, t, d), rows_ab, rows_dsa


def kernel(x_prompt, x_sample, cache_sb_k, cache_sb_v, cache_fox_k, cache_fox_v, cache_fox_logf,
           cache_dsa_k, cache_dsa_v, cache_dsa_idx_k, norm_mix, norm_ffn, norm_final,
           w_in_ab, b_forget, w_out_ab, w_in_dsa, w_out_dsa,
           moe_w_group, moe_w_router, moe_w_gate, moe_w_up, moe_w_down):
    weights = (norm_mix, norm_ffn, norm_final, w_in_ab, b_forget, w_out_ab, w_in_dsa, w_out_dsa,
               moe_w_group, moe_w_router, moe_w_gate, moe_w_up, moe_w_down)
    y_p, ab_p, dsa_p = _trunk(x_prompt, None, None, *weights)
    y_s, ab_s, dsa_s = _trunk(
        x_sample, (cache_sb_k, cache_sb_v, cache_fox_k, cache_fox_v, cache_fox_logf),
        (cache_dsa_k, cache_dsa_v, cache_dsa_idx_k), *weights)
    return (y_p, y_s) + ab_p + dsa_p + ab_s + dsa_s
```

```python
import functools
from typing import NamedTuple, Optional

import jax
import jax.numpy as jnp
import numpy as np
from jax import lax
from jax.experimental import pallas as pl
from jax.experimental.pallas import tpu as pltpu

CHUNK = 64
HEAD_DIM = 128
H_SB = 4
H_FOX = 4
H_DSA = 8
KV_DSA = 2
H_IDX = 4
D_IDX = 64
TOPK_MAX = 256
ROPE_THETA = 10000.0
EPS = 1e-6

LANES = 128
SUBLANES = 8
BF16_ROWS = 16

KEY_PAD = 512
LOG2E = 1.4426950408889634
M_INIT = -1e38
SB_DEAD_LOG2 = -150.0
INT32_MIN = -(2 ** 31)
SELECT_Q = 128
COUNT_ROWS = 64
NEG_INF_CODE = int(np.int32(np.uint32(0xFF800000) ^ np.uint32(0x7FFFFFFF)))

F32 = jnp.float32
BF16 = jnp.bfloat16
NT_DIMS = (((1,), (1,)), ((), ()))


def _round_up(a, b):
    return (a + b - 1) // b * b


def _cparams(semantics, vmem_mib=48):
    return pltpu.CompilerParams(dimension_semantics=semantics,
                                vmem_limit_bytes=vmem_mib * 1024 * 1024)


def _rms(x, g):
    return x * lax.rsqrt(jnp.mean(x * x, axis=-1, keepdims=True) + EPS) * g


def _log_sigmoid(z):
    return jnp.minimum(z, 0.0) - jnp.log1p(jnp.exp(-jnp.abs(z)))


def _split3(x):
    hi = x.astype(BF16)
    r1 = x - hi.astype(F32)
    mid = r1.astype(BF16)
    lo = (r1 - mid.astype(F32)).astype(BF16)
    return hi, mid, lo


class Segment(NamedTuple):
    start: int
    width: int
    rope: Optional[int] = None
    half: int = 0
    f32: bool = False
    bf16_scale: Optional[float] = None
    heads_as_rows: bool = False


PROJ_TILE = 256


def _rotate(x, cos, sin, half):
    if 2 * half == LANES:
        partner = pltpu.roll(x, half, 1)
    else:
        lane = lax.broadcasted_iota(jnp.int32, x.shape, 1)
        partner = jnp.where(lane % (2 * half) < half,
                            pltpu.roll(x, LANES - half, 1), pltpu.roll(x, half, 1))
    return x * cos + partner * sin


def _proj_kernel(x_ref, g_ref, w_ref, *refs, plan, n_tab):
    tabs, outs = refs[:n_tab], list(refs[n_tab:])
    h = _rms(x_ref[...], g_ref[...]).astype(BF16)
    for seg in plan:
        o_f32 = outs.pop(0) if seg.f32 else None
        o_b16 = outs.pop(0) if seg.bf16_scale is not None else None
        tile = min(PROJ_TILE, seg.width)
        for j in range(seg.width // tile):
            acc = jnp.dot(h, w_ref[:, seg.start + j * tile:seg.start + (j + 1) * tile],
                          preferred_element_type=F32)
            if seg.rope is not None:
                cos, sin = tabs[2 * seg.rope][...], tabs[2 * seg.rope + 1][...]
                acc = jnp.concatenate(
                    [_rotate(acc[:, gi * LANES:(gi + 1) * LANES], cos, sin, seg.half)
                     for gi in range(tile // LANES)], axis=1)
            cols = slice(j * tile, (j + 1) * tile)
            if o_f32 is not None and seg.heads_as_rows:
                heads = seg.width // LANES
                for gi in range(tile // LANES):
                    head = j * (tile // LANES) + gi
                    o_f32[pl.ds(head, acc.shape[0], stride=heads), :] = (
                        acc[:, gi * LANES:(gi + 1) * LANES])
            elif o_f32 is not None:
                o_f32[:, cols] = acc
            if o_b16 is not None:
                o_b16[:, cols] = (acc * seg.bf16_scale).astype(BF16)


def norm_project(x, g, w, plan, tables=()):
    n, d = x.shape
    e = w.shape[1]
    tm = min(256, n)
    flat_tabs = [t for pair in tables for t in pair]
    tab_specs = []
    for tab in flat_tabs:
        period = tab.shape[0] // tm
        tab_specs.append(pl.BlockSpec((tm, LANES), lambda i, period=period: (i % period, 0)))
    out_specs, out_shape = [], []
    for seg in plan:
        if seg.f32:
            heads = seg.width // LANES if seg.heads_as_rows else 1
            out_specs.append(pl.BlockSpec((tm * heads, seg.width // heads), lambda i: (i, 0)))
            out_shape.append(jax.ShapeDtypeStruct((n * heads, seg.width // heads), F32))
        if seg.bf16_scale is not None:
            out_specs.append(pl.BlockSpec((tm, seg.width), lambda i: (i, 0)))
            out_shape.append(jax.ShapeDtypeStruct((n, seg.width), BF16))
    return pl.pallas_call(
        functools.partial(_proj_kernel, plan=tuple(plan), n_tab=len(flat_tabs)),
        grid=(n // tm,),
        in_specs=[pl.BlockSpec((tm, d), lambda i: (i, 0)),
                  pl.BlockSpec((1, d), lambda i: (0, 0)),
                  pl.BlockSpec((d, e), lambda i: (0, 0))] + tab_specs,
        out_specs=out_specs,
        out_shape=out_shape,
        compiler_params=_cparams(("parallel",)),
    )(x, g.reshape(1, d), w, *flat_tabs)


def _logf_cumsum_kernel(pre_ref, b_ref, logf_ref, hi_ref, mid_ref, lo_ref, carry,
                        *, past, length, cw):
    j = pl.program_id(0)

    @pl.when(j == 0)
    def _():
        carry[...] = jnp.zeros_like(carry)

    pre = pre_ref[...]
    pos = j * cw + lax.broadcasted_iota(jnp.int32, pre.shape, 1)
    x = jnp.where(pos >= past, _log_sigmoid(pre + b_ref[...]), pre)
    x = jnp.where(pos < length, x, 0.0)
    logf_ref[...] = x
    row = lax.broadcasted_iota(jnp.int32, (cw, cw), 0)
    col = lax.broadcasted_iota(jnp.int32, (cw, cw), 1)
    upper = jnp.where(row <= col, 1.0, 0.0).astype(BF16)
    hi, mid, lo = _split3(x)
    cs = (jnp.dot(hi, upper, preferred_element_type=F32)
          + jnp.dot(mid, upper, preferred_element_type=F32)
          + jnp.dot(lo, upper, preferred_element_type=F32))
    cum = cs + carry[:, 0:1]
    carry[...] = jnp.broadcast_to(cum[:, cw - 1:cw], carry.shape)
    hi_ref[...], mid_ref[...], lo_ref[...] = _split3(cum * LOG2E)


def logf_cumsum(pre, b_col, past, length):
    r, lp = pre.shape
    cw = KEY_PAD
    blk = pl.BlockSpec((r, cw), lambda j: (0, j))
    return pl.pallas_call(
        functools.partial(_logf_cumsum_kernel, past=past, length=length, cw=cw),
        grid=(lp // cw,),
        in_specs=[blk, pl.BlockSpec((r, 1), lambda j: (0, 0))],
        out_specs=[blk, blk, blk, blk],
        out_shape=[jax.ShapeDtypeStruct((r, lp), F32)] + [jax.ShapeDtypeStruct((r, lp), BF16)] * 3,
        scratch_shapes=[pltpu.VMEM((r, LANES), F32)],
        compiler_params=_cparams(("arbitrary",)),
    )(pre, b_col)


def _softmax_rows(s_parts, m_old, l_old):
    mx = s_parts[0]
    for sp in s_parts[1:]:
        mx = jnp.maximum(mx, sp)
    m_new = jnp.maximum(m_old, jnp.max(mx, axis=-1, keepdims=True))
    alpha = jnp.exp2(m_old - m_new)
    ps = [jnp.exp2(sp - m_new) for sp in s_parts]
    tot = ps[0]
    for p in ps[1:]:
        tot = tot + p
    return ps, m_new, alpha * l_old + tot, alpha


def _sb_kernel(q_ref, k_ref, v_ref, o_ref, s_scr, b_scr, hi_scr, lo_scr, p_scr, tri, acc, carry,
               *, off, tq, tk, heads):
    qi = pl.program_id(1)
    q0 = off + qi * tq
    kb_last = (q0 + tq - 2) // tk
    ng = tk // LANES
    nchunk = tq // BF16_ROWS

    acc[...] = jnp.zeros_like(acc)
    carry[...] = jnp.zeros_like(carry)
    row = lax.broadcasted_iota(jnp.int32, (tk, tk), 0)
    col = lax.broadcasted_iota(jnp.int32, (tk, tk), 1)
    tri[...] = jnp.where(row > col, 1.0, 0.0).astype(BF16)

    def block(state):
        kb, _ = state
        k0 = pl.multiple_of(kb * tk, tk)
        kpos = k0 + lax.broadcasted_iota(jnp.int32, (1, LANES), 1)
        for h in range(heads):
            sl = slice(h * HEAD_DIM, (h + 1) * HEAD_DIM)
            s_scr[h] = lax.dot_general(q_ref[0, :, sl], k_ref[0, pl.ds(k0, tk), sl], NT_DIMS,
                                       preferred_element_type=F32)
        for h in range(heads):
            for c in range(nchunk):
                rows = slice(c * BF16_ROWS, (c + 1) * BF16_ROWS)
                qpos = q0 + c * BF16_ROWS + lax.broadcasted_iota(jnp.int32, (BF16_ROWS, 1), 0)
                for g in range(ng):
                    ls = slice(g * LANES, (g + 1) * LANES)
                    z = s_scr[h, rows, ls]
                    lsig = jnp.minimum(z, 0.0) - jnp.log2(1.0 + jnp.exp2(-jnp.abs(z)))
                    l1m = jnp.where(kpos + g * LANES < qpos, lsig - z, 0.0)
                    hi = l1m.astype(BF16)
                    hi_scr[h, rows, ls] = hi
                    lo_scr[h, rows, ls] = (l1m - hi.astype(F32)).astype(BF16)
                    s_scr[h, rows, ls] = lsig
        for h in range(heads):
            b_scr[h] = (jnp.dot(hi_scr[h], tri[...], preferred_element_type=F32)
                        + jnp.dot(lo_scr[h], tri[...], preferred_element_type=F32))
        for h in range(heads):
            for c in range(nchunk):
                rows = slice(c * BF16_ROWS, (c + 1) * BF16_ROWS)
                qpos = q0 + c * BF16_ROWS + lax.broadcasted_iota(jnp.int32, (BF16_ROWS, 1), 0)
                run = carry[h, rows, :]
                for g in range(ng):
                    ls = slice(g * LANES, (g + 1) * LANES)
                    a = jnp.exp2(s_scr[h, rows, ls] + b_scr[h, rows, ls] + run)
                    a = jnp.where(kpos + g * LANES < qpos, a, 0.0)
                    p_scr[h, rows, ls] = a.astype(BF16)
                first = (b_scr[h, rows, 0:1] + hi_scr[h, rows, 0:1].astype(F32)
                         + lo_scr[h, rows, 0:1].astype(F32))
                carry[h, rows, :] = run + first
        for h in range(heads):
            sl = slice(h * HEAD_DIM, (h + 1) * HEAD_DIM)
            acc[h] += jnp.dot(p_scr[h], v_ref[0, pl.ds(k0, tk), sl], preferred_element_type=F32)
        return kb - 1, jnp.max(carry[...]) > SB_DEAD_LOG2

    lax.while_loop(lambda st: jnp.logical_and(st[0] >= 0, st[1]), block,
                   (kb_last, jnp.bool_(True)))
    for h in range(heads):
        o_ref[0, :, h * HEAD_DIM:(h + 1) * HEAD_DIM] = acc[h].astype(o_ref.dtype)


def sb_attention(q, k_arr, kc, v_arr, vc, t, length, tq, tk):
    b = q.shape[0]
    lp = k_arr.shape[1]
    w = H_SB * HEAD_DIM
    return pl.pallas_call(
        functools.partial(_sb_kernel, off=length - t, tq=tq, tk=tk, heads=H_SB),
        grid=(b, t // tq),
        in_specs=[pl.BlockSpec((1, tq, w), lambda bi, qi: (bi, qi, 0)),
                  pl.BlockSpec((1, lp, w), lambda bi, qi: (bi, 0, kc)),
                  pl.BlockSpec((1, lp, w), lambda bi, qi: (bi, 0, vc))],
        out_specs=pl.BlockSpec((1, tq, w), lambda bi, qi: (bi, qi, 0)),
        out_shape=jax.ShapeDtypeStruct((b, t, w), BF16),
        scratch_shapes=[pltpu.VMEM((H_SB, tq, tk), F32),
                        pltpu.VMEM((H_SB, tq, tk), F32),
                        pltpu.VMEM((H_SB, tq, tk), BF16),
                        pltpu.VMEM((H_SB, tq, tk), BF16),
                        pltpu.VMEM((H_SB, tq, tk), BF16),
                        pltpu.VMEM((tk, tk), BF16),
                        pltpu.VMEM((H_SB, tq, HEAD_DIM), F32),
                        pltpu.VMEM((H_SB, tq, LANES), F32)],
        compiler_params=_cparams(("parallel", "arbitrary")),
    )(q, k_arr, v_arr)


def _fox_kernel(q_ref, qb_ref, k_ref, kb_ref, v_ref, o_ref, s_scr, p_scr, acc, m_scr, l_scr, a_scr,
                *, off, tq, tk, heads):
    qi = pl.program_id(1)
    q0 = off + qi * tq
    kb_last = (q0 + tq - 1) // tk
    ng = tk // LANES
    nchunk = tq // BF16_ROWS

    acc[...] = jnp.zeros_like(acc)
    m_scr[...] = jnp.full_like(m_scr, M_INIT)
    l_scr[...] = jnp.zeros_like(l_scr)

    def block(kb, _, masked):
        k0 = pl.multiple_of(kb * tk, tk)
        kpos = k0 + lax.broadcasted_iota(jnp.int32, (1, LANES), 1)
        for h in range(heads):
            sl = slice(h * HEAD_DIM, (h + 1) * HEAD_DIM)
            q_aug = jnp.concatenate([q_ref[0, :, sl], qb_ref[0, :, sl]], axis=1)
            k_aug = jnp.concatenate([k_ref[0, pl.ds(k0, tk), sl], kb_ref[0, pl.ds(k0, tk), sl]],
                                    axis=1)
            s_scr[h] = lax.dot_general(q_aug, k_aug, NT_DIMS, preferred_element_type=F32)
        for h in range(heads):
            for c in range(nchunk):
                rows = slice(c * BF16_ROWS, (c + 1) * BF16_ROWS)
                parts = [s_scr[h, rows, g * LANES:(g + 1) * LANES] for g in range(ng)]
                if masked:
                    qpos = q0 + c * BF16_ROWS + lax.broadcasted_iota(jnp.int32, (BF16_ROWS, 1), 0)
                    parts = [jnp.where(kpos + g * LANES <= qpos, parts[g], -jnp.inf)
                             for g in range(ng)]
                ps, m_new, l_new, alpha = _softmax_rows(parts, m_scr[h, rows, :], l_scr[h, rows, :])
                m_scr[h, rows, :] = m_new
                l_scr[h, rows, :] = l_new
                a_scr[h, rows, :] = alpha
                for g in range(ng):
                    p_scr[h, rows, g * LANES:(g + 1) * LANES] = ps[g].astype(BF16)
        for h in range(heads):
            sl = slice(h * HEAD_DIM, (h + 1) * HEAD_DIM)
            acc[h] = a_scr[h] * acc[h] + jnp.dot(p_scr[h], v_ref[0, pl.ds(k0, tk), sl],
                                                 preferred_element_type=F32)
        return 0

    n_plain = jnp.minimum((q0 + 1) // tk, kb_last + 1)
    lax.fori_loop(0, n_plain, functools.partial(block, masked=False), 0)
    lax.fori_loop(n_plain, kb_last + 1, functools.partial(block, masked=True), 0)
    for h in range(heads):
        l = jnp.sum(l_scr[h], axis=-1, keepdims=True)
        o_ref[0, :, h * HEAD_DIM:(h + 1) * HEAD_DIM] = (acc[h] / l).astype(o_ref.dtype)


def _fox_bias_cols(parts, first, sign):
    b, r, h, n = parts.shape
    place = np.zeros((h, n + 1, h, HEAD_DIM), np.float32)
    for hd in range(h):
        for j in range(n):
            place[hd, j, hd, first + j] = sign
            place[hd, n, hd, (first + n + j) % (2 * n)] = 1.0
    src = jnp.concatenate([parts, jnp.ones((b, r, h, 1), BF16)], axis=-1)
    out = jnp.einsum('brk,kc->brc', src.reshape(b, r, h * (n + 1)),
                     jnp.asarray(place.reshape(h * (n + 1), h * HEAD_DIM), BF16),
                     preferred_element_type=F32)
    return out.astype(BF16)


def fox_attention(q, q_bias, k_arr, kc, k_bias, v_arr, vc, t, length, tq, tk):
    b = q.shape[0]
    lp = k_arr.shape[1]
    w = H_FOX * HEAD_DIM
    return pl.pallas_call(
        functools.partial(_fox_kernel, off=length - t, tq=tq, tk=tk, heads=H_FOX),
        grid=(b, t // tq),
        in_specs=[pl.BlockSpec((1, tq, w), lambda bi, qi: (bi, qi, 0)),
                  pl.BlockSpec((1, tq, w), lambda bi, qi: (bi, qi, 0)),
                  pl.BlockSpec((1, lp, w), lambda bi, qi: (bi, 0, kc)),
                  pl.BlockSpec((1, lp, w), lambda bi, qi: (bi, 0, 0)),
                  pl.BlockSpec((1, lp, w), lambda bi, qi: (bi, 0, vc))],
        out_specs=pl.BlockSpec((1, tq, w), lambda bi, qi: (bi, qi, 0)),
        out_shape=jax.ShapeDtypeStruct((b, t, w), BF16),
        scratch_shapes=[pltpu.VMEM((H_FOX, tq, tk), F32),
                        pltpu.VMEM((H_FOX, tq, tk), BF16),
                        pltpu.VMEM((H_FOX, tq, HEAD_DIM), F32),
                        pltpu.VMEM((H_FOX, tq, LANES), F32),
                        pltpu.VMEM((H_FOX, tq, LANES), F32),
                        pltpu.VMEM((H_FOX, tq, LANES), F32)],
        compiler_params=_cparams(("parallel", "arbitrary")),
    )(q, q_bias, k_arr, k_bias, v_arr)


MOE_EXPERTS_PER_STEP = 2


def _moe_kernel(*refs, n_mix, n_groups, per_group, final_norm):
    x_ref, mix_refs = refs[0], refs[1:1 + n_mix]
    wo_ref, g_ref, wr_ref, wg_ref, wu_ref, wd_ref, gf_ref, o_ref, hb, comb = refs[1 + n_mix:]
    e = pl.program_id(1)
    n_exp = n_groups * per_group
    lane = lax.broadcasted_iota(jnp.int32, comb.shape, 1)

    @pl.when(e == 0)
    def _():
        x = x_ref[...]
        k0 = 0
        for m_ref in mix_refs:
            k = m_ref.shape[1]
            x = x + jnp.dot(m_ref[...], wo_ref[k0:k0 + k, :], preferred_element_type=F32)
            k0 += k
        h = _rms(x, g_ref[...])
        hb[...] = h.astype(BF16)
        h_hi = hb[...]
        h_lo = (h - h_hi.astype(F32)).astype(BF16)
        w = wr_ref[...]
        w_hi = w.astype(BF16)
        w_lo = (w - w_hi.astype(F32)).astype(BF16)
        logits = (jnp.dot(h_hi, w_hi, preferred_element_type=F32)
                  + jnp.dot(h_hi, w_lo, preferred_element_type=F32)
                  + jnp.dot(h_lo, w_hi, preferred_element_type=F32))
        lane_f = lane.astype(F32)
        gl = jnp.where(lane < n_groups, logits, -jnp.inf)
        gmax = jnp.max(gl, axis=-1, keepdims=True)
        gsel = jnp.min(jnp.where(gl == gmax, lane_f, float(LANES)), axis=-1, keepdims=True)
        p_group = 1.0 / jnp.sum(jnp.where(lane < n_groups, jnp.exp(gl - gmax), 0.0),
                                axis=-1, keepdims=True)
        first = n_groups + gsel * per_group
        el = jnp.where((lane_f >= first) & (lane_f < first + per_group), logits, -jnp.inf)
        v1 = jnp.max(el, axis=-1, keepdims=True)
        i1 = jnp.min(jnp.where(el == v1, lane_f, float(LANES)), axis=-1, keepdims=True)
        el2 = jnp.where(lane_f == i1, -jnp.inf, el)
        v2 = jnp.max(el2, axis=-1, keepdims=True)
        i2 = jnp.min(jnp.where(el2 == v2, lane_f, float(LANES)), axis=-1, keepdims=True)
        ratio = jnp.exp(v2 - v1)
        gate1 = p_group / (1.0 + ratio)
        comb[...] = jnp.where(lane_f == i1, gate1, jnp.where(lane_f == i2, gate1 * ratio, 0.0))
        o_ref[...] = x

    hv = hb[...]
    y = None
    for j in range(MOE_EXPERTS_PER_STEP):
        col = e * MOE_EXPERTS_PER_STEP + j + n_groups
        c = jnp.sum(jnp.where(lane == col, comb[...], 0.0), axis=-1, keepdims=True)
        gate = jnp.dot(hv, wg_ref[j], preferred_element_type=F32)
        up = jnp.dot(hv, wu_ref[j], preferred_element_type=F32)
        act = gate * (1.0 / (1.0 + jnp.exp(-gate))) * up * c
        yj = jnp.dot(act.astype(BF16), wd_ref[j], preferred_element_type=F32)
        y = yj if y is None else y + yj
    o_ref[...] += y

    if final_norm:
        @pl.when(e == n_exp // MOE_EXPERTS_PER_STEP - 1)
        def _():
            o_ref[...] = _rms(o_ref[...], gf_ref[...])


def moe_block(x, mix, w_out, g, w_group, w_router, w_gate, w_up, w_down, g_final=None):
    n, d = x.shape
    n_groups, per_group = w_router.shape[1], w_router.shape[2]
    n_exp, _, f = w_gate.shape
    w_route = jnp.concatenate(
        [w_group, w_router.reshape(d, n_exp),
         jnp.zeros((d, LANES - n_groups - n_exp), F32)], axis=1)
    final_norm = g_final is not None
    gf = (g_final if final_norm else g).reshape(1, d)
    tm = min(1024, n)
    return pl.pallas_call(
        functools.partial(_moe_kernel, n_mix=len(mix), n_groups=n_groups, per_group=per_group,
                          final_norm=final_norm),
        grid=(n // tm, n_exp // MOE_EXPERTS_PER_STEP),
        in_specs=[pl.BlockSpec((tm, d), lambda i, e: (i, 0))]
        + [pl.BlockSpec((tm, m.shape[1]), lambda i, e: (i, 0)) for m in mix]
        + [pl.BlockSpec(w_out.shape, lambda i, e: (0, 0)),
                  pl.BlockSpec((1, d), lambda i, e: (0, 0)),
                  pl.BlockSpec((d, LANES), lambda i, e: (0, 0)),
                  pl.BlockSpec((MOE_EXPERTS_PER_STEP, d, f), lambda i, e: (e, 0, 0)),
                  pl.BlockSpec((MOE_EXPERTS_PER_STEP, d, f), lambda i, e: (e, 0, 0)),
                  pl.BlockSpec((MOE_EXPERTS_PER_STEP, f, d), lambda i, e: (e, 0, 0)),
                  pl.BlockSpec((1, d), lambda i, e: (0, 0))],
        out_specs=pl.BlockSpec((tm, d), lambda i, e: (i, 0)),
        out_shape=jax.ShapeDtypeStruct((n, d), F32),
        scratch_shapes=[pltpu.VMEM((tm, d), BF16), pltpu.VMEM((tm, LANES), F32)],
        compiler_params=_cparams(("parallel", "arbitrary")),
    )(x, *mix, w_out, g.reshape(1, d), w_route, w_gate.astype(BF16), w_up.astype(BF16),
      w_down.astype(BF16), gf)


def _rope_tables(pos, head_dim, rows, live_lanes=LANES):
    half = head_dim // 2
    inv_freq = ROPE_THETA ** (-jnp.arange(half, dtype=F32) / half)
    ang = pos.astype(F32)[:, None] * inv_freq[None, :]
    cos, sin = jnp.cos(ang), jnp.sin(ang)
    reps = LANES // head_dim
    cos = jnp.tile(jnp.concatenate([cos, cos], axis=1), (1, reps))
    sin = jnp.tile(jnp.concatenate([-sin, sin], axis=1), (1, reps))
    live = jnp.arange(LANES) < live_lanes
    cos = jnp.where(live, cos, 1.0)
    sin = jnp.where(live, sin, 0.0)
    reps_rows = max(rows // pos.shape[0], 1)
    return jnp.tile(cos, (reps_rows, 1)), jnp.tile(sin, (reps_rows, 1))


def _code_to_float(code):
    return pltpu.bitcast(code ^ ((code >> 31) & 0x7FFFFFFF), F32)


def _dsa_reach(q_first, tq, length):
    return jnp.minimum(((q_first + tq - 1) // CHUNK + 1) * CHUNK, length)


def _dsa_select(qi_ref, wt_ref, ki_ref, bias, sc, *, off, length, tq, ksel, cw):
    qb = pl.program_id(1)
    q0 = off + qb * tq
    qchunk = (q0 + lax.broadcasted_iota(jnp.int32, (1, SELECT_Q), 1)) // CHUNK
    nck = (_dsa_reach(q0, tq, length) + cw - 1) // cw
    w = wt_ref[0] * (H_IDX ** -0.5 * D_IDX ** -0.5)
    kf = float(ksel)

    def fill(c, _):
        for s in range(cw // LANES):
            k0 = pl.multiple_of(c * cw, cw) + s * LANES
            kk = ki_ref[0, pl.ds(k0, LANES), :]
            score = jnp.zeros((LANES, SELECT_Q), F32)
            for h in range(H_IDX):
                lg = lax.dot_general(kk, qi_ref[0, :, h * D_IDX:(h + 1) * D_IDX], NT_DIMS,
                                     preferred_element_type=F32)
                score = score + jnp.maximum(lg, 0.0) * w[h:h + 1, :]
            kpos = k0 + lax.broadcasted_iota(jnp.int32, (LANES, 1), 0)
            admissible = (kpos // CHUNK <= qchunk) & (kpos < length)
            sc[c, s * LANES:(s + 1) * LANES, :] = jnp.where(admissible, score + 0.0, -jnp.inf)
        return 0

    lax.fori_loop(0, nck, fill, 0)

    def count(t, strict=False):
        def body(c, tot):
            x = sc[c].reshape(cw // COUNT_ROWS, COUNT_ROWS, SELECT_Q)
            hit = (x > t) if strict else (x >= t)
            return tot + jnp.sum(jnp.where(hit, 1.0, 0.0), axis=0)
        tot = lax.fori_loop(0, nck, body, jnp.zeros((COUNT_ROWS, SELECT_Q), F32))
        return jnp.sum(tot, axis=0, keepdims=True)

    code0 = jnp.where(count(jnp.zeros((1, SELECT_Q), F32)) >= kf, 0, INT32_MIN).astype(jnp.int32)

    def bit_step(i, code):
        cand = code + lax.shift_left(jnp.int32(1), 30 - i)
        return jnp.where(count(_code_to_float(cand)) >= kf, cand, code)

    code = lax.fori_loop(0, 31, bit_step, code0)
    thr = _code_to_float(jnp.maximum(code, NEG_INF_CODE + 1))
    real = lax.broadcasted_iota(jnp.int32, (1, SELECT_Q), 1) < tq
    n_ge = jnp.where(real, count(thr), 0.0)
    n_gt = count(thr, strict=True)
    tied = jnp.max(n_ge) > kf

    @pl.when(jnp.logical_not(tied))
    def _():
        def emit(c, _):
            bias[c] = jnp.where(sc[c] >= thr, 0.0, -jnp.inf).T.astype(bias.dtype)
            return 0
        lax.fori_loop(0, nck, emit, 0)

    @pl.when(tied)
    def _():
        need = kf - n_gt
        row = lax.broadcasted_iota(jnp.int32, (LANES, LANES), 0)
        col = lax.broadcasted_iota(jnp.int32, (LANES, LANES), 1)
        earlier = jnp.where(col < row, 1.0, 0.0).astype(BF16)

        def emit(c, run):
            for s in range(cw // LANES):
                x = sc[c, s * LANES:(s + 1) * LANES, :]
                eq = jnp.where(x == thr, 1.0, 0.0)
                rank = run + jnp.dot(earlier, eq.astype(BF16), preferred_element_type=F32)
                sel = (x > thr) | ((x == thr) & (rank < need))
                bias[c, :, s * LANES:(s + 1) * LANES] = (
                    jnp.where(sel, 0.0, -jnp.inf).T.astype(bias.dtype))
                run = run + jnp.sum(eq, axis=0, keepdims=True)
            return run
        lax.fori_loop(0, nck, emit, jnp.zeros((1, SELECT_Q), F32))


def _dsa_attend(q_ref, k_ref, v_ref, bias, o_ref, qs, s_scr, p_scr, acc, m_scr, l_scr, a_scr,
                *, off, length, tq, tk):
    qb = pl.program_id(1)
    rep = H_DSA // KV_DSA
    nkb = (_dsa_reach(off + qb * tq, tq, length) + tk - 1) // tk
    ng = tk // LANES
    nchunk = tq // BF16_ROWS

    for g in range(KV_DSA):
        for r in range(rep):
            hd = g * rep + r
            qs[g, r * tq:(r + 1) * tq, :] = q_ref[0, :, hd * HEAD_DIM:(hd + 1) * HEAD_DIM]
    acc[...] = jnp.zeros_like(acc)
    m_scr[...] = jnp.full_like(m_scr, M_INIT)
    l_scr[...] = jnp.zeros_like(l_scr)

    def scores(kb, dst):
        k0 = pl.multiple_of(kb * tk, tk)
        for g in range(KV_DSA):
            sl = slice(g * HEAD_DIM, (g + 1) * HEAD_DIM)
            dst[g] = lax.dot_general(qs[g], k_ref[0, pl.ds(k0, tk), sl], NT_DIMS,
                                     preferred_element_type=F32)

    def consume(kb, src):
        k0 = pl.multiple_of(kb * tk, tk)
        for c in range(nchunk):
            bias_c = [bias[kb, c * BF16_ROWS:(c + 1) * BF16_ROWS,
                           gl * LANES:(gl + 1) * LANES].astype(F32) for gl in range(ng)]
            for g in range(KV_DSA):
                for r in range(rep):
                    rows = slice(r * tq + c * BF16_ROWS, r * tq + (c + 1) * BF16_ROWS)
                    parts = [src[g, rows, gl * LANES:(gl + 1) * LANES] + bias_c[gl]
                             for gl in range(ng)]
                    ps, m_new, l_new, alpha = _softmax_rows(parts, m_scr[g, rows, :],
                                                            l_scr[g, rows, :])
                    m_scr[g, rows, :] = m_new
                    l_scr[g, rows, :] = l_new
                    a_scr[g, rows, :] = alpha
                    for gl in range(ng):
                        p_scr[g, rows, gl * LANES:(gl + 1) * LANES] = ps[gl].astype(BF16)
        for g in range(KV_DSA):
            sl = slice(g * HEAD_DIM, (g + 1) * HEAD_DIM)
            acc[g] = a_scr[g] * acc[g] + jnp.dot(p_scr[g], v_ref[0, pl.ds(k0, tk), sl],
                                                 preferred_element_type=F32)

    def block(kb, _):
        scores(kb, s_scr)
        consume(kb, s_scr)
        return 0

    lax.fori_loop(0, nkb, block, 0)
    for g in range(KV_DSA):
        o = acc[g] / jnp.sum(l_scr[g], axis=-1, keepdims=True)
        for r in range(rep):
            hd = g * rep + r
            o_ref[0, :, hd * HEAD_DIM:(hd + 1) * HEAD_DIM] = (
                o[r * tq:(r + 1) * tq]).astype(o_ref.dtype)


def _dsa_kernel(qi_ref, wt_ref, ki_ref, q_ref, k_ref, v_ref, o_ref, sc, bias, qs, s_scr, p_scr, acc,
                m_scr, l_scr, a_scr, *, off, length, tq, tk, ksel):
    _dsa_select(qi_ref, wt_ref, ki_ref, bias, sc, off=off, length=length, tq=tq, ksel=ksel, cw=tk)
    _dsa_attend(q_ref, k_ref, v_ref, bias, o_ref, qs, s_scr, p_scr, acc, m_scr, l_scr, a_scr,
                off=off, length=length, tq=tq, tk=tk)


def dsa_sparse_attention(qi, w_idx, ki, q, k_arr, v_arr, t, length, tq, tk):
    assert tq == SELECT_Q or (t == tq and tq < SELECT_Q)
    b, lp, _ = ki.shape
    t_pad = _round_up(t, SELECT_Q)
    qi = jnp.pad(qi, ((0, 0), (0, t_pad - t), (0, 0)))
    w_t = jnp.pad(jnp.swapaxes(w_idx, 1, 2), ((0, 0), (0, 0), (0, t_pad - t)))
    wq = H_DSA * HEAD_DIM
    wk = KV_DSA * HEAD_DIM
    rep = H_DSA // KV_DSA
    nch = lp // tk
    per_q = lambda w: pl.BlockSpec((1, tq, w), lambda bi, qb: (bi, qb, 0))
    per_b = lambda w: pl.BlockSpec((1, lp, w), lambda bi, qb: (bi, 0, 0))
    return pl.pallas_call(
        functools.partial(_dsa_kernel, off=length - t, length=length, tq=tq, tk=tk,
                          ksel=min(TOPK_MAX, length // 4)),
        grid=(b, t // tq),
        in_specs=[pl.BlockSpec((1, SELECT_Q, H_IDX * D_IDX), lambda bi, qb: (bi, qb, 0)),
                  pl.BlockSpec((1, H_IDX, SELECT_Q), lambda bi, qb: (bi, 0, qb)),
                  per_b(D_IDX), per_q(wq), per_b(wk), per_b(wk)],
        out_specs=per_q(wq),
        out_shape=jax.ShapeDtypeStruct((b, t, wq), BF16),
        scratch_shapes=[pltpu.VMEM((nch, tk, SELECT_Q), F32),
                        pltpu.VMEM((nch, SELECT_Q, tk), BF16),
                        pltpu.VMEM((KV_DSA, rep * tq, HEAD_DIM), BF16),
                        pltpu.VMEM((KV_DSA, rep * tq, tk), F32),
                        pltpu.VMEM((KV_DSA, rep * tq, tk), BF16),
                        pltpu.VMEM((KV_DSA, rep * tq, HEAD_DIM), F32),
                        pltpu.VMEM((KV_DSA, rep * tq, LANES), F32),
                        pltpu.VMEM((KV_DSA, rep * tq, LANES), F32),
                        pltpu.VMEM((KV_DSA, rep * tq, LANES), F32)],
        compiler_params=_cparams(("parallel", "arbitrary")),
    )(qi, w_t, ki, q, k_arr, v_arr)


def _with_past(past, new, lp):
    b = new.shape[0]
    parts = [] if past is None else [past.reshape(b, past.shape[1], -1).astype(BF16)]
    parts.append(new.astype(BF16))
    rows = sum(p.shape[1] for p in parts)
    if rows < lp:
        parts.append(jnp.zeros((b, lp - rows, new.shape[2]), BF16))
    return parts[0] if len(parts) == 1 else jnp.concatenate(parts, axis=1)


def _mixer_ab(x2, b, t, past, g, w_in, b_forget):
    wsb = H_SB * HEAD_DIM
    wfx = H_FOX * HEAD_DIM
    qscale = LOG2E * HEAD_DIM ** -0.5
    c_fx = 3 * wsb
    c_gate = c_fx + 3 * wfx
    plan = [Segment(0, wsb, bf16_scale=qscale),
            Segment(wsb, wsb, f32=True, bf16_scale=1.0, heads_as_rows=True),
            Segment(2 * wsb, wsb, f32=True, bf16_scale=1.0, heads_as_rows=True),
            Segment(c_fx, wfx, bf16_scale=qscale),
            Segment(c_fx + wfx, wfx, f32=True, bf16_scale=1.0, heads_as_rows=True),
            Segment(c_fx + 2 * wfx, wfx, f32=True, bf16_scale=1.0, heads_as_rows=True),
            Segment(c_gate, LANES, f32=True)]
    w_pad = jnp.pad(w_in, ((0, 0), (0, c_gate + LANES - w_in.shape[1]))).astype(BF16)
    (q_sb, k_sb, k_sb_b, v_sb, v_sb_b, q_fx, k_fx, k_fx_b, v_fx, v_fx_b, gate) = norm_project(
        x2, g, w_pad, plan)
    rows = tuple(a.reshape(b, t, -1, HEAD_DIM) for a in (k_sb, v_sb, k_fx, v_fx))
    f_logit = gate.reshape(b, t, LANES)[:, :, :H_FOX]

    p = 0 if past is None else past[0].shape[1]
    length = p + t
    lp = _round_up(length, KEY_PAD)
    new_t = jnp.swapaxes(f_logit, 1, 2)
    parts = [new_t] if past is None else [jnp.swapaxes(past[4].astype(F32), 1, 2), new_t]
    if length < lp:
        parts.append(jnp.zeros((b, H_FOX, lp - length), F32))
    pre = (parts[0] if len(parts) == 1 else jnp.concatenate(parts, axis=2)).reshape(b * H_FOX, lp)
    b_col = jnp.tile(b_forget.astype(F32), b).reshape(b * H_FOX, 1)
    logf_t, *cum_parts = logf_cumsum(pre, b_col, p, length)
    logf = jnp.swapaxes(logf_t.reshape(b, H_FOX, lp)[:, :, p:length], 1, 2)

    f_parts = jnp.stack([jnp.swapaxes(c.reshape(b, H_FOX, lp), 1, 2) for c in cum_parts],
                        axis=-1)
    q_bias = _fox_bias_cols(f_parts[:, p:length], 0, 1.0)
    k_bias = _fox_bias_cols(f_parts, f_parts.shape[-1], -1.0)

    as3 = lambda a: a.reshape(b, t, -1)
    ks, vs, kf, vf = (_with_past(None if past is None else past[i], as3(a), lp)
                      for i, a in enumerate((k_sb_b, v_sb_b, k_fx_b, v_fx_b)))
    tq = min(256, t)
    o_sb = sb_attention(as3(q_sb), ks, 0, vs, 0, t, length, tq, 256)
    o_fx = fox_attention(as3(q_fx), q_bias, kf, 0, k_bias, vf, 0, t, length, tq, 512)
    return [o_sb.reshape(b * t, wsb), o_fx.reshape(b * t, wfx)], rows + (logf,)


def _mixer_dsa(x2, b, t, past, g, w_in):
    n = b * t
    wq = H_DSA * HEAD_DIM
    wk = KV_DSA * HEAD_DIM
    wi = H_IDX * D_IDX
    p = 0 if past is None else past[0].shape[1]
    length = p + t
    lp = _round_up(length, KEY_PAD)
    pos = p + jnp.arange(t)
    tab_rows = max(t, min(256, n))
    tables = [_rope_tables(pos, HEAD_DIM, tab_rows),
              _rope_tables(pos, D_IDX, tab_rows),
              _rope_tables(pos, D_IDX, tab_rows, live_lanes=D_IDX)]
    c_k, c_v, c_qi, c_ki = wq, wq + wk, wq + 2 * wk, wq + 2 * wk + wi
    plan = [Segment(0, wq, rope=0, half=HEAD_DIM // 2, bf16_scale=LOG2E * HEAD_DIM ** -0.5),
            Segment(c_k, wk, rope=0, half=HEAD_DIM // 2, f32=True, bf16_scale=1.0,
                    heads_as_rows=True),
            Segment(c_v, wk, f32=True, bf16_scale=1.0, heads_as_rows=True),
            Segment(c_qi, wi, rope=1, half=D_IDX // 2, bf16_scale=1.0),
            Segment(c_ki, LANES, rope=2, half=D_IDX // 2, f32=True, bf16_scale=1.0)]
    w_pad = jnp.pad(w_in, ((0, 0), (0, c_ki + LANES - w_in.shape[1]))).astype(BF16)
    q_b, k_f, k_b, v_f, v_b, qi_b, kw_f, kw_b = norm_project(x2, g, w_pad, plan, tables)
    as3 = lambda a: a.reshape(b, t, -1)
    k_rows = k_f.reshape(b, t, KV_DSA, HEAD_DIM)
    v_rows = v_f.reshape(b, t, KV_DSA, HEAD_DIM)
    ki_rows = as3(kw_f)[:, :, :D_IDX]
    w_idx = as3(kw_f)[:, :, D_IDX:D_IDX + H_IDX]

    tk = KEY_PAD
    ki_all = _with_past(None if past is None else past[2], as3(kw_b)[:, :, :D_IDX], lp)
    k_all = _with_past(None if past is None else past[0], as3(k_b), lp)
    v_all = _with_past(None if past is None else past[1], as3(v_b), lp)
    o = dsa_sparse_attention(as3(qi_b), w_idx, ki_all, as3(q_b), k_all, v_all, t, length,
                             min(128, t), tk)
    return [o.reshape(n, wq)], (k_rows, v_rows, ki_rows)


def _trunk(x, past_ab, past_dsa, norm_mix, norm_ffn, norm_final, w_in_ab, b_forget, w_out_ab,
           w_in_dsa, w_out_dsa, moe_w_group, moe_w_router, moe_w_gate, moe_w_up, moe_w_down):
    b, t, d = x.shape
    x2 = x.reshape(b * t, d)
    mix, rows_ab = _mixer_ab(x2, b, t, past_ab, norm_mix[0], w_in_ab, b_forget)
    x2 = moe_block(x2, mix, w_out_ab.astype(BF16), norm_ffn[0], moe_w_group[0], moe_w_router[0],
                   moe_w_gate[0], moe_w_up[0], moe_w_down[0])
    mix, rows_dsa = _mixer_dsa(x2, b, t, past_dsa, norm_mix[1], w_in_dsa)
    y = moe_block(x2, mix, w_out_dsa.astype(BF16), norm_ffn[1], moe_w_group[1], moe_w_router[1],
                  moe_w_gate[1], moe_w_up[1], moe_w_down[1], g_final=norm_final)
    return y.reshape(b, t, d), rows_ab, rows_dsa


def kernel(x_prompt, x_sample, cache_sb_k, cache_sb_v, cache_fox_k, cache_fox_v, cache_fox_logf,
           cache_dsa_k, cache_dsa_v, cache_dsa_idx_k, norm_mix, norm_ffn, norm_final,
           w_in_ab, b_forget, w_out_ab, w_in_dsa, w_out_dsa,
           moe_w_group, moe_w_router, moe_w_gate, moe_w_up, moe_w_down):
    weights = (norm_mix, norm_ffn, norm_final, w_in_ab, b_forget, w_out_ab, w_in_dsa, w_out_dsa,
               moe_w_group, moe_w_router, moe_w_gate, moe_w_up, moe_w_down)
    y_p, ab_p, dsa_p = _trunk(x_prompt, None, None, *weights)
    y_s, ab_s, dsa_s = _trunk(
        x_sample, (cache_sb_k, cache_sb_v, cache_fox_k, cache_fox_v, cache_fox_logf),
        (cache_dsa_k, cache_dsa_v, cache_dsa_idx_k), *weights)
    return (y_p, y_s) + ab_p + dsa_p + ab_s + dsa_s
```
